```python
import math
import jax, jax.numpy as jnp
from jax import lax
import numpy as np

D_MODEL = 1024
BATCH = 2
SEQ = 16384
DEPTH = 1
DEC_BATCH = 16
DEC_SEQ = 64
PAST_LEN = 1024

CHUNK = 64
Q_BLOCK = 128
RMS_EPS = 1e-6

GDN_HEADS = 4
GDN_DK = 128
GDN_DV = 128
GDN_QK_DIM = GDN_HEADS * GDN_DK
GDN_V_DIM = GDN_HEADS * GDN_DV
GDN_CONV_DIM = 2 * GDN_QK_DIM + GDN_V_DIM
CONV_W = 4

MLA_HEADS = 4
Q_LORA = 512
KV_LORA = 256
QK_NOPE = 128
QK_ROPE = 64
V_HEAD = 128
QK_HEAD = QK_NOPE + QK_ROPE
ROPE_THETA = 10000.0

MIX_DIM = GDN_V_DIM + MLA_HEADS * V_HEAD
IN_COLS = GDN_CONV_DIM + GDN_V_DIM + GDN_HEADS + GDN_HEADS + Q_LORA + KV_LORA + QK_ROPE

N_GROUPS = 4
EXPERTS_PER_GROUP = 8
N_EXPERTS = N_GROUPS * EXPERTS_PER_GROUP
TOP_K = 2
D_EXPERT = 256

kernel_name = 'hybrid_gdn_mla_hmoe_stream_step'


def rms_norm(x, w):
    xf = x.astype(jnp.float32)
    xf = xf * lax.rsqrt(jnp.mean(xf * xf, axis=-1, keepdims=True) + RMS_EPS)
    return (xf * w.astype(jnp.float32)).astype(x.dtype)


def l2_normalize(x):
    xf = x.astype(jnp.float32)
    return xf * lax.rsqrt(jnp.sum(xf * xf, axis=-1, keepdims=True) + RMS_EPS)


def split_in_proj(proj):
    sizes = (GDN_CONV_DIM, GDN_V_DIM, GDN_HEADS, GDN_HEADS, Q_LORA, KV_LORA, QK_ROPE)
    offs = [int(o) for o in np.cumsum(sizes)[:-1]]
    return jnp.split(proj, offs, axis=-1)


def causal_conv(x, hist, w):
    seq_len = x.shape[1]
    xp = jnp.concatenate([hist.astype(x.dtype), x], axis=1)
    y = xp[:, 0:seq_len] * w[0]
    for i in range(1, CONV_W):
        y = y + xp[:, i:i + seq_len] * w[i]
    return jax.nn.silu(y), xp[:, -(CONV_W - 1):]


def gated_delta_rule(q, k, v, g, beta, s0):
    bsz, seq_len, nh, _ = q.shape
    c = CHUNK if seq_len % CHUNK == 0 else seq_len
    n = seq_len // c
    def blk(t):
        return t.reshape(bsz, n, c, nh, t.shape[-1]).transpose(1, 0, 3, 2, 4)
    def blk_s(t):
        return t.reshape(bsz, n, c, nh).transpose(1, 0, 3, 2)
    qc, kc, vc = blk(q), blk(k), blk(v)
    bc = blk_s(beta)
    gcum = jnp.cumsum(blk_s(g), axis=-1)
    tril = jnp.tril(jnp.ones((c, c), bool))
    strict = jnp.tril(jnp.ones((c, c), bool), -1)
    decay = jnp.exp(jnp.where(tril, gcum[..., :, None] - gcum[..., None, :], -jnp.inf))
    kb = kc * bc[..., None]
    lmat = jnp.where(strict, jnp.einsum('nbhid,nbhjd->nbhij', kb, kc) * decay, 0.0)
    amat = jnp.eye(c, dtype=lmat.dtype) + lmat
    rhs = jnp.concatenate([vc * bc[..., None], kb * jnp.exp(gcum)[..., None]], axis=-1)
    sol = lax.linalg.triangular_solve(amat, rhs, left_side=True, lower=True, unit_diagonal=True)
    u_val, k_cd = sol[..., :GDN_DV], sol[..., GDN_DV:]
    attn = jnp.einsum('nbhid,nbhjd->nbhij', qc, kc) * decay
    q_dec = qc * jnp.exp(gcum)[..., None]
    g_end = gcum[..., -1]
    k_end = kc * jnp.exp(g_end[..., None] - gcum)[..., None]

    def step(s, xs):
        u_c, kcd_c, attn_c, qd_c, kend_c, gend_c = xs
        u = u_c - jnp.einsum('bhck,bhkv->bhcv', kcd_c, s)
        o = jnp.einsum('bhck,bhkv->bhcv', qd_c, s) + jnp.einsum('bhij,bhjv->bhiv', attn_c, u)
        s = s * jnp.exp(gend_c)[..., None, None] + jnp.einsum('bhck,bhcv->bhkv', kend_c, u)
        return s, o

    s_final, o = lax.scan(step, s0, (u_val, k_cd, attn, q_dec, k_end, g_end))
    o = o.transpose(1, 0, 3, 2, 4).reshape(bsz, seq_len, nh, GDN_DV)
    return o, s_final


def apply_rope(x, pos):
    half = QK_ROPE // 2
    inv = ROPE_THETA ** (-jnp.arange(half, dtype=jnp.float32) / half)
    ang = pos.astype(jnp.float32)[:, None] * inv[None, :]
    cos, sin = jnp.cos(ang)[:, None, :], jnp.sin(ang)[:, None, :]
    xf = x.astype(jnp.float32)
    x1, x2 = xf[..., :half], xf[..., half:]
    return jnp.concatenate([x1 * cos - x2 * sin, x2 * cos + x1 * sin], axis=-1).astype(x.dtype)


def chunk_causal_attention(q, k, v, q_pos, k_pos):
    scale = QK_HEAD ** -0.5
    k_chunk = k_pos // CHUNK

    def attend(qb, qpb):
        s = jnp.einsum('bqhd,bkhd->bhqk', qb, k).astype(jnp.float32) * scale
        mask = k_chunk[None, :] <= (qpb // CHUNK)[:, None]
        p = jax.nn.softmax(jnp.where(mask, s, -jnp.inf), axis=-1)
        return jnp.einsum('bhqk,bkhd->bqhd', p.astype(v.dtype), v)

    bsz, n_q = q.shape[:2]
    if n_q > Q_BLOCK and n_q % Q_BLOCK == 0:
        nb = n_q // Q_BLOCK
        qb = jnp.moveaxis(q.reshape(bsz, nb, Q_BLOCK, MLA_HEADS, QK_HEAD), 1, 0)
        pb = q_pos.reshape(nb, Q_BLOCK)
        o = lax.map(lambda a: attend(a[0], a[1]), (qb, pb))
        return jnp.moveaxis(o, 0, 1).reshape(bsz, n_q, MLA_HEADS, V_HEAD)
    return attend(q, q_pos)


def hierarchical_moe(h, w_group_router, b_group_router, w_expert_router, b_expert_router, w_exp_gate, w_exp_up, w_exp_down):
    bsz, seq_len, d = h.shape
    t = h.reshape(bsz * seq_len, d)
    grp_prob = jax.nn.softmax((t @ w_group_router).astype(jnp.float32) + b_group_router.astype(jnp.float32), axis=-1)
    grp_p, grp_idx = lax.top_k(grp_prob, 1)
    e_logits = ((t @ w_expert_router).astype(jnp.float32) + b_expert_router.astype(jnp.float32)).reshape(-1, N_GROUPS, EXPERTS_PER_GROUP)
    sel = jnp.take_along_axis(e_logits, grp_idx[:, :, None], axis=1)[:, 0]
    top_v, top_i = lax.top_k(jax.nn.softmax(sel, axis=-1), TOP_K)
    top_v = top_v / jnp.sum(top_v, axis=-1, keepdims=True)
    within = jnp.sum(jax.nn.one_hot(top_i, EXPERTS_PER_GROUP, dtype=jnp.float32) * top_v[..., None], axis=1)
    combine = jax.nn.one_hot(grp_idx[:, 0], N_GROUPS, dtype=jnp.float32)[:, :, None] * (grp_p * within)[:, None, :]
    y = jnp.zeros_like(t)
    for gi in range(N_GROUPS):
        e0 = gi * EXPERTS_PER_GROUP
        e1 = e0 + EXPERTS_PER_GROUP
        hid = jax.nn.silu(jnp.einsum('td,edf->tef', t, w_exp_gate[e0:e1])) * jnp.einsum('td,edf->tef', t, w_exp_up[e0:e1])
        hid = hid * combine[:, gi, :, None].astype(t.dtype)
        y = y + jnp.einsum('tef,efd->td', hid, w_exp_down[e0:e1])
    return y.reshape(bsz, seq_len, d)


def hybrid_layer(x, conv_hist, s0, ckv_past, kpe_past, p):
    bsz, seq_len, _ = x.shape
    past = ckv_past.shape[1]
    f32 = jnp.float32
    h = rms_norm(x, p['ln_mix_w'])
    qkv, z, b_raw, a_raw, cq, ckv_raw, kpe = split_in_proj(h @ p['w_in'])

    qkv, conv_new = causal_conv(qkv, conv_hist, p['gdn_conv_w'])
    gq, gk, gv = jnp.split(qkv, [GDN_QK_DIM, 2 * GDN_QK_DIM], axis=-1)
    gq = l2_normalize(gq.reshape(bsz, seq_len, GDN_HEADS, GDN_DK)) * (GDN_DK ** -0.5)
    gk = l2_normalize(gk.reshape(bsz, seq_len, GDN_HEADS, GDN_DK))
    gv = gv.reshape(bsz, seq_len, GDN_HEADS, GDN_DV).astype(f32)
    beta = jax.nn.sigmoid(b_raw.astype(f32))
    g = -jnp.exp(p['gdn_a_log'].astype(f32)) * jax.nn.softplus(a_raw.astype(f32) + p['gdn_dt_bias'].astype(f32))
    go, s_new = gated_delta_rule(gq, gk, gv, g, beta, s0.astype(f32))
    go = rms_norm(go, p['gdn_norm_w']) * jax.nn.silu(z.reshape(bsz, seq_len, GDN_HEADS, GDN_DV).astype(f32))
    gdn_out = go.reshape(bsz, seq_len, GDN_V_DIM).astype(x.dtype)

    mq = (rms_norm(cq, p['mla_q_a_norm_w']) @ p['mla_w_uq']).reshape(bsz, seq_len, MLA_HEADS, QK_HEAD)
    ckv = rms_norm(ckv_raw, p['mla_kv_a_norm_w'])
    ckv_all = jnp.concatenate([ckv_past.astype(ckv.dtype), ckv], axis=1)
    kpe_all = jnp.concatenate([kpe_past.astype(kpe.dtype), kpe], axis=1)
    n_keys = past + seq_len
    kv = (ckv_all @ p['mla_w_ukv']).reshape(bsz, n_keys, MLA_HEADS, QK_NOPE + V_HEAD)
    k_nope, mv = kv[..., :QK_NOPE], kv[..., QK_NOPE:]
    mk = jnp.concatenate([k_nope, jnp.broadcast_to(kpe_all[:, :, None, :], (bsz, n_keys, MLA_HEADS, QK_ROPE))], axis=-1)
    mq = rms_norm(mq, p['mla_q_norm_w'])
    mk = rms_norm(mk, p['mla_k_norm_w'])
    q_pos = past + jnp.arange(seq_len, dtype=jnp.int32)
    k_pos = jnp.arange(n_keys, dtype=jnp.int32)
    mq = jnp.concatenate([mq[..., :QK_NOPE], apply_rope(mq[..., QK_NOPE:], q_pos)], axis=-1)
    mk = jnp.concatenate([mk[..., :QK_NOPE], apply_rope(mk[..., QK_NOPE:], k_pos)], axis=-1)
    mla_out = chunk_causal_attention(mq, mk, mv, q_pos, k_pos).reshape(bsz, seq_len, MLA_HEADS * V_HEAD)

    x = x + jnp.concatenate([gdn_out, mla_out], axis=-1) @ p['w_out']
    x = x + hierarchical_moe(rms_norm(x, p['ln_ffn_w']), p['w_group_router'], p['b_group_router'],
                             p['w_expert_router'], p['b_expert_router'], p['w_exp_gate'], p['w_exp_up'], p['w_exp_down'])
    return x, ckv, kpe, s_new.astype(s0.dtype), conv_new


def setup_inputs(seed: int = 0) -> dict:
    key = jax.random.key(seed)
    keys = iter(jax.random.split(key, 48))

    def nrm(shape, scale=1.0):
        return jax.random.normal(next(keys), shape, jnp.float32) * scale

    def gain(n):
        return 1.0 + 0.01 * nrm((DEPTH, n))

    dt = jnp.exp(jax.random.uniform(next(keys), (DEPTH, GDN_HEADS), jnp.float32, math.log(1e-3), math.log(1e-1)))
    a_init = jax.random.uniform(next(keys), (DEPTH, GDN_HEADS), jnp.float32, 1.0, 16.0)
    return {
        'x_prompt': nrm((BATCH, SEQ, D_MODEL)),
        'x_sample': nrm((DEC_BATCH, DEC_SEQ, D_MODEL)),
        'cache_mla_ckv': nrm((DEPTH, DEC_BATCH, PAST_LEN, KV_LORA)),
        'cache_mla_kpe': nrm((DEPTH, DEC_BATCH, PAST_LEN, QK_ROPE)),
        'state_gdn': nrm((DEPTH, DEC_BATCH, GDN_HEADS, GDN_DK, GDN_DV), 0.5),
        'state_gdn_conv': nrm((DEPTH, DEC_BATCH, CONV_W - 1, GDN_CONV_DIM)),
        'ln_mix_w': gain(D_MODEL),
        'w_in': nrm((DEPTH, D_MODEL, IN_COLS), D_MODEL ** -0.5),
        'gdn_conv_w': nrm((DEPTH, CONV_W, GDN_CONV_DIM), CONV_W ** -0.5),
        'gdn_a_log': jnp.log(a_init),
        'gdn_dt_bias': dt + jnp.log(-jnp.expm1(-dt)),
        'gdn_norm_w': gain(GDN_DV),
        'mla_q_a_norm_w': gain(Q_LORA),
        'mla_w_uq': nrm((DEPTH, Q_LORA, MLA_HEADS * QK_HEAD), Q_LORA ** -0.5),
        'mla_kv_a_norm_w': gain(KV_LORA),
        'mla_w_ukv': nrm((DEPTH, KV_LORA, MLA_HEADS * (QK_NOPE + V_HEAD)), KV_LORA ** -0.5),
        'mla_q_norm_w': gain(QK_HEAD),
        'mla_k_norm_w': gain(QK_HEAD),
        'w_out': nrm((DEPTH, MIX_DIM, D_MODEL), MIX_DIM ** -0.5),
        'ln_ffn_w': gain(D_MODEL),
        'w_group_router': nrm((DEPTH, D_MODEL, N_GROUPS), D_MODEL ** -0.5),
        'b_group_router': nrm((DEPTH, N_GROUPS), 0.01),
        'w_expert_router': nrm((DEPTH, D_MODEL, N_EXPERTS), D_MODEL ** -0.5),
        'b_expert_router': nrm((DEPTH, N_EXPERTS), 0.01),
        'w_exp_gate': nrm((DEPTH, N_EXPERTS, D_MODEL, D_EXPERT), D_MODEL ** -0.5),
        'w_exp_up': nrm((DEPTH, N_EXPERTS, D_MODEL, D_EXPERT), D_MODEL ** -0.5),
        'w_exp_down': nrm((DEPTH, N_EXPERTS, D_EXPERT, D_MODEL), D_EXPERT ** -0.5),
    }


def reference(x_prompt, x_sample, cache_mla_ckv, cache_mla_kpe, state_gdn, state_gdn_conv,
              ln_mix_w, w_in, gdn_conv_w, gdn_a_log, gdn_dt_bias, gdn_norm_w,
              mla_q_a_norm_w, mla_w_uq, mla_kv_a_norm_w, mla_w_ukv, mla_q_norm_w, mla_k_norm_w,
              w_out, ln_ffn_w, w_group_router, b_group_router, w_expert_router, b_expert_router,
              w_exp_gate, w_exp_up, w_exp_down):
    y_p, y_s = x_prompt, x_sample
    bsz = x_prompt.shape[0]
    ckv_p_l, kpe_p_l, gdn_p_l, conv_p_l = [], [], [], []
    ckv_s_l, kpe_s_l, gdn_s_l, conv_s_l = [], [], [], []
    for l in range(DEPTH):
        p = {
            'ln_mix_w': ln_mix_w[l], 'w_in': w_in[l], 'gdn_conv_w': gdn_conv_w[l],
            'gdn_a_log': gdn_a_log[l], 'gdn_dt_bias': gdn_dt_bias[l], 'gdn_norm_w': gdn_norm_w[l],
            'mla_q_a_norm_w': mla_q_a_norm_w[l], 'mla_w_uq': mla_w_uq[l],
            'mla_kv_a_norm_w': mla_kv_a_norm_w[l], 'mla_w_ukv': mla_w_ukv[l],
            'mla_q_norm_w': mla_q_norm_w[l], 'mla_k_norm_w': mla_k_norm_w[l],
            'w_out': w_out[l], 'ln_ffn_w': ln_ffn_w[l],
            'w_group_router': w_group_router[l], 'b_group_router': b_group_router[l],
            'w_expert_router': w_expert_router[l], 'b_expert_router': b_expert_router[l],
            'w_exp_gate': w_exp_gate[l], 'w_exp_up': w_exp_up[l], 'w_exp_down': w_exp_down[l],
        }
        y_p, ckv_p, kpe_p, gdn_p, conv_p = hybrid_layer(
            y_p,
            jnp.zeros((bsz, CONV_W - 1, GDN_CONV_DIM), y_p.dtype),
            jnp.zeros((bsz, GDN_HEADS, GDN_DK, GDN_DV), state_gdn.dtype),
            jnp.zeros((bsz, 0, KV_LORA), cache_mla_ckv.dtype),
            jnp.zeros((bsz, 0, QK_ROPE), cache_mla_kpe.dtype),
            p)
        y_s, ckv_s, kpe_s, gdn_s, conv_s = hybrid_layer(
            y_s, state_gdn_conv[l], state_gdn[l], cache_mla_ckv[l], cache_mla_kpe[l], p)
        ckv_p_l.append(ckv_p); kpe_p_l.append(kpe_p); gdn_p_l.append(gdn_p); conv_p_l.append(conv_p)
        ckv_s_l.append(ckv_s); kpe_s_l.append(kpe_s); gdn_s_l.append(gdn_s); conv_s_l.append(conv_s)
    new_ckv_prompt = jnp.stack(ckv_p_l)
    new_kpe_prompt = jnp.stack(kpe_p_l)
    new_gdn_prompt = jnp.stack(gdn_p_l)
    new_conv_prompt = jnp.stack(conv_p_l)
    new_ckv_sample = jnp.stack(ckv_s_l)
    new_kpe_sample = jnp.stack(kpe_s_l)
    new_gdn_sample = jnp.stack(gdn_s_l)
    new_conv_sample = jnp.stack(conv_s_l)
    return (y_p, y_s, new_ckv_prompt, new_kpe_prompt, new_gdn_prompt, new_conv_prompt,
            new_ckv_sample, new_kpe_sample, new_gdn_sample, new_conv_sample)
```

```python
import functools
import math

import jax
import jax.numpy as jnp
import numpy as np
from jax import lax
from jax.experimental import pallas as pl
from jax.experimental.pallas import tpu as pltpu

F32 = jnp.float32
BF16 = jnp.bfloat16

D_MODEL = 1024
CHUNK = 64
RMS_EPS = 1e-6

GDN_HEADS = 4
GDN_DK = 128
GDN_DV = 128
GDN_QK_DIM = GDN_HEADS * GDN_DK
GDN_V_DIM = GDN_HEADS * GDN_DV
GDN_CONV_DIM = 2 * GDN_QK_DIM + GDN_V_DIM
CONV_W = 4

MLA_HEADS = 4
Q_LORA = 512
KV_LORA = 256
QK_NOPE = 128
QK_ROPE = 64
ROPE_HALF = QK_ROPE // 2
V_HEAD = 128
QK_HEAD = QK_NOPE + QK_ROPE
ROPE_THETA = 10000.0

N_GROUPS = 4
EXPERTS_PER_GROUP = 8
N_EXPERTS = N_GROUPS * EXPERTS_PER_GROUP
D_EXPERT = 256

LANES = 128
VMEM_LIMIT_BYTES = 56 * 1024 * 1024

Q_SCALE = (QK_HEAD ** -0.5) * math.log2(math.e)
MASK_VALUE = -1e30


def _mm(a, b):
    return jnp.dot(a.astype(BF16), b.astype(BF16), preferred_element_type=F32)


def _mm_nt(a, b):
    return lax.dot_general(a.astype(BF16), b.astype(BF16), (((1,), (1,)), ((), ())),
                           preferred_element_type=F32)


def _mm_tn(a, b):
    return lax.dot_general(a.astype(BF16), b.astype(BF16), (((0,), (0,)), ((), ())),
                           preferred_element_type=F32)


def _split(a):
    hi = a.astype(BF16)
    lo = (a - hi.astype(F32)).astype(BF16)
    return hi, lo


def _mm3(a, b):
    ah, al = _split(a)
    bh, bl = _split(b)
    return _mm(ah, bh) + (_mm(ah, bl) + _mm(al, bh))


def _mm_exact_rhs(a, b01):
    a1 = a.astype(BF16)
    r1 = a - a1.astype(F32)
    a2 = r1.astype(BF16)
    a3 = (r1 - a2.astype(F32)).astype(BF16)
    return _mm(a1, b01) + (_mm(a2, b01) + _mm(a3, b01))


def _mm_exact_lhs(a01, b):
    b1 = b.astype(BF16)
    r1 = b - b1.astype(F32)
    b2 = r1.astype(BF16)
    b3 = (r1 - b2.astype(F32)).astype(BF16)
    return _mm(a01, b1) + (_mm(a01, b2) + _mm(a01, b3))


def _rms(x, w):
    return x * lax.rsqrt(jnp.mean(x * x, axis=-1, keepdims=True) + RMS_EPS) * w


def _silu(x):
    return x * jax.nn.sigmoid(x)


def _swap_halves(x):
    h = x.shape[-1] // 2
    return jnp.concatenate([x[:, h:], x[:, :h]], axis=-1)


def _const_spec(shape):
    nd = len(shape)
    return pl.BlockSpec(shape, lambda *_: (0,) * nd)


def _params(*sem):
    return pltpu.CompilerParams(dimension_semantics=sem, vmem_limit_bytes=VMEM_LIMIT_BYTES)


def _inproj_kernel(x_ref, lnw_ref, wqkv_ref, wz_ref, wba_ref, alog_ref, dtb_ref, wcq_ref, qanw_ref,
                   wuq_ref, qnw_ref, cos_ref, sin_ref, wckv_ref, kvanw_ref, wkpe_ref,
                   qkv_ref, z_ref, gates_ref, q_ref, ckv_ref, kpe_ref):
    h = _rms(x_ref[...], lnw_ref[...])
    hb = h.astype(BF16)
    qkv_ref[...] = _mm(hb, wqkv_ref[...])
    z_ref[...] = _mm(hb, wz_ref[...])

    raw = _mm3(h, wba_ref[...])
    lane = lax.broadcasted_iota(jnp.int32, raw.shape, 1)
    pre = raw + dtb_ref[...]
    softplus = jnp.maximum(pre, 0.0) + jnp.log1p(jnp.exp(-jnp.abs(pre)))
    g = -jnp.exp(alog_ref[...]) * softplus
    gates_ref[...] = jnp.where(lane < GDN_HEADS, jax.nn.sigmoid(raw), g)

    ckv_ref[...] = _rms(_mm(hb, wckv_ref[...]), kvanw_ref[...])
    kpe_ref[...] = _mm(hb, wkpe_ref[...])

    cqn = _rms(_mm(hb, wcq_ref[...]), qanw_ref[...])
    qall = _mm(cqn, wuq_ref[...])
    qnw = qnw_ref[...]
    cosf, sinf = cos_ref[...], sin_ref[...]
    rope0 = MLA_HEADS * QK_NOPE
    for hd in range(MLA_HEADS):
        nope = qall[:, hd * QK_NOPE:(hd + 1) * QK_NOPE]
        rr = qall[:, rope0 + hd * QK_ROPE: rope0 + (hd + 1) * QK_ROPE]
        ssq = jnp.sum(nope * nope, axis=-1, keepdims=True) + jnp.sum(rr * rr, axis=-1, keepdims=True)
        inv = lax.rsqrt(ssq * (1.0 / QK_HEAD) + RMS_EPS) * Q_SCALE
        rw = rr * qnw[:, QK_NOPE:]
        rot = rw * cosf + _swap_halves(rw) * sinf
        qh = jnp.concatenate([nope * qnw[:, :QK_NOPE], rot], axis=-1) * inv
        q_ref[hd] = qh.astype(BF16)


def _inproj(x2d, p, rope_q, tm):
    t = x2d.shape[0]
    nt = t // tm
    cos_t, sin_t = rope_q
    n_rope_blocks = cos_t.shape[0] // tm
    rope_spec = pl.BlockSpec((tm, QK_ROPE), lambda i: (i % n_rope_blocks, 0))
    row = lambda n: pl.BlockSpec((tm, n), lambda i: (i, 0))
    consts = [p['ln_mix_w'], p['w_qkv'], p['w_z'], p['w_ba'], p['a_log'], p['dt_bias'], p['w_cq'],
              p['q_a_norm_w'], p['w_uq'], p['q_norm_w']]
    consts2 = [p['w_ckv'], p['kv_a_norm_w'], p['w_kpe']]
    in_specs = ([row(D_MODEL)] + [_const_spec(c.shape) for c in consts] + [rope_spec, rope_spec]
                + [_const_spec(c.shape) for c in consts2])
    out_shape = (
        jax.ShapeDtypeStruct((t, GDN_CONV_DIM), F32),
        jax.ShapeDtypeStruct((t, GDN_V_DIM), F32),
        jax.ShapeDtypeStruct((t, LANES), F32),
        jax.ShapeDtypeStruct((MLA_HEADS, t, QK_HEAD), BF16),
        jax.ShapeDtypeStruct((t, KV_LORA), F32),
        jax.ShapeDtypeStruct((t, QK_ROPE), F32),
    )
    out_specs = (row(GDN_CONV_DIM), row(GDN_V_DIM), row(LANES),
                 pl.BlockSpec((MLA_HEADS, tm, QK_HEAD), lambda i: (0, i, 0)),
                 row(KV_LORA), row(QK_ROPE))
    return pl.pallas_call(
        _inproj_kernel, grid=(nt,), in_specs=in_specs, out_specs=out_specs, out_shape=out_shape,
        compiler_params=_params("parallel"), name="inproj",
    )(x2d, *consts, cos_t, sin_t, *consts2)


def _kvprep_kernel(ckv_ref, kpe_ref, cos_ref, sin_ref, wukv_ref, knw_ref, k_ref, v_ref):
    kv = _mm(ckv_ref[...], wukv_ref[...])
    kpe = kpe_ref[...]
    knw = knw_ref[...]
    kw = kpe * knw[:, QK_NOPE:]
    krot = kw * cos_ref[...] + _swap_halves(kw) * sin_ref[...]
    s_kpe = jnp.sum(kpe * kpe, axis=-1, keepdims=True)
    v0 = MLA_HEADS * QK_NOPE
    for hd in range(MLA_HEADS):
        kn = kv[:, hd * QK_NOPE:(hd + 1) * QK_NOPE]
        inv = lax.rsqrt((jnp.sum(kn * kn, axis=-1, keepdims=True) + s_kpe) * (1.0 / QK_HEAD) + RMS_EPS)
        kh = jnp.concatenate([kn * knw[:, :QK_NOPE], krot], axis=-1) * inv
        k_ref[hd] = kh.astype(BF16)
        v_ref[hd] = kv[:, v0 + hd * V_HEAD: v0 + (hd + 1) * V_HEAD].astype(BF16)


def _kvprep(ckv2d, kpe2d, p, rope_k, tm):
    t = ckv2d.shape[0]
    cos_t, sin_t = rope_k
    n_rope_blocks = cos_t.shape[0] // tm
    rope_spec = pl.BlockSpec((tm, QK_ROPE), lambda i: (i % n_rope_blocks, 0))
    row = lambda n: pl.BlockSpec((tm, n), lambda i: (i, 0))
    return pl.pallas_call(
        _kvprep_kernel, grid=(t // tm,),
        in_specs=[row(KV_LORA), row(QK_ROPE), rope_spec, rope_spec,
                  _const_spec(p['w_ukv'].shape), _const_spec(p['k_norm_w'].shape)],
        out_specs=(pl.BlockSpec((MLA_HEADS, tm, QK_HEAD), lambda i: (0, i, 0)),
                   pl.BlockSpec((MLA_HEADS, tm, V_HEAD), lambda i: (0, i, 0))),
        out_shape=(jax.ShapeDtypeStruct((MLA_HEADS, t, QK_HEAD), BF16),
                   jax.ShapeDtypeStruct((MLA_HEADS, t, V_HEAD), BF16)),
        compiler_params=_params("parallel"), name="kvprep",
    )(ckv2d, kpe2d, cos_t, sin_t, p['w_ukv'], p['k_norm_w'])


def _flash_kernel(q_ref, k_ref, v_ref, o_ref, m_sc, l_sc, acc_sc, *, tq, tk, nk, past):
    i = pl.program_id(2)
    q = q_ref[0]
    q_first = past + i * tq
    q_last = q_first + tq - 1
    n_full = jnp.minimum(nk, ((q_first // CHUNK + 1) * CHUNK) // tk)
    n_vis = jnp.minimum(nk, ((q_last // CHUNK + 1) * CHUNK + tk - 1) // tk)

    m_sc[...] = jnp.full(m_sc.shape, MASK_VALUE, F32)
    l_sc[...] = jnp.zeros(l_sc.shape, F32)
    acc_sc[...] = jnp.zeros(acc_sc.shape, F32)

    def step(j, masked):
        k0 = pl.multiple_of(j * tk, tk)
        kb = k_ref[0, pl.ds(k0, tk), :]
        vb = v_ref[0, pl.ds(k0, tk), :]
        s = _mm_nt(q, kb)
        if masked:
            kpos = k0 + lax.broadcasted_iota(jnp.int32, (1, tk), 1)
            qpos = q_first + lax.broadcasted_iota(jnp.int32, (tq, 1), 0)
            s = jnp.where((kpos // CHUNK) <= (qpos // CHUNK), s, MASK_VALUE)
        m_old = m_sc[...]
        m_new = jnp.maximum(m_old, jnp.max(s, axis=-1, keepdims=True))
        alpha = jnp.exp2(m_old - m_new)
        p = jnp.exp2(s - m_new)
        l_sc[...] = alpha * l_sc[...] + jnp.sum(p, axis=-1, keepdims=True)
        acc_sc[...] = alpha * acc_sc[...] + _mm(p, vb)
        m_sc[...] = m_new

    def full_body(j, c):
        step(j, False)
        return c

    def masked_body(j, c):
        step(j, True)
        return c

    lax.fori_loop(0, n_full, full_body, 0)
    lax.fori_loop(n_full, n_vis, masked_body, 0)
    o_ref[...] = (acc_sc[...] / l_sc[...]).astype(o_ref.dtype)


def _flash(q, k, v, bsz, lq, lk, past, tq, tk):
    nq = lq // tq
    nk = lk // tk
    kern = functools.partial(_flash_kernel, tq=tq, tk=tk, nk=nk, past=past)
    return pl.pallas_call(
        kern, grid=(bsz, MLA_HEADS, nq),
        in_specs=[pl.BlockSpec((1, tq, QK_HEAD), lambda b, h, i: (h, b * nq + i, 0)),
                  pl.BlockSpec((1, lk, QK_HEAD), lambda b, h, i: (h, b, 0)),
                  pl.BlockSpec((1, lk, V_HEAD), lambda b, h, i: (h, b, 0))],
        out_specs=pl.BlockSpec((tq, V_HEAD), lambda b, h, i: (b * nq + i, h)),
        out_shape=jax.ShapeDtypeStruct((bsz * lq, MLA_HEADS * V_HEAD), BF16),
        scratch_shapes=[pltpu.VMEM((tq, 1), F32), pltpu.VMEM((tq, 1), F32), pltpu.VMEM((tq, V_HEAD), F32)],
        compiler_params=_params("parallel", "parallel", "arbitrary"), name="flash",
    )(q, k, v)


def _gdn_kernel(qkv_ref, z_ref, gates_ref, hist_ref, s0_ref, convw_ref, normw_ref,
                o_ref, s_ref, xbuf, *, tb):
    j = pl.program_id(1)
    nc = tb // CHUNK
    pad = 8

    @pl.when(j == 0)
    def _():
        xbuf[0:pad, :] = hist_ref[0]
        s_ref[...] = s0_ref[...]

    x = qkv_ref[...]
    xbuf[pad:pad + tb, :] = x
    cw = convw_ref[...]
    y = xbuf[pad - 3:pad - 3 + tb, :] * cw[0:1, :]
    for i in range(1, CONV_W):
        y = y + xbuf[pad - 3 + i:pad - 3 + i + tb, :] * cw[i:i + 1, :]
    xbuf[0:pad, :] = xbuf[tb:tb + pad, :]
    y = _silu(y)

    gates = gates_ref[...]
    gates_t = gates.T[0:8, :]
    row = lax.broadcasted_iota(jnp.int32, (tb, tb), 0)
    col = lax.broadcasted_iota(jnp.int32, (tb, tb), 1)
    same = (row // CHUNK) == (col // CHUNK)
    incl = same & (row >= col)
    strict = same & (row > col)
    tril01 = jnp.where(incl, 1.0, 0.0).astype(BF16)
    triu01 = jnp.where(same & (row <= col), 1.0, 0.0).astype(BF16)
    gc_cols = _mm_exact_lhs(tril01, gates)
    gc_rows = _mm_exact_rhs(gates_t, triu01)
    eye = jnp.where(row == col, 1.0, 0.0)

    u_rows = []
    for hd in range(GDN_HEADS):
        qh = y[:, hd * GDN_DK:(hd + 1) * GDN_DK]
        kh = y[:, GDN_QK_DIM + hd * GDN_DK: GDN_QK_DIM + (hd + 1) * GDN_DK]
        vh = y[:, 2 * GDN_QK_DIM + hd * GDN_DV: 2 * GDN_QK_DIM + (hd + 1) * GDN_DV]
        qh = qh * (lax.rsqrt(jnp.sum(qh * qh, axis=-1, keepdims=True) + RMS_EPS) * (GDN_DK ** -0.5))
        kh = kh * lax.rsqrt(jnp.sum(kh * kh, axis=-1, keepdims=True) + RMS_EPS)
        beta = gates[:, hd:hd + 1]
        gc = gc_cols[:, GDN_HEADS + hd: GDN_HEADS + hd + 1]
        gr = gc_rows[GDN_HEADS + hd: GDN_HEADS + hd + 1, :]
        decay = jnp.where(incl, jnp.exp(jnp.where(incl, gc - gr, 0.0)), 0.0)
        kb = kh * beta
        kbb, khb = kb.astype(BF16), kh.astype(BF16)
        neg_l = jnp.where(strict, -(_mm_nt(kbb, khb) * decay), 0.0)
        attn = _mm_nt(qh, khb) * decay

        tinv = eye + neg_l
        pw = neg_l
        for _ in range(int(math.log2(CHUNK)) - 1):
            pw = _mm3(pw, pw)
            tinv = tinv + _mm3(tinv, pw)

        eg = jnp.exp(gc)
        sol = _mm3(tinv, jnp.concatenate([vh * beta, kb * eg], axis=-1))
        u_val, k_cd = sol[:, :GDN_DV], sol[:, GDN_DV:]
        q_dec = qh * eg

        s = s_ref[0, hd]
        us, o_inter = [], []
        for c in range(nc):
            r0, r1 = c * CHUNK, (c + 1) * CHUNK
            g_end = gc[r1 - 1:r1, :]
            k_end = kh[r0:r1] * jnp.exp(g_end - gc[r0:r1])
            both = _mm(jnp.concatenate([k_cd[r0:r1], q_dec[r0:r1]], axis=0), s)
            u = u_val[r0:r1] - both[:CHUNK]
            o_inter.append(both[CHUNK:])
            us.append(u)
            s = s * jnp.exp(g_end) + _mm_tn(k_end, u)
        s_ref[0, hd] = s
        u_all = us[0] if nc == 1 else jnp.concatenate(us, axis=0)
        o_all = o_inter[0] if nc == 1 else jnp.concatenate(o_inter, axis=0)
        o = o_all + _mm(attn, u_all)
        o = _rms(o, normw_ref[...]) * _silu(z_ref[:, hd * GDN_DV:(hd + 1) * GDN_DV])
        o_ref[:, hd * GDN_DV:(hd + 1) * GDN_DV] = o.astype(o_ref.dtype)


def _gdn(qkv2d, z2d, gates2d, hist_pad, s0, p, bsz, seq, tb):
    nb = seq // tb
    kern = functools.partial(_gdn_kernel, tb=tb)
    row = lambda n: pl.BlockSpec((tb, n), lambda b, j: (b * nb + j, 0))
    state_spec = pl.BlockSpec((1, GDN_HEADS, GDN_DK, GDN_DV), lambda b, j: (b, 0, 0, 0))
    return pl.pallas_call(
        kern, grid=(bsz, nb),
        in_specs=[row(GDN_CONV_DIM), row(GDN_V_DIM), row(LANES),
                  pl.BlockSpec((1, 8, GDN_CONV_DIM), lambda b, j: (b, 0, 0)),
                  state_spec, _const_spec(p['conv_w'].shape), _const_spec(p['gdn_norm_w'].shape)],
        out_specs=(row(GDN_V_DIM), state_spec),
        out_shape=(jax.ShapeDtypeStruct((bsz * seq, GDN_V_DIM), BF16),
                   jax.ShapeDtypeStruct((bsz, GDN_HEADS, GDN_DK, GDN_DV), F32)),
        scratch_shapes=[pltpu.VMEM((tb + 8, GDN_CONV_DIM), F32)],
        compiler_params=_params("parallel", "arbitrary"), name="gdn",
    )(qkv2d, z2d, gates2d, hist_pad, s0, p['conv_w'], p['gdn_norm_w'])


def _outproj_kernel(x_ref, g_ref, a_ref, wog_ref, woa_ref, lnw_ref, wr_ref, br_ref,
                    x1_ref, h2_ref, comb_ref):
    x1 = x_ref[...] + _mm(g_ref[...], wog_ref[...]) + _mm(a_ref[...], woa_ref[...])
    x1_ref[...] = x1
    h2 = _rms(x1, lnw_ref[...])
    h2_ref[...] = h2.astype(BF16)

    logits = _mm3(h2, wr_ref[...]) + br_ref[...]
    lane = lax.broadcasted_iota(jnp.int32, logits.shape, 1)
    big = jnp.int32(LANES)
    is_grp = (lane >= N_EXPERTS) & (lane < N_EXPERTS + N_GROUPS)
    gl = jnp.where(is_grp, logits, -jnp.inf)
    gmax = jnp.max(gl, axis=-1, keepdims=True)
    grp_p = 1.0 / jnp.sum(jnp.exp(gl - gmax), axis=-1, keepdims=True)
    grp_idx = jnp.min(jnp.where(gl == gmax, lane - N_EXPERTS, big), axis=-1, keepdims=True)

    in_grp = (lane < N_EXPERTS) & ((lane // EXPERTS_PER_GROUP) == grp_idx)
    el = jnp.where(in_grp, logits, -jnp.inf)
    m1 = jnp.max(el, axis=-1, keepdims=True)
    i1 = jnp.min(jnp.where(el == m1, lane, big), axis=-1, keepdims=True)
    el2 = jnp.where(lane == i1, -jnp.inf, el)
    m2 = jnp.max(el2, axis=-1, keepdims=True)
    i2 = jnp.min(jnp.where(el2 == m2, lane, big), axis=-1, keepdims=True)
    e2 = jnp.exp(m2 - m1)
    w1 = 1.0 / (1.0 + e2)
    w2 = e2 * w1
    comb_ref[...] = jnp.where(lane == i1, grp_p * w1, jnp.where(lane == i2, grp_p * w2, 0.0))


def _outproj(x2d, gdn_out, mla_out, p, tm):
    t = x2d.shape[0]
    row = lambda n: pl.BlockSpec((tm, n), lambda i: (i, 0))
    consts = [p['w_out_gdn'], p['w_out_mla'], p['ln_ffn_w'], p['w_router'], p['b_router']]
    return pl.pallas_call(
        _outproj_kernel, grid=(t // tm,),
        in_specs=[row(D_MODEL), row(GDN_V_DIM), row(MLA_HEADS * V_HEAD)] + [_const_spec(c.shape) for c in consts],
        out_specs=(row(D_MODEL), row(D_MODEL), row(LANES)),
        out_shape=(jax.ShapeDtypeStruct((t, D_MODEL), F32), jax.ShapeDtypeStruct((t, D_MODEL), BF16),
                   jax.ShapeDtypeStruct((t, LANES), F32)),
        compiler_params=_params("parallel"), name="outproj",
    )(x2d, gdn_out, mla_out, *consts)


def _moe_kernel(h_ref, comb_ref, x1_ref, wg_ref, wu_ref, wd_ref, y_ref):
    e = pl.program_id(1)

    @pl.when(e == 0)
    def _():
        y_ref[...] = x1_ref[...]

    h = h_ref[...]
    comb = comb_ref[...]
    lane = lax.broadcasted_iota(jnp.int32, comb.shape, 1)
    c = jnp.sum(jnp.where(lane == e, comb, 0.0), axis=-1, keepdims=True)
    hid = _silu(_mm(h, wg_ref[0])) * _mm(h, wu_ref[0]) * c
    y_ref[...] += _mm(hid, wd_ref[0])


def _moe(h2, comb, x1, p, tm):
    t = h2.shape[0]
    row = lambda n: pl.BlockSpec((tm, n), lambda i, e: (i, 0))
    return pl.pallas_call(
        _moe_kernel, grid=(t // tm, N_EXPERTS),
        in_specs=[row(D_MODEL), row(LANES), row(D_MODEL),
                  pl.BlockSpec((1, D_MODEL, D_EXPERT), lambda i, e: (e, 0, 0)),
                  pl.BlockSpec((1, D_MODEL, D_EXPERT), lambda i, e: (e, 0, 0)),
                  pl.BlockSpec((1, D_EXPERT, D_MODEL), lambda i, e: (e, 0, 0))],
        out_specs=row(D_MODEL),
        out_shape=jax.ShapeDtypeStruct((t, D_MODEL), F32),
        compiler_params=_params("parallel", "arbitrary"), name="moe",
    )(h2, comb, x1, p['w_exp_gate'], p['w_exp_up'], p['w_exp_down'])


def _rope_tables(pos):
    inv = ROPE_THETA ** (-jnp.arange(ROPE_HALF, dtype=F32) / ROPE_HALF)
    ang = pos.astype(F32)[:, None] * inv[None, :]
    cos, sin = jnp.cos(ang), jnp.sin(ang)
    return jnp.concatenate([cos, cos], axis=-1), jnp.concatenate([-sin, sin], axis=-1)


def _tile_rows(tables, tm):
    n = tables[0].shape[0]
    if n >= tm:
        assert n % tm == 0
        return tables
    assert tm % n == 0
    return tuple(jnp.tile(t, (tm // n, 1)) for t in tables)


def _pad_lanes(v, offset):
    return jnp.zeros((1, LANES), F32).at[0, offset:offset + v.shape[0]].set(v.astype(F32))


def _prepare_params(p):
    w_in = p['w_in']
    sizes = (GDN_CONV_DIM, GDN_V_DIM, GDN_HEADS, GDN_HEADS, Q_LORA, KV_LORA, QK_ROPE)
    offs = np.concatenate([[0], np.cumsum(sizes)])
    part = [w_in[:, int(offs[i]):int(offs[i + 1])] for i in range(len(sizes))]
    w_ba = jnp.zeros((D_MODEL, LANES), F32).at[:, :2 * GDN_HEADS].set(jnp.concatenate([part[2], part[3]], axis=1))

    uq = p['mla_w_uq'].reshape(Q_LORA, MLA_HEADS, QK_HEAD)
    w_uq = jnp.concatenate([uq[:, :, :QK_NOPE].reshape(Q_LORA, -1), uq[:, :, QK_NOPE:].reshape(Q_LORA, -1)], axis=1)
    ukv = p['mla_w_ukv'].reshape(KV_LORA, MLA_HEADS, QK_NOPE + V_HEAD)
    w_ukv = jnp.concatenate([ukv[:, :, :QK_NOPE].reshape(KV_LORA, -1), ukv[:, :, QK_NOPE:].reshape(KV_LORA, -1)], axis=1)

    w_router = jnp.zeros((D_MODEL, LANES), F32)
    w_router = w_router.at[:, :N_EXPERTS].set(p['w_expert_router'])
    w_router = w_router.at[:, N_EXPERTS:N_EXPERTS + N_GROUPS].set(p['w_group_router'])
    b_router = _pad_lanes(jnp.concatenate([p['b_expert_router'], p['b_group_router']]), 0)

    row = lambda v: v.astype(F32)[None, :]
    return {
        'ln_mix_w': row(p['ln_mix_w']),
        'w_qkv': part[0].astype(BF16), 'w_z': part[1].astype(BF16), 'w_ba': w_ba,
        'a_log': _pad_lanes(p['gdn_a_log'], GDN_HEADS), 'dt_bias': _pad_lanes(p['gdn_dt_bias'], GDN_HEADS),
        'w_cq': part[4].astype(BF16), 'w_ckv': part[5].astype(BF16), 'w_kpe': part[6].astype(BF16),
        'q_a_norm_w': row(p['mla_q_a_norm_w']), 'kv_a_norm_w': row(p['mla_kv_a_norm_w']),
        'w_uq': w_uq.astype(BF16), 'w_ukv': w_ukv.astype(BF16),
        'q_norm_w': row(p['mla_q_norm_w']), 'k_norm_w': row(p['mla_k_norm_w']),
        'conv_w': p['gdn_conv_w'].astype(F32), 'gdn_norm_w': row(p['gdn_norm_w']),
        'w_out_gdn': p['w_out'][:GDN_V_DIM].astype(BF16), 'w_out_mla': p['w_out'][GDN_V_DIM:].astype(BF16),
        'ln_ffn_w': row(p['ln_ffn_w']), 'w_router': w_router, 'b_router': b_router,
        'w_exp_gate': p['w_exp_gate'].astype(BF16), 'w_exp_up': p['w_exp_up'].astype(BF16),
        'w_exp_down': p['w_exp_down'].astype(BF16),
    }


def _tiles(bsz, seq, past):
    t = bsz * seq
    tm = min(512, t)
    n_keys = past + seq
    if past == 0:
        tq = tk = min(512, seq)
        lk = n_keys
    else:
        assert n_keys % CHUNK == 0
        tq = seq
        lk = -(-n_keys // LANES) * LANES
        tk = lk
    tb = min(256, seq)
    tmoe = min(1024, t)
    assert seq % tq == 0 and lk % tk == 0 and seq % tb == 0 and tb % CHUNK == 0 and t % tm == 0 and t % tmoe == 0
    return tm, tq, tk, lk, tb, tmoe


def _hybrid_layer(x, conv_hist, s0, ckv_past, kpe_past, p):
    bsz, seq, _ = x.shape
    past = ckv_past.shape[1]
    t = bsz * seq
    tm, tq, tk, lk, tb, tmoe = _tiles(bsz, seq, past)
    x2d = x.reshape(t, D_MODEL)

    rope_q = _tile_rows(_rope_tables(past + jnp.arange(seq, dtype=jnp.int32)), tm)
    qkv, z, gates, q, ckv, kpe = _inproj(x2d, p, rope_q, tm)

    n_keys = past + seq
    ckv3, kpe3 = ckv.reshape(bsz, seq, KV_LORA), kpe.reshape(bsz, seq, QK_ROPE)
    if lk == seq:
        ckv_all, kpe_all = ckv, kpe
    else:
        zpad = lambda n: jnp.zeros((bsz, lk - n_keys, n), F32)
        ckv_all = jnp.concatenate([ckv_past.astype(F32), ckv3, zpad(KV_LORA)], axis=1).reshape(bsz * lk, KV_LORA)
        kpe_all = jnp.concatenate([kpe_past.astype(F32), kpe3, zpad(QK_ROPE)], axis=1).reshape(bsz * lk, QK_ROPE)
    tkv = tm if lk % tm == 0 else lk
    rope_k = _tile_rows(_rope_tables(jnp.arange(lk, dtype=jnp.int32)), tkv)
    k, v = _kvprep(ckv_all, kpe_all, p, rope_k, tkv)
    mla_out = _flash(q, k, v, bsz, seq, lk, past, tq, tk)

    hist_pad = jnp.concatenate([jnp.zeros((bsz, 8 - (CONV_W - 1), GDN_CONV_DIM), F32), conv_hist.astype(F32)], axis=1)
    gdn_out, s_new = _gdn(qkv, z, gates, hist_pad, s0.astype(F32), p, bsz, seq, tb)
    assert seq >= CONV_W - 1
    conv_new = qkv.reshape(bsz, seq, GDN_CONV_DIM)[:, seq - (CONV_W - 1):]

    x1, h2, comb = _outproj(x2d, gdn_out, mla_out, p, tm)
    y = _moe(h2, comb, x1, p, tmoe)
    return y.reshape(bsz, seq, D_MODEL), ckv3, kpe3, s_new.astype(s0.dtype), conv_new


def kernel(x_prompt, x_sample, cache_mla_ckv, cache_mla_kpe, state_gdn, state_gdn_conv, ln_mix_w, w_in, gdn_conv_w, gdn_a_log, gdn_dt_bias, gdn_norm_w, mla_q_a_norm_w, mla_w_uq, mla_kv_a_norm_w, mla_w_ukv, mla_q_norm_w, mla_k_norm_w, w_out, ln_ffn_w, w_group_router, b_group_router, w_expert_router, b_expert_router, w_exp_gate, w_exp_up, w_exp_down):
    depth = w_in.shape[0]
    bsz = x_prompt.shape[0]
    y_p, y_s = x_prompt, x_sample
    outs_p, outs_s = [], []
    for l in range(depth):
        p = _prepare_params({
            'ln_mix_w': ln_mix_w[l], 'w_in': w_in[l], 'gdn_conv_w': gdn_conv_w[l],
            'gdn_a_log': gdn_a_log[l], 'gdn_dt_bias': gdn_dt_bias[l], 'gdn_norm_w': gdn_norm_w[l],
            'mla_q_a_norm_w': mla_q_a_norm_w[l], 'mla_w_uq': mla_w_uq[l],
            'mla_kv_a_norm_w': mla_kv_a_norm_w[l], 'mla_w_ukv': mla_w_ukv[l],
            'mla_q_norm_w': mla_q_norm_w[l], 'mla_k_norm_w': mla_k_norm_w[l],
            'w_out': w_out[l], 'ln_ffn_w': ln_ffn_w[l],
            'w_group_router': w_group_router[l], 'b_group_router': b_group_router[l],
            'w_expert_router': w_expert_router[l], 'b_expert_router': b_expert_router[l],
            'w_exp_gate': w_exp_gate[l], 'w_exp_up': w_exp_up[l], 'w_exp_down': w_exp_down[l],
        })
        y_p, *st_p = _hybrid_layer(
            y_p,
            jnp.zeros((bsz, CONV_W - 1, GDN_CONV_DIM), y_p.dtype),
            jnp.zeros((bsz, GDN_HEADS, GDN_DK, GDN_DV), state_gdn.dtype),
            jnp.zeros((bsz, 0, KV_LORA), cache_mla_ckv.dtype),
            jnp.zeros((bsz, 0, QK_ROPE), cache_mla_kpe.dtype),
            p)
        y_s, *st_s = _hybrid_layer(y_s, state_gdn_conv[l], state_gdn[l], cache_mla_ckv[l], cache_mla_kpe[l], p)
        outs_p.append(st_p)
        outs_s.append(st_s)
    stack = lambda outs, i: jnp.stack([o[i] for o in outs])
    return (y_p, y_s,
            stack(outs_p, 0), stack(outs_p, 1), stack(outs_p, 2), stack(outs_p, 3),
            stack(outs_s, 0), stack(outs_s, 1), stack(outs_s, 2), stack(outs_s, 3))
```

```python
import functools
import math

import jax
import jax.numpy as jnp
import numpy as np
from jax import lax
from jax.experimental import pallas as pl
from jax.experimental.pallas import tpu as pltpu

F32 = jnp.float32
BF16 = jnp.bfloat16

D_MODEL = 1024
CHUNK = 64
RMS_EPS = 1e-6

GDN_HEADS = 4
GDN_DK = 128
GDN_DV = 128
GDN_QK_DIM = GDN_HEADS * GDN_DK
GDN_V_DIM = GDN_HEADS * GDN_DV
GDN_CONV_DIM = 2 * GDN_QK_DIM + GDN_V_DIM
CONV_W = 4

MLA_HEADS = 4
Q_LORA = 512
KV_LORA = 256
QK_NOPE = 128
QK_ROPE = 64
ROPE_HALF = QK_ROPE // 2
V_HEAD = 128
QK_HEAD = QK_NOPE + QK_ROPE
ROPE_THETA = 10000.0

N_GROUPS = 4
EXPERTS_PER_GROUP = 8
N_EXPERTS = N_GROUPS * EXPERTS_PER_GROUP
D_EXPERT = 256

LANES = 128
VMEM_LIMIT_BYTES = 56 * 1024 * 1024

Q_SCALE = (QK_HEAD ** -0.5) * math.log2(math.e)
MASK_VALUE = -1e30
FLASH_HEADS_PER_STEP = 2
GDN_SOLVE_PASSES = 1


def _mm(a, b):
    return jnp.dot(a.astype(BF16), b.astype(BF16), preferred_element_type=F32)


def _mm_nt(a, b):
    return lax.dot_general(a.astype(BF16), b.astype(BF16), (((1,), (1,)), ((), ())),
                           preferred_element_type=F32)


def _mm_tn(a, b):
    return lax.dot_general(a.astype(BF16), b.astype(BF16), (((0,), (0,)), ((), ())),
                           preferred_element_type=F32)


def _split(a):
    hi = a.astype(BF16)
    lo = (a - hi.astype(F32)).astype(BF16)
    return hi, lo


def _mm3(a, b):
    ah, al = _split(a)
    bh, bl = _split(b)
    return _mm(ah, bh) + (_mm(ah, bl) + _mm(al, bh))


def _mm_solve(a, b):
    return _mm3(a, b) if GDN_SOLVE_PASSES == 3 else _mm(a, b)


def _mm_exact_rhs(a, b01):
    a1 = a.astype(BF16)
    r1 = a - a1.astype(F32)
    a2 = r1.astype(BF16)
    a3 = (r1 - a2.astype(F32)).astype(BF16)
    return _mm(a1, b01) + (_mm(a2, b01) + _mm(a3, b01))


def _mm_exact_lhs(a01, b):
    b1 = b.astype(BF16)
    r1 = b - b1.astype(F32)
    b2 = r1.astype(BF16)
    b3 = (r1 - b2.astype(F32)).astype(BF16)
    return _mm(a01, b1) + (_mm(a01, b2) + _mm(a01, b3))


def _rms(x, w):
    return x * lax.rsqrt(jnp.mean(x * x, axis=-1, keepdims=True) + RMS_EPS) * w


def _silu(x):
    return x * jax.nn.sigmoid(x)


def _swap_halves(x):
    h = x.shape[-1] // 2
    return jnp.concatenate([x[:, h:], x[:, :h]], axis=-1)


def _const_spec(shape):
    nd = len(shape)
    return pl.BlockSpec(shape, lambda *_: (0,) * nd)


def _params(*sem):
    return pltpu.CompilerParams(dimension_semantics=sem, vmem_limit_bytes=VMEM_LIMIT_BYTES)


def _inproj_kernel(x_ref, lnw_ref, wqkv_ref, wz_ref, wba_ref, alog_ref, dtb_ref, wcq_ref, qanw_ref,
                   wuq_ref, qnw_ref, cos_ref, sin_ref, wckv_ref, kvanw_ref, wkpe_ref,
                   qkv_ref, z_ref, gates_ref, q_ref, ckv_ref, kpe_ref):
    h = _rms(x_ref[...], lnw_ref[...])
    hb = h.astype(BF16)
    qkv_ref[...] = _mm(hb, wqkv_ref[...])
    z_ref[...] = _mm(hb, wz_ref[...])

    raw = _mm3(h, wba_ref[...])
    lane = lax.broadcasted_iota(jnp.int32, raw.shape, 1)
    pre = raw + dtb_ref[...]
    softplus = jnp.maximum(pre, 0.0) + jnp.log1p(jnp.exp(-jnp.abs(pre)))
    g = -jnp.exp(alog_ref[...]) * softplus
    gates_ref[...] = jnp.where(lane < GDN_HEADS, jax.nn.sigmoid(raw), g)

    ckv_ref[...] = _rms(_mm(hb, wckv_ref[...]), kvanw_ref[...])
    kpe_ref[...] = _mm(hb, wkpe_ref[...])

    cqn = _rms(_mm(hb, wcq_ref[...]), qanw_ref[...])
    qall = _mm(cqn, wuq_ref[...])
    qnw = qnw_ref[...]
    cosf, sinf = cos_ref[...], sin_ref[...]
    rope0 = MLA_HEADS * QK_NOPE
    for hd in range(MLA_HEADS):
        nope = qall[:, hd * QK_NOPE:(hd + 1) * QK_NOPE]
        rr = qall[:, rope0 + hd * QK_ROPE: rope0 + (hd + 1) * QK_ROPE]
        ssq = jnp.sum(nope * nope, axis=-1, keepdims=True) + jnp.sum(rr * rr, axis=-1, keepdims=True)
        inv = lax.rsqrt(ssq * (1.0 / QK_HEAD) + RMS_EPS) * Q_SCALE
        rw = rr * qnw[:, QK_NOPE:]
        rot = rw * cosf + _swap_halves(rw) * sinf
        qh = jnp.concatenate([nope * qnw[:, :QK_NOPE], rot], axis=-1) * inv
        q_ref[hd] = qh.astype(BF16)


def _inproj(x2d, p, rope_q, tm):
    t = x2d.shape[0]
    nt = t // tm
    cos_t, sin_t = rope_q
    n_rope_blocks = cos_t.shape[0] // tm
    rope_spec = pl.BlockSpec((tm, QK_ROPE), lambda i: (i % n_rope_blocks, 0))
    row = lambda n: pl.BlockSpec((tm, n), lambda i: (i, 0))
    consts = [p['ln_mix_w'], p['w_qkv'], p['w_z'], p['w_ba'], p['a_log'], p['dt_bias'], p['w_cq'],
              p['q_a_norm_w'], p['w_uq'], p['q_norm_w']]
    consts2 = [p['w_ckv'], p['kv_a_norm_w'], p['w_kpe']]
    in_specs = ([row(D_MODEL)] + [_const_spec(c.shape) for c in consts] + [rope_spec, rope_spec]
                + [_const_spec(c.shape) for c in consts2])
    out_shape = (
        jax.ShapeDtypeStruct((t, GDN_CONV_DIM), F32),
        jax.ShapeDtypeStruct((t, GDN_V_DIM), F32),
        jax.ShapeDtypeStruct((t, LANES), F32),
        jax.ShapeDtypeStruct((MLA_HEADS, t, QK_HEAD), BF16),
        jax.ShapeDtypeStruct((t, KV_LORA), F32),
        jax.ShapeDtypeStruct((t, QK_ROPE), F32),
    )
    out_specs = (row(GDN_CONV_DIM), row(GDN_V_DIM), row(LANES),
                 pl.BlockSpec((MLA_HEADS, tm, QK_HEAD), lambda i: (0, i, 0)),
                 row(KV_LORA), row(QK_ROPE))
    return pl.pallas_call(
        _inproj_kernel, grid=(nt,), in_specs=in_specs, out_specs=out_specs, out_shape=out_shape,
        compiler_params=_params("parallel"), name="inproj",
    )(x2d, *consts, cos_t, sin_t, *consts2)


def _kvprep_kernel(ckv_ref, kpe_ref, cos_ref, sin_ref, wukv_ref, knw_ref, k_ref, vt_ref):
    kv = _mm(ckv_ref[...], wukv_ref[...])
    kpe = kpe_ref[...]
    knw = knw_ref[...]
    kw = kpe * knw[:, QK_NOPE:]
    krot = kw * cos_ref[...] + _swap_halves(kw) * sin_ref[...]
    s_kpe = jnp.sum(kpe * kpe, axis=-1, keepdims=True)
    v0 = MLA_HEADS * QK_NOPE
    for hd in range(MLA_HEADS):
        kn = kv[:, hd * QK_NOPE:(hd + 1) * QK_NOPE]
        inv = lax.rsqrt((jnp.sum(kn * kn, axis=-1, keepdims=True) + s_kpe) * (1.0 / QK_HEAD) + RMS_EPS)
        kh = jnp.concatenate([kn * knw[:, :QK_NOPE], krot], axis=-1) * inv
        k_ref[hd] = kh.astype(BF16)
        vt_ref[hd, 0] = kv[:, v0 + hd * V_HEAD: v0 + (hd + 1) * V_HEAD].T.astype(BF16)


def _kvprep(ckv2d, kpe2d, p, rope_k, tm):
    t = ckv2d.shape[0]
    cos_t, sin_t = rope_k
    n_rope_blocks = cos_t.shape[0] // tm
    rope_spec = pl.BlockSpec((tm, QK_ROPE), lambda i: (i % n_rope_blocks, 0))
    row = lambda n: pl.BlockSpec((tm, n), lambda i: (i, 0))
    return pl.pallas_call(
        _kvprep_kernel, grid=(t // tm,),
        in_specs=[row(KV_LORA), row(QK_ROPE), rope_spec, rope_spec,
                  _const_spec(p['w_ukv'].shape), _const_spec(p['k_norm_w'].shape)],
        out_specs=(pl.BlockSpec((MLA_HEADS, tm, QK_HEAD), lambda i: (0, i, 0)),
                   pl.BlockSpec((MLA_HEADS, 1, V_HEAD, tm), lambda i: (0, i, 0, 0))),
        out_shape=(jax.ShapeDtypeStruct((MLA_HEADS, t, QK_HEAD), BF16),
                   jax.ShapeDtypeStruct((MLA_HEADS, t // tm, V_HEAD, tm), BF16)),
        compiler_params=_params("parallel"), name="kvprep",
    )(ckv2d, kpe2d, cos_t, sin_t, p['w_ukv'], p['k_norm_w'])


def _flash_kernel(q_ref, k_ref, vt_ref, o_ref, m_sc, l_sc, acc_sc, *, tq, tk, nk, past, hps):
    i = pl.program_id(2)
    q_first = past + i * tq
    q_last = q_first + tq - 1
    n_full = jnp.minimum(nk, ((q_first // CHUNK + 1) * CHUNK) // tk)
    n_vis = jnp.minimum(nk, ((q_last // CHUNK + 1) * CHUNK + tk - 1) // tk)

    m_sc[...] = jnp.full(m_sc.shape, MASK_VALUE, F32)
    l_sc[...] = jnp.zeros(l_sc.shape, F32)
    acc_sc[...] = jnp.zeros(acc_sc.shape, F32)

    def step(j, masked):
        k0 = pl.multiple_of(j * tk, tk)
        heads = range(hps)
        s_ = [_mm_nt(k_ref[hd, pl.ds(k0, tk), :], q_ref[hd]) for hd in heads]
        if masked:
            kpos = k0 + lax.broadcasted_iota(jnp.int32, (tk, 1), 0)
            qpos = q_first + lax.broadcasted_iota(jnp.int32, (1, tq), 1)
            vis = (kpos // CHUNK) <= (qpos // CHUNK)
            s_ = [jnp.where(vis, s, MASK_VALUE) for s in s_]
        p_, alpha_ = [], []
        for hd in heads:
            m_old = m_sc[hd]
            m_new = jnp.maximum(m_old, jnp.max(s_[hd], axis=0, keepdims=True))
            alpha = jnp.exp2(m_old - m_new)
            p = jnp.exp2(s_[hd] - m_new)
            l_sc[hd] = alpha * l_sc[hd] + jnp.sum(p, axis=0, keepdims=True)
            m_sc[hd] = m_new
            p_.append(p.astype(BF16))
            alpha_.append(alpha)
        for hd in heads:
            acc_sc[hd] = alpha_[hd] * acc_sc[hd] + _mm(vt_ref[hd, j], p_[hd])

    def full_body(j, c):
        step(j, False)
        return c

    def masked_body(j, c):
        step(j, True)
        return c

    lax.fori_loop(0, n_full, full_body, 0)
    lax.fori_loop(n_full, n_vis, masked_body, 0)
    for hd in range(hps):
        o_t = acc_sc[hd] / l_sc[hd]
        o_ref[:, hd * V_HEAD:(hd + 1) * V_HEAD] = o_t.T.astype(o_ref.dtype)


def _flash(q, k, vt, bsz, lq, lk, past, tq, tk, hps):
    nq = lq // tq
    nk = lk // tk
    kern = functools.partial(_flash_kernel, tq=tq, tk=tk, nk=nk, past=past, hps=hps)
    once = pl.Buffered(1)
    return pl.pallas_call(
        kern, grid=(bsz, MLA_HEADS // hps, nq),
        in_specs=[pl.BlockSpec((hps, tq, QK_HEAD), lambda b, h, i: (h, b * nq + i, 0)),
                  pl.BlockSpec((hps, lk, QK_HEAD), lambda b, h, i: (h, b, 0), pipeline_mode=once),
                  pl.BlockSpec((hps, nk, V_HEAD, tk), lambda b, h, i: (h, b, 0, 0), pipeline_mode=once)],
        out_specs=pl.BlockSpec((tq, hps * V_HEAD), lambda b, h, i: (b * nq + i, h)),
        out_shape=jax.ShapeDtypeStruct((bsz * lq, MLA_HEADS * V_HEAD), BF16),
        scratch_shapes=[pltpu.VMEM((hps, 1, tq), F32), pltpu.VMEM((hps, 1, tq), F32),
                        pltpu.VMEM((hps, V_HEAD, tq), F32)],
        compiler_params=_params("parallel", "parallel", "arbitrary"), name="flash",
    )(q, k, vt)


def _gdn_kernel(qkv_ref, z_ref, gates_ref, hist_ref, s0_ref, convw_ref, normw_ref,
                o_ref, s_ref, xbuf, *, tb):
    j = pl.program_id(1)
    nc = tb // CHUNK
    pad = 8

    @pl.when(j == 0)
    def _():
        xbuf[0:pad, :] = hist_ref[0]
        s_ref[...] = s0_ref[...]

    x = qkv_ref[...]
    xbuf[pad:pad + tb, :] = x
    cw = convw_ref[...]
    y = xbuf[pad - 3:pad - 3 + tb, :] * cw[0:1, :]
    for i in range(1, CONV_W):
        y = y + xbuf[pad - 3 + i:pad - 3 + i + tb, :] * cw[i:i + 1, :]
    xbuf[0:pad, :] = xbuf[tb:tb + pad, :]
    y = _silu(y)

    gates = gates_ref[...]
    gates_t = gates.T[0:8, :]
    row = lax.broadcasted_iota(jnp.int32, (tb, tb), 0)
    col = lax.broadcasted_iota(jnp.int32, (tb, tb), 1)
    same = (row // CHUNK) == (col // CHUNK)
    incl = same & (row >= col)
    strict = same & (row > col)
    tril01 = jnp.where(incl, 1.0, 0.0).astype(BF16)
    triu01 = jnp.where(same & (row <= col), 1.0, 0.0).astype(BF16)
    gc_cols = _mm_exact_lhs(tril01, gates)
    gc_rows = _mm_exact_rhs(gates_t, triu01)
    eye = jnp.where(row == col, 1.0, 0.0)

    heads = range(GDN_HEADS)
    q_, k_, v_, beta_, gc_, decay_, kb_ = [], [], [], [], [], [], []
    for hd in heads:
        qh = y[:, hd * GDN_DK:(hd + 1) * GDN_DK]
        kh = y[:, GDN_QK_DIM + hd * GDN_DK: GDN_QK_DIM + (hd + 1) * GDN_DK]
        q_.append(qh * (lax.rsqrt(jnp.sum(qh * qh, axis=-1, keepdims=True) + RMS_EPS) * (GDN_DK ** -0.5)))
        k_.append(kh * lax.rsqrt(jnp.sum(kh * kh, axis=-1, keepdims=True) + RMS_EPS))
        v_.append(y[:, 2 * GDN_QK_DIM + hd * GDN_DV: 2 * GDN_QK_DIM + (hd + 1) * GDN_DV])
        beta_.append(gates[:, hd:hd + 1])
        gc = gc_cols[:, GDN_HEADS + hd: GDN_HEADS + hd + 1]
        gr = gc_rows[GDN_HEADS + hd: GDN_HEADS + hd + 1, :]
        gc_.append(gc)
        decay_.append(jnp.where(incl, jnp.exp(jnp.where(incl, gc - gr, 0.0)), 0.0))
        kb_.append(k_[hd] * beta_[hd])
    khb_ = [k.astype(BF16) for k in k_]
    pw_ = [jnp.where(strict, -(_mm_nt(kb_[hd], khb_[hd]) * decay_[hd]), 0.0) for hd in heads]
    attn_ = [_mm_nt(q_[hd], khb_[hd]) * decay_[hd] for hd in heads]

    tinv_ = [eye + pw for pw in pw_]
    for _ in range(int(math.log2(CHUNK)) - 1):
        pw_ = [_mm_solve(pw, pw) for pw in pw_]
        tinv_ = [tinv_[hd] + _mm_solve(tinv_[hd], pw_[hd]) for hd in heads]

    eg_ = [jnp.exp(gc) for gc in gc_]
    sol_ = [_mm_solve(tinv_[hd], jnp.concatenate([v_[hd] * beta_[hd], kb_[hd] * eg_[hd]], axis=-1)) for hd in heads]
    qdec_ = [q_[hd] * eg_[hd] for hd in heads]

    s_ = [s_ref[0, hd] for hd in heads]
    us_ = [[] for _ in heads]
    oi_ = [[] for _ in heads]
    for c in range(nc):
        r0, r1 = c * CHUNK, (c + 1) * CHUNK
        for hd in heads:
            g_end = gc_[hd][r1 - 1:r1, :]
            k_end = k_[hd][r0:r1] * jnp.exp(g_end - gc_[hd][r0:r1])
            both = _mm(jnp.concatenate([sol_[hd][r0:r1, GDN_DV:], qdec_[hd][r0:r1]], axis=0), s_[hd])
            u = sol_[hd][r0:r1, :GDN_DV] - both[:CHUNK]
            oi_[hd].append(both[CHUNK:])
            us_[hd].append(u)
            s_[hd] = s_[hd] * jnp.exp(g_end) + _mm_tn(k_end, u)
    for hd in heads:
        s_ref[0, hd] = s_[hd]
        u_all = us_[hd][0] if nc == 1 else jnp.concatenate(us_[hd], axis=0)
        o_all = oi_[hd][0] if nc == 1 else jnp.concatenate(oi_[hd], axis=0)
        o = o_all + _mm(attn_[hd], u_all)
        o = _rms(o, normw_ref[...]) * _silu(z_ref[:, hd * GDN_DV:(hd + 1) * GDN_DV])
        o_ref[:, hd * GDN_DV:(hd + 1) * GDN_DV] = o.astype(o_ref.dtype)


def _gdn(qkv2d, z2d, gates2d, hist_pad, s0, p, bsz, seq, tb):
    nb = seq // tb
    kern = functools.partial(_gdn_kernel, tb=tb)
    row = lambda n: pl.BlockSpec((tb, n), lambda b, j: (b * nb + j, 0))
    state_spec = pl.BlockSpec((1, GDN_HEADS, GDN_DK, GDN_DV), lambda b, j: (b, 0, 0, 0))
    return pl.pallas_call(
        kern, grid=(bsz, nb),
        in_specs=[row(GDN_CONV_DIM), row(GDN_V_DIM), row(LANES),
                  pl.BlockSpec((1, 8, GDN_CONV_DIM), lambda b, j: (b, 0, 0)),
                  state_spec, _const_spec(p['conv_w'].shape), _const_spec(p['gdn_norm_w'].shape)],
        out_specs=(row(GDN_V_DIM), state_spec),
        out_shape=(jax.ShapeDtypeStruct((bsz * seq, GDN_V_DIM), BF16),
                   jax.ShapeDtypeStruct((bsz, GDN_HEADS, GDN_DK, GDN_DV), F32)),
        scratch_shapes=[pltpu.VMEM((tb + 8, GDN_CONV_DIM), F32)],
        compiler_params=_params("parallel", "arbitrary"), name="gdn",
    )(qkv2d, z2d, gates2d, hist_pad, s0, p['conv_w'], p['gdn_norm_w'])


def _outproj_kernel(x_ref, g_ref, a_ref, wog_ref, woa_ref, lnw_ref, wr_ref, br_ref,
                    x1_ref, h2_ref, comb_ref):
    x1 = x_ref[...] + _mm(g_ref[...], wog_ref[...]) + _mm(a_ref[...], woa_ref[...])
    x1_ref[...] = x1
    h2 = _rms(x1, lnw_ref[...])
    h2_ref[...] = h2.astype(BF16)

    logits = _mm3(h2, wr_ref[...]) + br_ref[...]
    lane = lax.broadcasted_iota(jnp.int32, logits.shape, 1)
    big = jnp.int32(LANES)
    is_grp = (lane >= N_EXPERTS) & (lane < N_EXPERTS + N_GROUPS)
    gl = jnp.where(is_grp, logits, -jnp.inf)
    gmax = jnp.max(gl, axis=-1, keepdims=True)
    grp_p = 1.0 / jnp.sum(jnp.exp(gl - gmax), axis=-1, keepdims=True)
    grp_idx = jnp.min(jnp.where(gl == gmax, lane - N_EXPERTS, big), axis=-1, keepdims=True)

    in_grp = (lane < N_EXPERTS) & ((lane // EXPERTS_PER_GROUP) == grp_idx)
    el = jnp.where(in_grp, logits, -jnp.inf)
    m1 = jnp.max(el, axis=-1, keepdims=True)
    i1 = jnp.min(jnp.where(el == m1, lane, big), axis=-1, keepdims=True)
    el2 = jnp.where(lane == i1, -jnp.inf, el)
    m2 = jnp.max(el2, axis=-1, keepdims=True)
    i2 = jnp.min(jnp.where(el2 == m2, lane, big), axis=-1, keepdims=True)
    e2 = jnp.exp(m2 - m1)
    w1 = 1.0 / (1.0 + e2)
    w2 = e2 * w1
    comb_ref[...] = jnp.where(lane == i1, grp_p * w1, jnp.where(lane == i2, grp_p * w2, 0.0))


def _outproj(x2d, gdn_out, mla_out, p, tm):
    t = x2d.shape[0]
    row = lambda n: pl.BlockSpec((tm, n), lambda i: (i, 0))
    consts = [p['w_out_gdn'], p['w_out_mla'], p['ln_ffn_w'], p['w_router'], p['b_router']]
    return pl.pallas_call(
        _outproj_kernel, grid=(t // tm,),
        in_specs=[row(D_MODEL), row(GDN_V_DIM), row(MLA_HEADS * V_HEAD)] + [_const_spec(c.shape) for c in consts],
        out_specs=(row(D_MODEL), row(D_MODEL), row(LANES)),
        out_shape=(jax.ShapeDtypeStruct((t, D_MODEL), F32), jax.ShapeDtypeStruct((t, D_MODEL), BF16),
                   jax.ShapeDtypeStruct((t, LANES), F32)),
        compiler_params=_params("parallel"), name="outproj",
    )(x2d, gdn_out, mla_out, *consts)


def _moe_kernel(h_ref, comb_ref, x1_ref, wg_ref, wu_ref, wd_ref, y_ref):
    e = pl.program_id(1)

    @pl.when(e == 0)
    def _():
        y_ref[...] = x1_ref[...]

    h = h_ref[...]
    comb = comb_ref[...]
    lane = lax.broadcasted_iota(jnp.int32, comb.shape, 1)
    c = jnp.sum(jnp.where(lane == e, comb, 0.0), axis=-1, keepdims=True)
    hid = _silu(_mm(h, wg_ref[0])) * _mm(h, wu_ref[0]) * c
    y_ref[...] += _mm(hid, wd_ref[0])


def _moe(h2, comb, x1, p, tm):
    t = h2.shape[0]
    row = lambda n: pl.BlockSpec((tm, n), lambda i, e: (i, 0))
    return pl.pallas_call(
        _moe_kernel, grid=(t // tm, N_EXPERTS),
        in_specs=[row(D_MODEL), row(LANES), row(D_MODEL),
                  pl.BlockSpec((1, D_MODEL, D_EXPERT), lambda i, e: (e, 0, 0)),
                  pl.BlockSpec((1, D_MODEL, D_EXPERT), lambda i, e: (e, 0, 0)),
                  pl.BlockSpec((1, D_EXPERT, D_MODEL), lambda i, e: (e, 0, 0))],
        out_specs=row(D_MODEL),
        out_shape=jax.ShapeDtypeStruct((t, D_MODEL), F32),
        compiler_params=_params("parallel", "arbitrary"), name="moe",
    )(h2, comb, x1, p['w_exp_gate'], p['w_exp_up'], p['w_exp_down'])


def _rope_tables(pos):
    inv = ROPE_THETA ** (-jnp.arange(ROPE_HALF, dtype=F32) / ROPE_HALF)
    ang = pos.astype(F32)[:, None] * inv[None, :]
    cos, sin = jnp.cos(ang), jnp.sin(ang)
    return jnp.concatenate([cos, cos], axis=-1), jnp.concatenate([-sin, sin], axis=-1)


def _tile_rows(tables, tm):
    n = tables[0].shape[0]
    if n >= tm:
        assert n % tm == 0
        return tables
    assert tm % n == 0
    return tuple(jnp.tile(t, (tm // n, 1)) for t in tables)


def _pad_lanes(v, offset):
    return jnp.zeros((1, LANES), F32).at[0, offset:offset + v.shape[0]].set(v.astype(F32))


def _prepare_params(p):
    w_in = p['w_in']
    sizes = (GDN_CONV_DIM, GDN_V_DIM, GDN_HEADS, GDN_HEADS, Q_LORA, KV_LORA, QK_ROPE)
    offs = np.concatenate([[0], np.cumsum(sizes)])
    part = [w_in[:, int(offs[i]):int(offs[i + 1])] for i in range(len(sizes))]
    w_ba = jnp.zeros((D_MODEL, LANES), F32).at[:, :2 * GDN_HEADS].set(jnp.concatenate([part[2], part[3]], axis=1))

    uq = p['mla_w_uq'].reshape(Q_LORA, MLA_HEADS, QK_HEAD)
    w_uq = jnp.concatenate([uq[:, :, :QK_NOPE].reshape(Q_LORA, -1), uq[:, :, QK_NOPE:].reshape(Q_LORA, -1)], axis=1)
    ukv = p['mla_w_ukv'].reshape(KV_LORA, MLA_HEADS, QK_NOPE + V_HEAD)
    w_ukv = jnp.concatenate([ukv[:, :, :QK_NOPE].reshape(KV_LORA, -1), ukv[:, :, QK_NOPE:].reshape(KV_LORA, -1)], axis=1)

    w_router = jnp.zeros((D_MODEL, LANES), F32)
    w_router = w_router.at[:, :N_EXPERTS].set(p['w_expert_router'])
    w_router = w_router.at[:, N_EXPERTS:N_EXPERTS + N_GROUPS].set(p['w_group_router'])
    b_router = _pad_lanes(jnp.concatenate([p['b_expert_router'], p['b_group_router']]), 0)

    row = lambda v: v.astype(F32)[None, :]
    return {
        'ln_mix_w': row(p['ln_mix_w']),
        'w_qkv': part[0].astype(BF16), 'w_z': part[1].astype(BF16), 'w_ba': w_ba,
        'a_log': _pad_lanes(p['gdn_a_log'], GDN_HEADS), 'dt_bias': _pad_lanes(p['gdn_dt_bias'], GDN_HEADS),
        'w_cq': part[4].astype(BF16), 'w_ckv': part[5].astype(BF16), 'w_kpe': part[6].astype(BF16),
        'q_a_norm_w': row(p['mla_q_a_norm_w']), 'kv_a_norm_w': row(p['mla_kv_a_norm_w']),
        'w_uq': w_uq.astype(BF16), 'w_ukv': w_ukv.astype(BF16),
        'q_norm_w': row(p['mla_q_norm_w']), 'k_norm_w': row(p['mla_k_norm_w']),
        'conv_w': p['gdn_conv_w'].astype(F32), 'gdn_norm_w': row(p['gdn_norm_w']),
        'w_out_gdn': p['w_out'][:GDN_V_DIM].astype(BF16), 'w_out_mla': p['w_out'][GDN_V_DIM:].astype(BF16),
        'ln_ffn_w': row(p['ln_ffn_w']), 'w_router': w_router, 'b_router': b_router,
        'w_exp_gate': p['w_exp_gate'].astype(BF16), 'w_exp_up': p['w_exp_up'].astype(BF16),
        'w_exp_down': p['w_exp_down'].astype(BF16),
    }


def _tiles(bsz, seq, past):
    t = bsz * seq
    tm = min(512, t)
    n_keys = past + seq
    if past == 0:
        tq = tk = min(512, seq)
        lk = n_keys
    else:
        assert n_keys % CHUNK == 0
        tq = seq
        lk = -(-n_keys // LANES) * LANES
        tk = lk
    tb = min(256, seq)
    tmoe = min(1024, t)
    assert seq % tq == 0 and lk % tk == 0 and seq % tb == 0 and tb % CHUNK == 0 and t % tm == 0 and t % tmoe == 0
    return tm, tq, tk, lk, tb, tmoe


def _hybrid_layer(x, conv_hist, s0, ckv_past, kpe_past, p):
    bsz, seq, _ = x.shape
    past = ckv_past.shape[1]
    t = bsz * seq
    tm, tq, tk, lk, tb, tmoe = _tiles(bsz, seq, past)
    x2d = x.reshape(t, D_MODEL)

    rope_q = _tile_rows(_rope_tables(past + jnp.arange(seq, dtype=jnp.int32)), tm)
    qkv, z, gates, q, ckv, kpe = _inproj(x2d, p, rope_q, tm)

    n_keys = past + seq
    ckv3, kpe3 = ckv.reshape(bsz, seq, KV_LORA), kpe.reshape(bsz, seq, QK_ROPE)
    if lk == seq:
        ckv_all, kpe_all = ckv, kpe
    else:
        zpad = lambda n: jnp.zeros((bsz, lk - n_keys, n), F32)
        ckv_all = jnp.concatenate([ckv_past.astype(F32), ckv3, zpad(KV_LORA)], axis=1).reshape(bsz * lk, KV_LORA)
        kpe_all = jnp.concatenate([kpe_past.astype(F32), kpe3, zpad(QK_ROPE)], axis=1).reshape(bsz * lk, QK_ROPE)
    rope_k = _tile_rows(_rope_tables(jnp.arange(lk, dtype=jnp.int32)), tk)
    k, vt = _kvprep(ckv_all, kpe_all, p, rope_k, tk)
    mla_out = _flash(q, k, vt, bsz, seq, lk, past, tq, tk, FLASH_HEADS_PER_STEP)

    hist_pad = jnp.concatenate([jnp.zeros((bsz, 8 - (CONV_W - 1), GDN_CONV_DIM), F32), conv_hist.astype(F32)], axis=1)
    gdn_out, s_new = _gdn(qkv, z, gates, hist_pad, s0.astype(F32), p, bsz, seq, tb)
    assert seq >= CONV_W - 1
    conv_new = qkv.reshape(bsz, seq, GDN_CONV_DIM)[:, seq - (CONV_W - 1):]

    x1, h2, comb = _outproj(x2d, gdn_out, mla_out, p, tm)
    y = _moe(h2, comb, x1, p, tmoe)
    return y.reshape(bsz, seq, D_MODEL), ckv3, kpe3, s_new.astype(s0.dtype), conv_new


def kernel(x_prompt, x_sample, cache_mla_ckv, cache_mla_kpe, state_gdn, state_gdn_conv, ln_mix_w, w_in, gdn_conv_w, gdn_a_log, gdn_dt_bias, gdn_norm_w, mla_q_a_norm_w, mla_w_uq, mla_kv_a_norm_w, mla_w_ukv, mla_q_norm_w, mla_k_norm_w, w_out, ln_ffn_w, w_group_router, b_group_router, w_expert_router, b_expert_router, w_exp_gate, w_exp_up, w_exp_down):
    depth = w_in.shape[0]
    bsz = x_prompt.shape[0]
    y_p, y_s = x_prompt, x_sample
    outs_p, outs_s = [], []
    for l in range(depth):
        p = _prepare_params({
            'ln_mix_w': ln_mix_w[l], 'w_in': w_in[l], 'gdn_conv_w': gdn_conv_w[l],
            'gdn_a_log': gdn_a_log[l], 'gdn_dt_bias': gdn_dt_bias[l], 'gdn_norm_w': gdn_norm_w[l],
            'mla_q_a_norm_w': mla_q_a_norm_w[l], 'mla_w_uq': mla_w_uq[l],
            'mla_kv_a_norm_w': mla_kv_a_norm_w[l], 'mla_w_ukv': mla_w_ukv[l],
            'mla_q_norm_w': mla_q_norm_w[l], 'mla_k_norm_w': mla_k_norm_w[l],
            'w_out': w_out[l], 'ln_ffn_w': ln_ffn_w[l],
            'w_group_router': w_group_router[l], 'b_group_router': b_group_router[l],
            'w_expert_router': w_expert_router[l], 'b_expert_router': b_expert_router[l],
            'w_exp_gate': w_exp_gate[l], 'w_exp_up': w_exp_up[l], 'w_exp_down': w_exp_down[l],
        })
        y_p, *st_p = _hybrid_layer(
            y_p,
            jnp.zeros((bsz, CONV_W - 1, GDN_CONV_DIM), y_p.dtype),
            jnp.zeros((bsz, GDN_HEADS, GDN_DK, GDN_DV), state_gdn.dtype),
            jnp.zeros((bsz, 0, KV_LORA), cache_mla_ckv.dtype),
            jnp.zeros((bsz, 0, QK_ROPE), cache_mla_kpe.dtype),
            p)
        y_s, *st_s = _hybrid_layer(y_s, state_gdn_conv[l], state_gdn[l], cache_mla_ckv[l], cache_mla_kpe[l], p)
        outs_p.append(st_p)
        outs_s.append(st_s)
    stack = lambda outs, i: jnp.stack([o[i] for o in outs])
    return (y_p, y_s,
            stack(outs_p, 0), stack(outs_p, 1), stack(outs_p, 2), stack(outs_p, 3),
            stack(outs_s, 0), stack(outs_s, 1), stack(outs_s, 2), stack(outs_s, 3))
```

```python
import functools
import math

import jax
import jax.numpy as jnp
import numpy as np
from jax import lax
from jax.experimental import pallas as pl
from jax.experimental.pallas import tpu as pltpu

F32 = jnp.float32
BF16 = jnp.bfloat16

D_MODEL = 1024
CHUNK = 64
RMS_EPS = 1e-6

GDN_HEADS = 4
GDN_DK = 128
GDN_DV = 128
GDN_QK_DIM = GDN_HEADS * GDN_DK
GDN_V_DIM = GDN_HEADS * GDN_DV
GDN_CONV_DIM = 2 * GDN_QK_DIM + GDN_V_DIM
CONV_W = 4

MLA_HEADS = 4
Q_LORA = 512
KV_LORA = 256
QK_NOPE = 128
QK_ROPE = 64
ROPE_HALF = QK_ROPE // 2
V_HEAD = 128
V_ROWS = V_HEAD + 16
QK_HEAD = QK_NOPE + QK_ROPE
ROPE_THETA = 10000.0

N_GROUPS = 4
EXPERTS_PER_GROUP = 8
N_EXPERTS = N_GROUPS * EXPERTS_PER_GROUP
D_EXPERT = 256

LANES = 128
VMEM_LIMIT_BYTES = 56 * 1024 * 1024

Q_SCALE = (QK_HEAD ** -0.5) * math.log2(math.e)
MASK_VALUE = -1e30
FLASH_HEADS_PER_STEP = 2
GDN_SOLVE_PASSES = 1
MOE_SEG_ALIGN = 16
MOE_ROW_BLOCK = 128
MOE_EXPERTS_PER_STEP = 4


def _mm(a, b):
    return jnp.dot(a.astype(BF16), b.astype(BF16), preferred_element_type=F32)


def _mm_nt(a, b):
    return lax.dot_general(a.astype(BF16), b.astype(BF16), (((1,), (1,)), ((), ())),
                           preferred_element_type=F32)


def _mm_tn(a, b):
    return lax.dot_general(a.astype(BF16), b.astype(BF16), (((0,), (0,)), ((), ())),
                           preferred_element_type=F32)


def _split(a):
    hi = a.astype(BF16)
    lo = (a - hi.astype(F32)).astype(BF16)
    return hi, lo


def _mm3(a, b):
    ah, al = _split(a)
    bh, bl = _split(b)
    return _mm(ah, bh) + (_mm(ah, bl) + _mm(al, bh))


def _mm_solve(a, b):
    return _mm3(a, b) if GDN_SOLVE_PASSES == 3 else _mm(a, b)


def _mm_exact_rhs(a, b01):
    a1 = a.astype(BF16)
    r1 = a - a1.astype(F32)
    a2 = r1.astype(BF16)
    a3 = (r1 - a2.astype(F32)).astype(BF16)
    return _mm(a1, b01) + (_mm(a2, b01) + _mm(a3, b01))


def _mm_exact_lhs(a01, b):
    b1 = b.astype(BF16)
    r1 = b - b1.astype(F32)
    b2 = r1.astype(BF16)
    b3 = (r1 - b2.astype(F32)).astype(BF16)
    return _mm(a01, b1) + (_mm(a01, b2) + _mm(a01, b3))


def _rms(x, w):
    return x * lax.rsqrt(jnp.mean(x * x, axis=-1, keepdims=True) + RMS_EPS) * w


def _silu(x):
    return x * jax.nn.sigmoid(x)


def _swap_halves(x):
    h = x.shape[-1] // 2
    return jnp.concatenate([x[:, h:], x[:, :h]], axis=-1)


def _const_spec(shape):
    nd = len(shape)
    return pl.BlockSpec(shape, lambda *_: (0,) * nd)


def _params(*sem):
    return pltpu.CompilerParams(dimension_semantics=sem, vmem_limit_bytes=VMEM_LIMIT_BYTES)


def _inproj_kernel(x_ref, lnw_ref, wqkv_ref, wz_ref, wba_ref, alog_ref, dtb_ref, wcq_ref, qanw_ref,
                   wuq_ref, qnw_ref, cos_ref, sin_ref, wckv_ref, kvanw_ref, wkpe_ref,
                   qkv_ref, z_ref, gates_ref, q_ref, ckv_ref, kpe_ref):
    h = _rms(x_ref[...], lnw_ref[...])
    hb = h.astype(BF16)
    qkv_ref[...] = _mm(hb, wqkv_ref[...])
    z_ref[...] = _mm(hb, wz_ref[...])

    raw = _mm3(h, wba_ref[...])
    lane = lax.broadcasted_iota(jnp.int32, raw.shape, 1)
    pre = raw + dtb_ref[...]
    softplus = jnp.maximum(pre, 0.0) + jnp.log1p(jnp.exp(-jnp.abs(pre)))
    g = -jnp.exp(alog_ref[...]) * softplus
    gates_ref[...] = jnp.where(lane < GDN_HEADS, jax.nn.sigmoid(raw), g)

    ckv_ref[...] = _rms(_mm(hb, wckv_ref[...]), kvanw_ref[...])
    kpe_ref[...] = _mm(hb, wkpe_ref[...])

    cqn = _rms(_mm(hb, wcq_ref[...]), qanw_ref[...])
    qall = _mm(cqn, wuq_ref[...])
    qnw = qnw_ref[...]
    cosf, sinf = cos_ref[...], sin_ref[...]
    rope0 = MLA_HEADS * QK_NOPE
    for hd in range(MLA_HEADS):
        nope = qall[:, hd * QK_NOPE:(hd + 1) * QK_NOPE]
        rr = qall[:, rope0 + hd * QK_ROPE: rope0 + (hd + 1) * QK_ROPE]
        ssq = jnp.sum(nope * nope, axis=-1, keepdims=True) + jnp.sum(rr * rr, axis=-1, keepdims=True)
        inv = lax.rsqrt(ssq * (1.0 / QK_HEAD) + RMS_EPS) * Q_SCALE
        rw = rr * qnw[:, QK_NOPE:]
        rot = rw * cosf + _swap_halves(rw) * sinf
        qh = jnp.concatenate([nope * qnw[:, :QK_NOPE], rot], axis=-1) * inv
        q_ref[hd] = qh.astype(BF16)


def _inproj(x2d, p, rope_q, tm):
    t = x2d.shape[0]
    nt = t // tm
    cos_t, sin_t = rope_q
    n_rope_blocks = cos_t.shape[0] // tm
    rope_spec = pl.BlockSpec((tm, QK_ROPE), lambda i: (i % n_rope_blocks, 0))
    row = lambda n: pl.BlockSpec((tm, n), lambda i: (i, 0))
    consts = [p['ln_mix_w'], p['w_qkv'], p['w_z'], p['w_ba'], p['a_log'], p['dt_bias'], p['w_cq'],
              p['q_a_norm_w'], p['w_uq'], p['q_norm_w']]
    consts2 = [p['w_ckv'], p['kv_a_norm_w'], p['w_kpe']]
    in_specs = ([row(D_MODEL)] + [_const_spec(c.shape) for c in consts] + [rope_spec, rope_spec]
                + [_const_spec(c.shape) for c in consts2])
    out_shape = (
        jax.ShapeDtypeStruct((t, GDN_CONV_DIM), F32),
        jax.ShapeDtypeStruct((t, GDN_V_DIM), F32),
        jax.ShapeDtypeStruct((t, LANES), F32),
        jax.ShapeDtypeStruct((MLA_HEADS, t, QK_HEAD), BF16),
        jax.ShapeDtypeStruct((t, KV_LORA), F32),
        jax.ShapeDtypeStruct((t, QK_ROPE), F32),
    )
    out_specs = (row(GDN_CONV_DIM), row(GDN_V_DIM), row(LANES),
                 pl.BlockSpec((MLA_HEADS, tm, QK_HEAD), lambda i: (0, i, 0)),
                 row(KV_LORA), row(QK_ROPE))
    return pl.pallas_call(
        _inproj_kernel, grid=(nt,), in_specs=in_specs, out_specs=out_specs, out_shape=out_shape,
        compiler_params=_params("parallel"), name="inproj",
    )(x2d, *consts, cos_t, sin_t, *consts2)


def _kvprep_kernel(ckv_ref, kpe_ref, cos_ref, sin_ref, wukv_ref, knw_ref, k_ref, vt_ref):
    kv = _mm(ckv_ref[...], wukv_ref[...])
    kpe = kpe_ref[...]
    knw = knw_ref[...]
    kw = kpe * knw[:, QK_NOPE:]
    krot = kw * cos_ref[...] + _swap_halves(kw) * sin_ref[...]
    s_kpe = jnp.sum(kpe * kpe, axis=-1, keepdims=True)
    v0 = MLA_HEADS * QK_NOPE
    pad_row = lax.broadcasted_iota(jnp.int32, (V_ROWS - V_HEAD, kv.shape[0]), 0)
    ones_rows = jnp.where(pad_row == 0, 1.0, 0.0)
    for hd in range(MLA_HEADS):
        kn = kv[:, hd * QK_NOPE:(hd + 1) * QK_NOPE]
        inv = lax.rsqrt((jnp.sum(kn * kn, axis=-1, keepdims=True) + s_kpe) * (1.0 / QK_HEAD) + RMS_EPS)
        kh = jnp.concatenate([kn * knw[:, :QK_NOPE], krot], axis=-1) * inv
        k_ref[hd] = kh.astype(BF16)
        vt = kv[:, v0 + hd * V_HEAD: v0 + (hd + 1) * V_HEAD].T
        vt_ref[hd, 0] = jnp.concatenate([vt, ones_rows], axis=0).astype(BF16)


def _kvprep(ckv2d, kpe2d, p, rope_k, tm):
    t = ckv2d.shape[0]
    cos_t, sin_t = rope_k
    n_rope_blocks = cos_t.shape[0] // tm
    rope_spec = pl.BlockSpec((tm, QK_ROPE), lambda i: (i % n_rope_blocks, 0))
    row = lambda n: pl.BlockSpec((tm, n), lambda i: (i, 0))
    return pl.pallas_call(
        _kvprep_kernel, grid=(t // tm,),
        in_specs=[row(KV_LORA), row(QK_ROPE), rope_spec, rope_spec,
                  _const_spec(p['w_ukv'].shape), _const_spec(p['k_norm_w'].shape)],
        out_specs=(pl.BlockSpec((MLA_HEADS, tm, QK_HEAD), lambda i: (0, i, 0)),
                   pl.BlockSpec((MLA_HEADS, 1, V_ROWS, tm), lambda i: (0, i, 0, 0))),
        out_shape=(jax.ShapeDtypeStruct((MLA_HEADS, t, QK_HEAD), BF16),
                   jax.ShapeDtypeStruct((MLA_HEADS, t // tm, V_ROWS, tm), BF16)),
        compiler_params=_params("parallel"), name="kvprep",
    )(ckv2d, kpe2d, cos_t, sin_t, p['w_ukv'], p['k_norm_w'])


def _flash_kernel(q_ref, k_ref, vt_ref, o_ref, m_sc, acc_sc, *, tq, tk, nk, past, hps):
    i = pl.program_id(2)
    q_first = past + i * tq
    q_last = q_first + tq - 1
    n_full = jnp.minimum(nk, ((q_first // CHUNK + 1) * CHUNK) // tk)
    n_vis = jnp.minimum(nk, ((q_last // CHUNK + 1) * CHUNK + tk - 1) // tk)

    m_sc[...] = jnp.full(m_sc.shape, MASK_VALUE, F32)
    acc_sc[...] = jnp.zeros(acc_sc.shape, F32)

    def step(j, masked):
        k0 = pl.multiple_of(j * tk, tk)
        heads = range(hps)
        s_ = [_mm_nt(k_ref[hd, pl.ds(k0, tk), :], q_ref[hd]) for hd in heads]
        if masked:
            kpos = k0 + lax.broadcasted_iota(jnp.int32, (tk, 1), 0)
            qpos = q_first + lax.broadcasted_iota(jnp.int32, (1, tq), 1)
            vis = (kpos // CHUNK) <= (qpos // CHUNK)
            s_ = [jnp.where(vis, s, MASK_VALUE) for s in s_]
        p_, alpha_ = [], []
        for hd in heads:
            m_old = m_sc[hd]
            m_new = jnp.maximum(m_old, jnp.max(s_[hd], axis=0, keepdims=True))
            alpha_.append(jnp.exp2(m_old - m_new))
            p_.append(jnp.exp2(s_[hd] - m_new).astype(BF16))
            m_sc[hd] = m_new
        for hd in heads:
            acc_sc[hd] = alpha_[hd] * acc_sc[hd] + _mm(vt_ref[hd, j], p_[hd])

    def full_body(j, c):
        step(j, False)
        return c

    def masked_body(j, c):
        step(j, True)
        return c

    lax.fori_loop(0, n_full, full_body, 0)
    lax.fori_loop(n_full, n_vis, masked_body, 0)
    for hd in range(hps):
        o_t = acc_sc[hd, :V_HEAD, :] / acc_sc[hd, V_HEAD:V_HEAD + 1, :]
        o_ref[:, hd * V_HEAD:(hd + 1) * V_HEAD] = o_t.T.astype(o_ref.dtype)


def _flash(q, k, vt, bsz, lq, lk, past, tq, tk, hps):
    nq = lq // tq
    nk = lk // tk
    kern = functools.partial(_flash_kernel, tq=tq, tk=tk, nk=nk, past=past, hps=hps)
    once = pl.Buffered(1)
    return pl.pallas_call(
        kern, grid=(bsz, MLA_HEADS // hps, nq),
        in_specs=[pl.BlockSpec((hps, tq, QK_HEAD), lambda b, h, i: (h, b * nq + i, 0)),
                  pl.BlockSpec((hps, lk, QK_HEAD), lambda b, h, i: (h, b, 0), pipeline_mode=once),
                  pl.BlockSpec((hps, nk, V_ROWS, tk), lambda b, h, i: (h, b, 0, 0), pipeline_mode=once)],
        out_specs=pl.BlockSpec((tq, hps * V_HEAD), lambda b, h, i: (b * nq + i, h)),
        out_shape=jax.ShapeDtypeStruct((bsz * lq, MLA_HEADS * V_HEAD), BF16),
        scratch_shapes=[pltpu.VMEM((hps, 1, tq), F32), pltpu.VMEM((hps, V_ROWS, tq), F32)],
        compiler_params=_params("parallel", "parallel", "arbitrary"), name="flash",
    )(q, k, vt)


def _gdn_kernel(qkv_ref, z_ref, gates_ref, hist_ref, s0_ref, convw_ref, normw_ref,
                o_ref, s_ref, xbuf, *, tb):
    j = pl.program_id(1)
    nc = tb // CHUNK
    pad = 8

    @pl.when(j == 0)
    def _():
        xbuf[0:pad, :] = hist_ref[0]
        s_ref[...] = s0_ref[...]

    x = qkv_ref[...]
    xbuf[pad:pad + tb, :] = x
    cw = convw_ref[...]
    y = xbuf[pad - 3:pad - 3 + tb, :] * cw[0:1, :]
    for i in range(1, CONV_W):
        y = y + xbuf[pad - 3 + i:pad - 3 + i + tb, :] * cw[i:i + 1, :]
    xbuf[0:pad, :] = xbuf[tb:tb + pad, :]
    y = _silu(y)

    gates = gates_ref[...]
    gates_t = gates.T[0:8, :]
    row = lax.broadcasted_iota(jnp.int32, (tb, tb), 0)
    col = lax.broadcasted_iota(jnp.int32, (tb, tb), 1)
    same = (row // CHUNK) == (col // CHUNK)
    incl = same & (row >= col)
    strict = same & (row > col)
    tril01 = jnp.where(incl, 1.0, 0.0).astype(BF16)
    triu01 = jnp.where(same & (row <= col), 1.0, 0.0).astype(BF16)
    gc_cols = _mm_exact_lhs(tril01, gates)
    gc_rows = _mm_exact_rhs(gates_t, triu01)
    eye = jnp.where(row == col, 1.0, 0.0)

    heads = range(GDN_HEADS)
    q_, k_, v_, beta_, gc_, decay_, kb_ = [], [], [], [], [], [], []
    for hd in heads:
        qh = y[:, hd * GDN_DK:(hd + 1) * GDN_DK]
        kh = y[:, GDN_QK_DIM + hd * GDN_DK: GDN_QK_DIM + (hd + 1) * GDN_DK]
        q_.append(qh * (lax.rsqrt(jnp.sum(qh * qh, axis=-1, keepdims=True) + RMS_EPS) * (GDN_DK ** -0.5)))
        k_.append(kh * lax.rsqrt(jnp.sum(kh * kh, axis=-1, keepdims=True) + RMS_EPS))
        v_.append(y[:, 2 * GDN_QK_DIM + hd * GDN_DV: 2 * GDN_QK_DIM + (hd + 1) * GDN_DV])
        beta_.append(gates[:, hd:hd + 1])
        gc = gc_cols[:, GDN_HEADS + hd: GDN_HEADS + hd + 1]
        gr = gc_rows[GDN_HEADS + hd: GDN_HEADS + hd + 1, :]
        gc_.append(gc)
        decay_.append(jnp.where(incl, jnp.exp(jnp.where(incl, gc - gr, 0.0)), 0.0))
        kb_.append(k_[hd] * beta_[hd])
    khb_ = [k.astype(BF16) for k in k_]
    pw_ = [jnp.where(strict, -(_mm_nt(kb_[hd], khb_[hd]) * decay_[hd]), 0.0) for hd in heads]
    attn_ = [_mm_nt(q_[hd], khb_[hd]) * decay_[hd] for hd in heads]

    tinv_ = [eye + pw for pw in pw_]
    for _ in range(int(math.log2(CHUNK)) - 1):
        pw_ = [_mm_solve(pw, pw) for pw in pw_]
        tinv_ = [tinv_[hd] + _mm_solve(tinv_[hd], pw_[hd]) for hd in heads]

    eg_ = [jnp.exp(gc) for gc in gc_]
    sol_ = [_mm_solve(tinv_[hd], jnp.concatenate([v_[hd] * beta_[hd], kb_[hd] * eg_[hd]], axis=-1)) for hd in heads]
    qdec_ = [q_[hd] * eg_[hd] for hd in heads]

    s_ = [s_ref[0, hd] for hd in heads]
    us_ = [[] for _ in heads]
    oi_ = [[] for _ in heads]
    for c in range(nc):
        r0, r1 = c * CHUNK, (c + 1) * CHUNK
        for hd in heads:
            g_end = gc_[hd][r1 - 1:r1, :]
            k_end = k_[hd][r0:r1] * jnp.exp(g_end - gc_[hd][r0:r1])
            both = _mm(jnp.concatenate([sol_[hd][r0:r1, GDN_DV:], qdec_[hd][r0:r1]], axis=0), s_[hd])
            u = sol_[hd][r0:r1, :GDN_DV] - both[:CHUNK]
            oi_[hd].append(both[CHUNK:])
            us_[hd].append(u)
            s_[hd] = s_[hd] * jnp.exp(g_end) + _mm_tn(k_end, u)
    for hd in heads:
        s_ref[0, hd] = s_[hd]
        u_all = us_[hd][0] if nc == 1 else jnp.concatenate(us_[hd], axis=0)
        o_all = oi_[hd][0] if nc == 1 else jnp.concatenate(oi_[hd], axis=0)
        o = o_all + _mm(attn_[hd], u_all)
        o = _rms(o, normw_ref[...]) * _silu(z_ref[:, hd * GDN_DV:(hd + 1) * GDN_DV])
        o_ref[:, hd * GDN_DV:(hd + 1) * GDN_DV] = o.astype(o_ref.dtype)


def _gdn(qkv2d, z2d, gates2d, hist_pad, s0, p, bsz, seq, tb):
    nb = seq // tb
    kern = functools.partial(_gdn_kernel, tb=tb)
    row = lambda n: pl.BlockSpec((tb, n), lambda b, j: (b * nb + j, 0))
    state_spec = pl.BlockSpec((1, GDN_HEADS, GDN_DK, GDN_DV), lambda b, j: (b, 0, 0, 0))
    return pl.pallas_call(
        kern, grid=(bsz, nb),
        in_specs=[row(GDN_CONV_DIM), row(GDN_V_DIM), row(LANES),
                  pl.BlockSpec((1, 8, GDN_CONV_DIM), lambda b, j: (b, 0, 0)),
                  state_spec, _const_spec(p['conv_w'].shape), _const_spec(p['gdn_norm_w'].shape)],
        out_specs=(row(GDN_V_DIM), state_spec),
        out_shape=(jax.ShapeDtypeStruct((bsz * seq, GDN_V_DIM), BF16),
                   jax.ShapeDtypeStruct((bsz, GDN_HEADS, GDN_DK, GDN_DV), F32)),
        scratch_shapes=[pltpu.VMEM((tb + 8, GDN_CONV_DIM), F32)],
        compiler_params=_params("parallel", "arbitrary"), name="gdn",
    )(qkv2d, z2d, gates2d, hist_pad, s0, p['conv_w'], p['gdn_norm_w'])


def _outproj_kernel(x_ref, g_ref, a_ref, wog_ref, woa_ref, lnw_ref, wr_ref, br_ref,
                    x1_ref, h2_ref, comb_ref):
    x1 = x_ref[...] + _mm(g_ref[...], wog_ref[...]) + _mm(a_ref[...], woa_ref[...])
    x1_ref[...] = x1
    h2 = _rms(x1, lnw_ref[...])
    h2_ref[...] = h2.astype(BF16)

    logits = _mm3(h2, wr_ref[...]) + br_ref[...]
    lane = lax.broadcasted_iota(jnp.int32, logits.shape, 1)
    big = jnp.int32(LANES)
    is_grp = (lane >= N_EXPERTS) & (lane < N_EXPERTS + N_GROUPS)
    gl = jnp.where(is_grp, logits, -jnp.inf)
    gmax = jnp.max(gl, axis=-1, keepdims=True)
    grp_p = 1.0 / jnp.sum(jnp.exp(gl - gmax), axis=-1, keepdims=True)
    grp_idx = jnp.min(jnp.where(gl == gmax, lane - N_EXPERTS, big), axis=-1, keepdims=True)

    in_grp = (lane < N_EXPERTS) & ((lane // EXPERTS_PER_GROUP) == grp_idx)
    el = jnp.where(in_grp, logits, -jnp.inf)
    m1 = jnp.max(el, axis=-1, keepdims=True)
    i1 = jnp.min(jnp.where(el == m1, lane, big), axis=-1, keepdims=True)
    el2 = jnp.where(lane == i1, -jnp.inf, el)
    m2 = jnp.max(el2, axis=-1, keepdims=True)
    i2 = jnp.min(jnp.where(el2 == m2, lane, big), axis=-1, keepdims=True)
    e2 = jnp.exp(m2 - m1)
    w1 = 1.0 / (1.0 + e2)
    w2 = e2 * w1
    comb_ref[...] = jnp.where(lane == 0, i1.astype(F32), jnp.where(lane == 1, i2.astype(F32),
                              jnp.where(lane == 2, grp_p * w1, jnp.where(lane == 3, grp_p * w2, 0.0))))


def _outproj(x2d, gdn_out, mla_out, p, tm):
    t = x2d.shape[0]
    row = lambda n: pl.BlockSpec((tm, n), lambda i: (i, 0))
    consts = [p['w_out_gdn'], p['w_out_mla'], p['ln_ffn_w'], p['w_router'], p['b_router']]
    return pl.pallas_call(
        _outproj_kernel, grid=(t // tm,),
        in_specs=[row(D_MODEL), row(GDN_V_DIM), row(MLA_HEADS * V_HEAD)] + [_const_spec(c.shape) for c in consts],
        out_specs=(row(D_MODEL), row(D_MODEL), row(LANES)),
        out_shape=(jax.ShapeDtypeStruct((t, D_MODEL), F32), jax.ShapeDtypeStruct((t, D_MODEL), BF16),
                   jax.ShapeDtypeStruct((t, LANES), F32)),
        compiler_params=_params("parallel"), name="outproj",
    )(x2d, gdn_out, mla_out, *consts)


def _moe_slots(tm):
    n = 2 * tm + N_EXPERTS * MOE_SEG_ALIGN + MOE_ROW_BLOCK
    return -(-n // LANES) * LANES


def _moe_kernel(h_ref, route_ref, x1_ref, wgu_ref, wd_ref, y_ref, xs_sc, ys_sc, dcol_sc, offs_ref,
                *, tm, slots, n_chunks):
    g = pl.program_id(1)
    chunk = slots // n_chunks
    lb = min(2 * LANES, tm)
    nlb = tm // lb

    @pl.when(g == 0)
    def _sort_picks():
        rt = route_ref[...].T
        eid = lax.broadcasted_iota(jnp.int32, (N_EXPERTS, lb), 0).astype(F32)
        pieces = [jnp.where(rt[k:k + 1, b * lb:(b + 1) * lb] == eid, 1.0, 0.0)
                  for k in range(2) for b in range(nlb)]
        ind = jnp.concatenate(pieces, axis=0)
        r_i = lax.broadcasted_iota(jnp.int32, (lb, lb), 0)
        c_i = lax.broadcasted_iota(jnp.int32, (lb, lb), 1)
        before = _mm(ind, jnp.where(r_i < c_i, 1.0, 0.0))
        tot = jnp.sum(ind, axis=1, keepdims=True)
        run = jnp.zeros((N_EXPERTS, 1), F32)
        runs = []
        for i in range(len(pieces)):
            runs.append(run)
            run = run + tot[i * N_EXPERTS:(i + 1) * N_EXPERTS]
        seg = jnp.floor((run + (MOE_SEG_ALIGN - 1)) * (1.0 / MOE_SEG_ALIGN)) * MOE_SEG_ALIGN
        er = lax.broadcasted_iota(jnp.int32, (N_EXPERTS, N_EXPERTS), 0)
        ec = lax.broadcasted_iota(jnp.int32, (N_EXPERTS, N_EXPERTS), 1)
        start = _mm_exact_lhs(jnp.where(er > ec, 1.0, 0.0).astype(BF16),
                              jnp.broadcast_to(seg, (N_EXPERTS, LANES)))[:, 0:1]
        start_i = start.astype(jnp.int32)
        for e in range(N_EXPERTS):
            offs_ref[e] = start_i[e, 0]
        offs_ref[N_EXPERTS] = start_i[N_EXPERTS - 1, 0] + seg.astype(jnp.int32)[N_EXPERTS - 1, 0]

        dest = []
        for k in range(2):
            cols = []
            for b in range(nlb):
                i = k * nlb + b
                piece = ind[i * N_EXPERTS:(i + 1) * N_EXPERTS]
                pos = start + runs[i] + before[i * N_EXPERTS:(i + 1) * N_EXPERTS]
                cols.append(jnp.sum(piece * pos, axis=0, keepdims=True))
            dest.append(cols[0] if nlb == 1 else jnp.concatenate(cols, axis=1))
        rec = jnp.concatenate([dest[0], dest[1], rt[2:4], jnp.zeros((LANES - 4, tm), F32)], axis=0)
        dcol_sc[...] = rec.T

        h = h_ref[...]
        for c in range(n_chunks):
            s_i = (c * chunk + lax.broadcasted_iota(jnp.int32, (chunk, 1), 0)).astype(F32)
            sel = jnp.where(s_i == dest[0], 1.0, jnp.where(s_i == dest[1], 1.0, 0.0))
            xs_sc[c * chunk:(c + 1) * chunk, :] = _mm(sel, h).astype(BF16)
        ys_sc[...] = jnp.zeros(ys_sc.shape, BF16)

    for q in range(MOE_EXPERTS_PER_STEP):
        e = g * MOE_EXPERTS_PER_STEP + q
        off = offs_ref[e]
        n_blocks = (offs_ref[e + 1] - off + (MOE_ROW_BLOCK - 1)) // MOE_ROW_BLOCK

        def block(b, carry, q=q, off=off):
            r0 = pl.multiple_of(off + b * MOE_ROW_BLOCK, MOE_SEG_ALIGN)
            hid = _mm(xs_sc[pl.ds(r0, MOE_ROW_BLOCK), :], wgu_ref[q])
            act = _silu(hid[:, :D_EXPERT]) * hid[:, D_EXPERT:]
            ys_sc[pl.ds(r0, MOE_ROW_BLOCK), :] = _mm(act, wd_ref[q]).astype(BF16)
            return carry

        lax.fori_loop(0, n_blocks, block, 0)

    @pl.when(g == pl.num_programs(1) - 1)
    def _combine():
        rec = dcol_sc[...]
        d0, d1, c0, c1 = rec[:, 0:1], rec[:, 1:2], rec[:, 2:3], rec[:, 3:4]
        y = x1_ref[...]
        for c in range(n_chunks):
            s_i = (c * chunk + lax.broadcasted_iota(jnp.int32, (1, chunk), 1)).astype(F32)
            wsel = jnp.where(s_i == d0, c0, jnp.where(s_i == d1, c1, 0.0))
            y = y + _mm(wsel, ys_sc[c * chunk:(c + 1) * chunk, :])
        y_ref[...] = y


def _moe(h2, route, x1, p, tm):
    t = h2.shape[0]
    slots = _moe_slots(tm)
    n_chunks = 3 if slots % (3 * LANES) == 0 else 1
    kern = functools.partial(_moe_kernel, tm=tm, slots=slots, n_chunks=n_chunks)
    once = pl.Buffered(1)
    row = lambda n: pl.BlockSpec((tm, n), lambda i, g: (i, 0), pipeline_mode=once)
    eps = MOE_EXPERTS_PER_STEP
    return pl.pallas_call(
        kern, grid=(t // tm, N_EXPERTS // eps),
        in_specs=[row(D_MODEL), row(LANES), row(D_MODEL),
                  pl.BlockSpec((eps, D_MODEL, 2 * D_EXPERT), lambda i, g: (g, 0, 0)),
                  pl.BlockSpec((eps, D_EXPERT, D_MODEL), lambda i, g: (g, 0, 0))],
        out_specs=pl.BlockSpec((tm, D_MODEL), lambda i, g: (i, 0)),
        out_shape=jax.ShapeDtypeStruct((t, D_MODEL), F32),
        scratch_shapes=[pltpu.VMEM((slots, D_MODEL), BF16), pltpu.VMEM((slots, D_MODEL), BF16),
                        pltpu.VMEM((tm, LANES), F32), pltpu.SMEM((N_EXPERTS + 1,), jnp.int32)],
        compiler_params=_params("parallel", "arbitrary"), name="moe",
    )(h2, route, x1, p['w_exp_gate_up'], p['w_exp_down'])


def _rope_tables(pos):
    inv = ROPE_THETA ** (-jnp.arange(ROPE_HALF, dtype=F32) / ROPE_HALF)
    ang = pos.astype(F32)[:, None] * inv[None, :]
    cos, sin = jnp.cos(ang), jnp.sin(ang)
    return jnp.concatenate([cos, cos], axis=-1), jnp.concatenate([-sin, sin], axis=-1)


def _tile_rows(tables, tm):
    n = tables[0].shape[0]
    if n >= tm:
        assert n % tm == 0
        return tables
    assert tm % n == 0
    return tuple(jnp.tile(t, (tm // n, 1)) for t in tables)


def _pad_lanes(v, offset):
    return jnp.zeros((1, LANES), F32).at[0, offset:offset + v.shape[0]].set(v.astype(F32))


def _prepare_params(p):
    w_in = p['w_in']
    sizes = (GDN_CONV_DIM, GDN_V_DIM, GDN_HEADS, GDN_HEADS, Q_LORA, KV_LORA, QK_ROPE)
    offs = np.concatenate([[0], np.cumsum(sizes)])
    part = [w_in[:, int(offs[i]):int(offs[i + 1])] for i in range(len(sizes))]
    w_ba = jnp.zeros((D_MODEL, LANES), F32).at[:, :2 * GDN_HEADS].set(jnp.concatenate([part[2], part[3]], axis=1))

    uq = p['mla_w_uq'].reshape(Q_LORA, MLA_HEADS, QK_HEAD)
    w_uq = jnp.concatenate([uq[:, :, :QK_NOPE].reshape(Q_LORA, -1), uq[:, :, QK_NOPE:].reshape(Q_LORA, -1)], axis=1)
    ukv = p['mla_w_ukv'].reshape(KV_LORA, MLA_HEADS, QK_NOPE + V_HEAD)
    w_ukv = jnp.concatenate([ukv[:, :, :QK_NOPE].reshape(KV_LORA, -1), ukv[:, :, QK_NOPE:].reshape(KV_LORA, -1)], axis=1)

    w_router = jnp.zeros((D_MODEL, LANES), F32)
    w_router = w_router.at[:, :N_EXPERTS].set(p['w_expert_router'])
    w_router = w_router.at[:, N_EXPERTS:N_EXPERTS + N_GROUPS].set(p['w_group_router'])
    b_router = _pad_lanes(jnp.concatenate([p['b_expert_router'], p['b_group_router']]), 0)

    row = lambda v: v.astype(F32)[None, :]
    return {
        'ln_mix_w': row(p['ln_mix_w']),
        'w_qkv': part[0].astype(BF16), 'w_z': part[1].astype(BF16), 'w_ba': w_ba,
        'a_log': _pad_lanes(p['gdn_a_log'], GDN_HEADS), 'dt_bias': _pad_lanes(p['gdn_dt_bias'], GDN_HEADS),
        'w_cq': part[4].astype(BF16), 'w_ckv': part[5].astype(BF16), 'w_kpe': part[6].astype(BF16),
        'q_a_norm_w': row(p['mla_q_a_norm_w']), 'kv_a_norm_w': row(p['mla_kv_a_norm_w']),
        'w_uq': w_uq.astype(BF16), 'w_ukv': w_ukv.astype(BF16),
        'q_norm_w': row(p['mla_q_norm_w']), 'k_norm_w': row(p['mla_k_norm_w']),
        'conv_w': p['gdn_conv_w'].astype(F32), 'gdn_norm_w': row(p['gdn_norm_w']),
        'w_out_gdn': p['w_out'][:GDN_V_DIM].astype(BF16), 'w_out_mla': p['w_out'][GDN_V_DIM:].astype(BF16),
        'ln_ffn_w': row(p['ln_ffn_w']), 'w_router': w_router, 'b_router': b_router,
        'w_exp_gate_up': jnp.concatenate([p['w_exp_gate'].astype(BF16), p['w_exp_up'].astype(BF16)], axis=-1),
        'w_exp_down': p['w_exp_down'].astype(BF16),
    }


def _tiles(bsz, seq, past):
    t = bsz * seq
    tm = min(512, t)
    n_keys = past + seq
    if past == 0:
        tq = tk = min(512, seq)
        lk = n_keys
    else:
        assert n_keys % CHUNK == 0
        tq = seq
        lk = -(-n_keys // LANES) * LANES
        tk = lk
    tb = min(256, seq)
    tmoe = min(1024, t)
    assert seq % tq == 0 and lk % tk == 0 and seq % tb == 0 and tb % CHUNK == 0 and t % tm == 0 and t % tmoe == 0
    return tm, tq, tk, lk, tb, tmoe


def _hybrid_layer(x, conv_hist, s0, ckv_past, kpe_past, p):
    bsz, seq, _ = x.shape
    past = ckv_past.shape[1]
    t = bsz * seq
    tm, tq, tk, lk, tb, tmoe = _tiles(bsz, seq, past)
    x2d = x.reshape(t, D_MODEL)

    rope_q = _tile_rows(_rope_tables(past + jnp.arange(seq, dtype=jnp.int32)), tm)
    qkv, z, gates, q, ckv, kpe = _inproj(x2d, p, rope_q, tm)

    n_keys = past + seq
    ckv3, kpe3 = ckv.reshape(bsz, seq, KV_LORA), kpe.reshape(bsz, seq, QK_ROPE)
    if lk == seq:
        ckv_all, kpe_all = ckv, kpe
    else:
        zpad = lambda n: jnp.zeros((bsz, lk - n_keys, n), F32)
        ckv_all = jnp.concatenate([ckv_past.astype(F32), ckv3, zpad(KV_LORA)], axis=1).reshape(bsz * lk, KV_LORA)
        kpe_all = jnp.concatenate([kpe_past.astype(F32), kpe3, zpad(QK_ROPE)], axis=1).reshape(bsz * lk, QK_ROPE)
    rope_k = _tile_rows(_rope_tables(jnp.arange(lk, dtype=jnp.int32)), tk)
    k, vt = _kvprep(ckv_all, kpe_all, p, rope_k, tk)
    mla_out = _flash(q, k, vt, bsz, seq, lk, past, tq, tk, FLASH_HEADS_PER_STEP)

    hist_pad = jnp.concatenate([jnp.zeros((bsz, 8 - (CONV_W - 1), GDN_CONV_DIM), F32), conv_hist.astype(F32)], axis=1)
    gdn_out, s_new = _gdn(qkv, z, gates, hist_pad, s0.astype(F32), p, bsz, seq, tb)
    assert seq >= CONV_W - 1
    conv_new = qkv.reshape(bsz, seq, GDN_CONV_DIM)[:, seq - (CONV_W - 1):]

    x1, h2, comb = _outproj(x2d, gdn_out, mla_out, p, tm)
    y = _moe(h2, comb, x1, p, tmoe)
    return y.reshape(bsz, seq, D_MODEL), ckv3, kpe3, s_new.astype(s0.dtype), conv_new


def kernel(x_prompt, x_sample, cache_mla_ckv, cache_mla_kpe, state_gdn, state_gdn_conv, ln_mix_w, w_in, gdn_conv_w, gdn_a_log, gdn_dt_bias, gdn_norm_w, mla_q_a_norm_w, mla_w_uq, mla_kv_a_norm_w, mla_w_ukv, mla_q_norm_w, mla_k_norm_w, w_out, ln_ffn_w, w_group_router, b_group_router, w_expert_router, b_expert_router, w_exp_gate, w_exp_up, w_exp_down):
    depth = w_in.shape[0]
    bsz = x_prompt.shape[0]
    y_p, y_s = x_prompt, x_sample
    outs_p, outs_s = [], []
    for l in range(depth):
        p = _prepare_params({
            'ln_mix_w': ln_mix_w[l], 'w_in': w_in[l], 'gdn_conv_w': gdn_conv_w[l],
            'gdn_a_log': gdn_a_log[l], 'gdn_dt_bias': gdn_dt_bias[l], 'gdn_norm_w': gdn_norm_w[l],
            'mla_q_a_norm_w': mla_q_a_norm_w[l], 'mla_w_uq': mla_w_uq[l],
            'mla_kv_a_norm_w': mla_kv_a_norm_w[l], 'mla_w_ukv': mla_w_ukv[l],
            'mla_q_norm_w': mla_q_norm_w[l], 'mla_k_norm_w': mla_k_norm_w[l],
            'w_out': w_out[l], 'ln_ffn_w': ln_ffn_w[l],
            'w_group_router': w_group_router[l], 'b_group_router': b_group_router[l],
            'w_expert_router': w_expert_router[l], 'b_expert_router': b_expert_router[l],
            'w_exp_gate': w_exp_gate[l], 'w_exp_up': w_exp_up[l], 'w_exp_down': w_exp_down[l],
        })
        y_p, *st_p = _hybrid_layer(
            y_p,
            jnp.zeros((bsz, CONV_W - 1, GDN_CONV_DIM), y_p.dtype),
            jnp.zeros((bsz, GDN_HEADS, GDN_DK, GDN_DV), state_gdn.dtype),
            jnp.zeros((bsz, 0, KV_LORA), cache_mla_ckv.dtype),
            jnp.zeros((bsz, 0, QK_ROPE), cache_mla_kpe.dtype),
            p)
        y_s, *st_s = _hybrid_layer(y_s, state_gdn_conv[l], state_gdn[l], cache_mla_ckv[l], cache_mla_kpe[l], p)
        outs_p.append(st_p)
        outs_s.append(st_s)
    stack = lambda outs, i: jnp.stack([o[i] for o in outs])
    return (y_p, y_s,
            stack(outs_p, 0), stack(outs_p, 1), stack(outs_p, 2), stack(outs_p, 3),
            stack(outs_s, 0), stack(outs_s, 1), stack(outs_s, 2), stack(outs_s, 3))
```

```python
import functools
import math

import jax
import jax.numpy as jnp
import numpy as np
from jax import lax
from jax.experimental import pallas as pl
from jax.experimental.pallas import tpu as pltpu

F32 = jnp.float32
BF16 = jnp.bfloat16

D_MODEL = 1024
CHUNK = 64
RMS_EPS = 1e-6

GDN_HEADS = 4
GDN_DK = 128
GDN_DV = 128
GDN_QK_DIM = GDN_HEADS * GDN_DK
GDN_V_DIM = GDN_HEADS * GDN_DV
GDN_CONV_DIM = 2 * GDN_QK_DIM + GDN_V_DIM
CONV_W = 4

MLA_HEADS = 4
Q_LORA = 512
KV_LORA = 256
QK_NOPE = 128
QK_ROPE = 64
ROPE_HALF = QK_ROPE // 2
V_HEAD = 128
V_ROWS = V_HEAD + 16
QK_HEAD = QK_NOPE + QK_ROPE
QK_PAD = QK_NOPE + 2 * QK_ROPE
ROPE_THETA = 10000.0

N_GROUPS = 4
EXPERTS_PER_GROUP = 8
N_EXPERTS = N_GROUPS * EXPERTS_PER_GROUP
D_EXPERT = 256

LANES = 128
VMEM_LIMIT_BYTES = 56 * 1024 * 1024

Q_SCALE = (QK_HEAD ** -0.5) * math.log2(math.e)
MASK_VALUE = -1e30
FLASH_HEADS_PER_STEP = 2
GDN_SOLVE_PASSES = 1
TOKEN_SUBTILES = 2
MOE_SEG_ALIGN = 16
MOE_ROW_BLOCK = 128
MOE_EXPERTS_PER_STEP = 4


def _mm(a, b):
    return jnp.dot(a.astype(BF16), b.astype(BF16), preferred_element_type=F32)


def _mm_nt(a, b):
    return lax.dot_general(a.astype(BF16), b.astype(BF16), (((1,), (1,)), ((), ())),
                           preferred_element_type=F32)


def _mm_tn(a, b):
    return lax.dot_general(a.astype(BF16), b.astype(BF16), (((0,), (0,)), ((), ())),
                           preferred_element_type=F32)


def _split(a):
    hi = a.astype(BF16)
    lo = (a - hi.astype(F32)).astype(BF16)
    return hi, lo


def _mm3(a, b):
    ah, al = _split(a)
    bh, bl = _split(b)
    return _mm(ah, bh) + (_mm(ah, bl) + _mm(al, bh))


def _mm_solve(a, b):
    return _mm3(a, b) if GDN_SOLVE_PASSES == 3 else _mm(a, b)


def _mm_exact_rhs(a, b01):
    a1 = a.astype(BF16)
    r1 = a - a1.astype(F32)
    a2 = r1.astype(BF16)
    a3 = (r1 - a2.astype(F32)).astype(BF16)
    return _mm(a1, b01) + (_mm(a2, b01) + _mm(a3, b01))


def _mm2_exact_rhs(a, b01):
    hi, lo = _split(a)
    return _mm(hi, b01) + _mm(lo, b01)


def _mm_exact_lhs(a01, b):
    b1 = b.astype(BF16)
    r1 = b - b1.astype(F32)
    b2 = r1.astype(BF16)
    b3 = (r1 - b2.astype(F32)).astype(BF16)
    return _mm(a01, b1) + (_mm(a01, b2) + _mm(a01, b3))


def _rms(x, w):
    return x * lax.rsqrt(jnp.mean(x * x, axis=-1, keepdims=True) + RMS_EPS) * w


def _silu(x):
    return x * jax.nn.sigmoid(x)


def _swap_halves(x):
    h = x.shape[-1] // 2
    return jnp.concatenate([x[:, h:], x[:, :h]], axis=-1)


def _subtiles(tm):
    n = tm // TOKEN_SUBTILES
    return [pl.ds(i * n, n) for i in range(TOKEN_SUBTILES)]


def _const_spec(shape):
    nd = len(shape)
    return pl.BlockSpec(shape, lambda *_: (0,) * nd)


def _params(*sem):
    return pltpu.CompilerParams(dimension_semantics=sem, vmem_limit_bytes=VMEM_LIMIT_BYTES)


def _inproj_kernel(x_ref, lnw_ref, wqkv_ref, wz_ref, wba_ref, alog_ref, dtb_ref, wcq_ref, qanw_ref,
                   wuq_ref, qnw_ref, hsum_ref, swap_ref, cos_ref, sin_ref, wckv_ref, kvanw_ref, wkpe_ref,
                   qkv_ref, z_ref, gates_ref, q_ref, ckv_ref, kpe_ref):
    subs = _subtiles(x_ref.shape[0])
    h_ = [_rms(x_ref[r, :], lnw_ref[...]) for r in subs]
    hb_ = [h.astype(BF16) for h in h_]
    raw_, ckv_, cq_ = [], [], []
    for r, h, hb in zip(subs, h_, hb_):
        qkv_ref[r, :] = _mm(hb, wqkv_ref[...])
        z_ref[r, :] = _mm(hb, wz_ref[...])
        kpe_ref[r, :] = _mm(hb, wkpe_ref[...])
        raw_.append(_mm3(h, wba_ref[...]))
        ckv_.append(_mm(hb, wckv_ref[...]))
        cq_.append(_mm(hb, wcq_ref[...]))

    qall_ = [_mm(_rms(cq, qanw_ref[...]), wuq_ref[...]) for cq in cq_]

    for r, raw, ckv in zip(subs, raw_, ckv_):
        lane = lax.broadcasted_iota(jnp.int32, raw.shape, 1)
        pre = raw + dtb_ref[...]
        softplus = jnp.maximum(pre, 0.0) + jnp.log1p(jnp.exp(-jnp.abs(pre)))
        g = -jnp.exp(alog_ref[...]) * softplus
        gates_ref[r, :] = jnp.where(lane < GDN_HEADS, jax.nn.sigmoid(raw), g)
        ckv_ref[r, :] = _rms(ckv, kvanw_ref[...])

    rope0 = MLA_HEADS * QK_NOPE
    lane = lax.broadcasted_iota(jnp.int32, (1, 2 * QK_ROPE), 1)
    for r, qall in zip(subs, qall_):
        ssq = _mm(qall * qall, hsum_ref[...])
        qn = qall * (lax.rsqrt(ssq * (1.0 / QK_HEAD) + RMS_EPS) * Q_SCALE) * qnw_ref[...]
        rw = qn[:, rope0:]
        cos2 = jnp.concatenate([cos_ref[r, :]] * 2, axis=-1)
        sin2 = jnp.concatenate([sin_ref[r, :]] * 2, axis=-1)
        rot = rw * cos2 + _mm2_exact_rhs(rw, swap_ref[...]) * sin2
        for hd in range(MLA_HEADS):
            pair = rot[:, (hd // 2) * 2 * QK_ROPE:(hd // 2 + 1) * 2 * QK_ROPE]
            mine = jnp.where((lane // QK_ROPE) == (hd % 2), pair, 0.0)
            q_ref[hd, r, :] = jnp.concatenate([qn[:, hd * QK_NOPE:(hd + 1) * QK_NOPE], mine], axis=-1).astype(BF16)


def _inproj(x2d, p, rope_q, tm):
    t = x2d.shape[0]
    nt = t // tm
    cos_t, sin_t = rope_q
    n_rope_blocks = cos_t.shape[0] // tm
    rope_spec = pl.BlockSpec((tm, 2 * QK_ROPE), lambda i: (i % n_rope_blocks, 0))
    row = lambda n: pl.BlockSpec((tm, n), lambda i: (i, 0))
    consts = [p['ln_mix_w'], p['w_qkv'], p['w_z'], p['w_ba'], p['a_log'], p['dt_bias'], p['w_cq'],
              p['q_a_norm_w'], p['w_uq'], p['q_norm_w'], p['q_head_sum'], p['rope_swap']]
    consts2 = [p['w_ckv'], p['kv_a_norm_w'], p['w_kpe']]
    in_specs = ([row(D_MODEL)] + [_const_spec(c.shape) for c in consts] + [rope_spec, rope_spec]
                + [_const_spec(c.shape) for c in consts2])
    out_shape = (
        jax.ShapeDtypeStruct((t, GDN_CONV_DIM), F32),
        jax.ShapeDtypeStruct((t, GDN_V_DIM), F32),
        jax.ShapeDtypeStruct((t, LANES), F32),
        jax.ShapeDtypeStruct((MLA_HEADS, t, QK_PAD), BF16),
        jax.ShapeDtypeStruct((t, KV_LORA), F32),
        jax.ShapeDtypeStruct((t, QK_ROPE), F32),
    )
    out_specs = (row(GDN_CONV_DIM), row(GDN_V_DIM), row(LANES),
                 pl.BlockSpec((MLA_HEADS, tm, QK_PAD), lambda i: (0, i, 0)),
                 row(KV_LORA), row(QK_ROPE))
    return pl.pallas_call(
        _inproj_kernel, grid=(nt,), in_specs=in_specs, out_specs=out_specs, out_shape=out_shape,
        compiler_params=_params("parallel"), name="inproj",
    )(x2d, *consts, cos_t, sin_t, *consts2)


def _kvprep_kernel(ckv_ref, kpe_ref, cos_ref, sin_ref, wukv_ref, knw_ref, k_ref, vt_ref):
    kv = _mm(ckv_ref[...], wukv_ref[...])
    kpe = kpe_ref[...]
    knw = knw_ref[...]
    kw = kpe * knw[:, QK_NOPE:]
    krot = kw * cos_ref[:, :QK_ROPE] + _swap_halves(kw) * sin_ref[:, :QK_ROPE]
    krot = jnp.concatenate([krot, krot], axis=-1)
    s_kpe = jnp.sum(kpe * kpe, axis=-1, keepdims=True)
    v0 = MLA_HEADS * QK_NOPE
    pad_row = lax.broadcasted_iota(jnp.int32, (V_ROWS - V_HEAD, kv.shape[0]), 0)
    ones_rows = jnp.where(pad_row == 0, 1.0, 0.0)
    for hd in range(MLA_HEADS):
        kn = kv[:, hd * QK_NOPE:(hd + 1) * QK_NOPE]
        inv = lax.rsqrt((jnp.sum(kn * kn, axis=-1, keepdims=True) + s_kpe) * (1.0 / QK_HEAD) + RMS_EPS)
        kh = jnp.concatenate([kn * knw[:, :QK_NOPE], krot], axis=-1) * inv
        k_ref[hd] = kh.astype(BF16)
        vt = kv[:, v0 + hd * V_HEAD: v0 + (hd + 1) * V_HEAD].T
        vt_ref[hd, 0] = jnp.concatenate([vt, ones_rows], axis=0).astype(BF16)


def _kvprep(ckv2d, kpe2d, p, rope_k, tm):
    t = ckv2d.shape[0]
    cos_t, sin_t = rope_k
    n_rope_blocks = cos_t.shape[0] // tm
    rope_spec = pl.BlockSpec((tm, 2 * QK_ROPE), lambda i: (i % n_rope_blocks, 0))
    row = lambda n: pl.BlockSpec((tm, n), lambda i: (i, 0))
    return pl.pallas_call(
        _kvprep_kernel, grid=(t // tm,),
        in_specs=[row(KV_LORA), row(QK_ROPE), rope_spec, rope_spec,
                  _const_spec(p['w_ukv'].shape), _const_spec(p['k_norm_w'].shape)],
        out_specs=(pl.BlockSpec((MLA_HEADS, tm, QK_PAD), lambda i: (0, i, 0)),
                   pl.BlockSpec((MLA_HEADS, 1, V_ROWS, tm), lambda i: (0, i, 0, 0))),
        out_shape=(jax.ShapeDtypeStruct((MLA_HEADS, t, QK_PAD), BF16),
                   jax.ShapeDtypeStruct((MLA_HEADS, t // tm, V_ROWS, tm), BF16)),
        compiler_params=_params("parallel"), name="kvprep",
    )(ckv2d, kpe2d, cos_t, sin_t, p['w_ukv'], p['k_norm_w'])


def _flash_kernel(q_ref, k_ref, vt_ref, o_ref, m_sc, acc_sc, *, tq, tk, nk, past, hps, qsplit):
    i = pl.program_id(2)
    q_first = past + i * tq
    q_last = q_first + tq - 1
    n_full = jnp.minimum(nk, ((q_first // CHUNK + 1) * CHUNK) // tk)
    n_vis = jnp.minimum(nk, ((q_last // CHUNK + 1) * CHUNK + tk - 1) // tk)

    m_sc[...] = jnp.full(m_sc.shape, MASK_VALUE, F32)
    acc_sc[...] = jnp.zeros(acc_sc.shape, F32)

    tqs = tq // qsplit
    streams = [(hd, pl.ds(c * tqs, tqs)) for hd in range(hps) for c in range(qsplit)]

    def step(j, masked):
        k0 = pl.multiple_of(j * tk, tk)
        s_ = [_mm_nt(k_ref[hd, pl.ds(k0, tk), :], q_ref[hd, cols, :]) for hd, cols in streams]
        if masked:
            kpos = k0 + lax.broadcasted_iota(jnp.int32, (tk, 1), 0)
            for n, (hd, cols) in enumerate(streams):
                qpos = q_first + (n % qsplit) * tqs + lax.broadcasted_iota(jnp.int32, (1, tqs), 1)
                s_[n] = jnp.where((kpos // CHUNK) <= (qpos // CHUNK), s_[n], MASK_VALUE)
        p_, alpha_ = [], []
        for n, (hd, cols) in enumerate(streams):
            m_old = m_sc[hd, :, cols]
            m_new = jnp.maximum(m_old, jnp.max(s_[n], axis=0, keepdims=True))
            alpha_.append(jnp.exp2(m_old - m_new))
            p_.append(jnp.exp2(s_[n] - m_new).astype(BF16))
            m_sc[hd, :, cols] = m_new
        for n, (hd, cols) in enumerate(streams):
            acc_sc[hd, :, cols] = alpha_[n] * acc_sc[hd, :, cols] + _mm(vt_ref[hd, j], p_[n])

    def full_body(j, c):
        step(j, False)
        return c

    def masked_body(j, c):
        step(j, True)
        return c

    lax.fori_loop(0, n_full, full_body, 0)
    lax.fori_loop(n_full, n_vis, masked_body, 0)
    for hd in range(hps):
        o_t = acc_sc[hd, :V_HEAD, :] / acc_sc[hd, V_HEAD:V_HEAD + 1, :]
        o_ref[:, hd * V_HEAD:(hd + 1) * V_HEAD] = o_t.T.astype(o_ref.dtype)


def _flash(q, k, vt, bsz, lq, lk, past, tq, tk, hps, qsplit):
    nq = lq // tq
    nk = lk // tk
    kern = functools.partial(_flash_kernel, tq=tq, tk=tk, nk=nk, past=past, hps=hps, qsplit=qsplit)
    once = pl.Buffered(1)
    return pl.pallas_call(
        kern, grid=(bsz, MLA_HEADS // hps, nq),
        in_specs=[pl.BlockSpec((hps, tq, QK_PAD), lambda b, h, i: (h, b * nq + i, 0)),
                  pl.BlockSpec((hps, lk, QK_PAD), lambda b, h, i: (h, b, 0), pipeline_mode=once),
                  pl.BlockSpec((hps, nk, V_ROWS, tk), lambda b, h, i: (h, b, 0, 0), pipeline_mode=once)],
        out_specs=pl.BlockSpec((tq, hps * V_HEAD), lambda b, h, i: (b * nq + i, h)),
        out_shape=jax.ShapeDtypeStruct((bsz * lq, MLA_HEADS * V_HEAD), BF16),
        scratch_shapes=[pltpu.VMEM((hps, 1, tq), F32), pltpu.VMEM((hps, V_ROWS, tq), F32)],
        compiler_params=_params("parallel", "parallel", "arbitrary"), name="flash",
    )(q, k, vt)


def _gdn_kernel(qkv_ref, z_ref, gates_ref, hist_ref, s0_ref, convw_ref, normw_ref,
                o_ref, s_ref, xbuf, *, tb):
    j = pl.program_id(1)
    nc = tb // CHUNK
    pad = 8

    @pl.when(j == 0)
    def _():
        xbuf[0:pad, :] = hist_ref[0]
        s_ref[...] = s0_ref[...]

    x = qkv_ref[...]
    xbuf[pad:pad + tb, :] = x
    cw = convw_ref[...]
    y = xbuf[pad - 3:pad - 3 + tb, :] * cw[0:1, :]
    for i in range(1, CONV_W):
        y = y + xbuf[pad - 3 + i:pad - 3 + i + tb, :] * cw[i:i + 1, :]
    xbuf[0:pad, :] = xbuf[tb:tb + pad, :]
    y = _silu(y)

    gates = gates_ref[...]
    gates_t = gates.T[0:8, :]
    row = lax.broadcasted_iota(jnp.int32, (tb, tb), 0)
    col = lax.broadcasted_iota(jnp.int32, (tb, tb), 1)
    same = (row // CHUNK) == (col // CHUNK)
    incl = same & (row >= col)
    strict = same & (row > col)
    tril01 = jnp.where(incl, 1.0, 0.0).astype(BF16)
    triu01 = jnp.where(same & (row <= col), 1.0, 0.0).astype(BF16)
    gc_cols = _mm_exact_lhs(tril01, gates)
    gc_rows = _mm_exact_rhs(gates_t, triu01)
    eye = jnp.where(row == col, 1.0, 0.0)

    heads = range(GDN_HEADS)
    q_, k_, v_, beta_, gc_, decay_, kb_ = [], [], [], [], [], [], []
    for hd in heads:
        qh = y[:, hd * GDN_DK:(hd + 1) * GDN_DK]
        kh = y[:, GDN_QK_DIM + hd * GDN_DK: GDN_QK_DIM + (hd + 1) * GDN_DK]
        q_.append(qh * (lax.rsqrt(jnp.sum(qh * qh, axis=-1, keepdims=True) + RMS_EPS) * (GDN_DK ** -0.5)))
        k_.append(kh * lax.rsqrt(jnp.sum(kh * kh, axis=-1, keepdims=True) + RMS_EPS))
        v_.append(y[:, 2 * GDN_QK_DIM + hd * GDN_DV: 2 * GDN_QK_DIM + (hd + 1) * GDN_DV])
        beta_.append(gates[:, hd:hd + 1])
        gc = gc_cols[:, GDN_HEADS + hd: GDN_HEADS + hd + 1]
        gr = gc_rows[GDN_HEADS + hd: GDN_HEADS + hd + 1, :]
        gc_.append(gc)
        decay_.append(jnp.where(incl, jnp.exp(jnp.where(incl, gc - gr, 0.0)), 0.0))
        kb_.append(k_[hd] * beta_[hd])
    khb_ = [k.astype(BF16) for k in k_]
    pw_ = [jnp.where(strict, -(_mm_nt(kb_[hd], khb_[hd]) * decay_[hd]), 0.0) for hd in heads]
    attn_ = [_mm_nt(q_[hd], khb_[hd]) * decay_[hd] for hd in heads]

    tinv_ = [eye + pw for pw in pw_]
    for _ in range(int(math.log2(CHUNK)) - 1):
        pw_ = [_mm_solve(pw, pw) for pw in pw_]
        tinv_ = [tinv_[hd] + _mm_solve(tinv_[hd], pw_[hd]) for hd in heads]

    eg_ = [jnp.exp(gc) for gc in gc_]
    sol_ = [_mm_solve(tinv_[hd], jnp.concatenate([v_[hd] * beta_[hd], kb_[hd] * eg_[hd]], axis=-1)) for hd in heads]
    qdec_ = [q_[hd] * eg_[hd] for hd in heads]

    s_ = [s_ref[0, hd] for hd in heads]
    us_ = [[] for _ in heads]
    oi_ = [[] for _ in heads]
    for c in range(nc):
        r0, r1 = c * CHUNK, (c + 1) * CHUNK
        for hd in heads:
            g_end = gc_[hd][r1 - 1:r1, :]
            k_end = k_[hd][r0:r1] * jnp.exp(g_end - gc_[hd][r0:r1])
            both = _mm(jnp.concatenate([sol_[hd][r0:r1, GDN_DV:], qdec_[hd][r0:r1]], axis=0), s_[hd])
            u = sol_[hd][r0:r1, :GDN_DV] - both[:CHUNK]
            oi_[hd].append(both[CHUNK:])
            us_[hd].append(u)
            s_[hd] = s_[hd] * jnp.exp(g_end) + _mm_tn(k_end, u)
    for hd in heads:
        s_ref[0, hd] = s_[hd]
        u_all = us_[hd][0] if nc == 1 else jnp.concatenate(us_[hd], axis=0)
        o_all = oi_[hd][0] if nc == 1 else jnp.concatenate(oi_[hd], axis=0)
        o = o_all + _mm(attn_[hd], u_all)
        o = _rms(o, normw_ref[...]) * _silu(z_ref[:, hd * GDN_DV:(hd + 1) * GDN_DV])
        o_ref[:, hd * GDN_DV:(hd + 1) * GDN_DV] = o.astype(o_ref.dtype)


def _gdn(qkv2d, z2d, gates2d, hist_pad, s0, p, bsz, seq, tb):
    nb = seq // tb
    kern = functools.partial(_gdn_kernel, tb=tb)
    row = lambda n: pl.BlockSpec((tb, n), lambda b, j: (b * nb + j, 0))
    state_spec = pl.BlockSpec((1, GDN_HEADS, GDN_DK, GDN_DV), lambda b, j: (b, 0, 0, 0))
    return pl.pallas_call(
        kern, grid=(bsz, nb),
        in_specs=[row(GDN_CONV_DIM), row(GDN_V_DIM), row(LANES),
                  pl.BlockSpec((1, 8, GDN_CONV_DIM), lambda b, j: (b, 0, 0)),
                  state_spec, _const_spec(p['conv_w'].shape), _const_spec(p['gdn_norm_w'].shape)],
        out_specs=(row(GDN_V_DIM), state_spec),
        out_shape=(jax.ShapeDtypeStruct((bsz * seq, GDN_V_DIM), BF16),
                   jax.ShapeDtypeStruct((bsz, GDN_HEADS, GDN_DK, GDN_DV), F32)),
        scratch_shapes=[pltpu.VMEM((tb + 8, GDN_CONV_DIM), F32)],
        compiler_params=_params("parallel", "arbitrary"), name="gdn",
    )(qkv2d, z2d, gates2d, hist_pad, s0, p['conv_w'], p['gdn_norm_w'])


def _outproj_kernel(x_ref, g_ref, a_ref, wog_ref, woa_ref, lnw_ref, wr_ref, br_ref,
                    x1_ref, h2_ref, comb_ref):
    subs = _subtiles(x_ref.shape[0])
    h2_ = []
    for r in subs:
        x1 = x_ref[r, :] + _mm(g_ref[r, :], wog_ref[...]) + _mm(a_ref[r, :], woa_ref[...])
        x1_ref[r, :] = x1
        h2_.append(_rms(x1, lnw_ref[...]))
        h2_ref[r, :] = h2_[-1].astype(BF16)
    logits_ = [_mm3(h2, wr_ref[...]) + br_ref[...] for h2 in h2_]
    for r, logits in zip(subs, logits_):
        comb_ref[r, :] = _route(logits)


def _route(logits):
    lane = lax.broadcasted_iota(jnp.int32, logits.shape, 1)
    big = jnp.int32(LANES)
    is_grp = (lane >= N_EXPERTS) & (lane < N_EXPERTS + N_GROUPS)
    gl = jnp.where(is_grp, logits, -jnp.inf)
    gmax = jnp.max(gl, axis=-1, keepdims=True)
    grp_p = 1.0 / jnp.sum(jnp.exp(gl - gmax), axis=-1, keepdims=True)
    grp_idx = jnp.min(jnp.where(gl == gmax, lane - N_EXPERTS, big), axis=-1, keepdims=True)

    in_grp = (lane < N_EXPERTS) & ((lane // EXPERTS_PER_GROUP) == grp_idx)
    el = jnp.where(in_grp, logits, -jnp.inf)
    m1 = jnp.max(el, axis=-1, keepdims=True)
    i1 = jnp.min(jnp.where(el == m1, lane, big), axis=-1, keepdims=True)
    el2 = jnp.where(lane == i1, -jnp.inf, el)
    m2 = jnp.max(el2, axis=-1, keepdims=True)
    i2 = jnp.min(jnp.where(el2 == m2, lane, big), axis=-1, keepdims=True)
    e2 = jnp.exp(m2 - m1)
    w1 = 1.0 / (1.0 + e2)
    w2 = e2 * w1
    return jnp.where(lane == 0, i1.astype(F32), jnp.where(lane == 1, i2.astype(F32),
                     jnp.where(lane == 2, grp_p * w1, jnp.where(lane == 3, grp_p * w2, 0.0))))


def _outproj(x2d, gdn_out, mla_out, p, tm):
    t = x2d.shape[0]
    row = lambda n: pl.BlockSpec((tm, n), lambda i: (i, 0))
    consts = [p['w_out_gdn'], p['w_out_mla'], p['ln_ffn_w'], p['w_router'], p['b_router']]
    return pl.pallas_call(
        _outproj_kernel, grid=(t // tm,),
        in_specs=[row(D_MODEL), row(GDN_V_DIM), row(MLA_HEADS * V_HEAD)] + [_const_spec(c.shape) for c in consts],
        out_specs=(row(D_MODEL), row(D_MODEL), row(LANES)),
        out_shape=(jax.ShapeDtypeStruct((t, D_MODEL), F32), jax.ShapeDtypeStruct((t, D_MODEL), BF16),
                   jax.ShapeDtypeStruct((t, LANES), F32)),
        compiler_params=_params("parallel"), name="outproj",
    )(x2d, gdn_out, mla_out, *consts)


def _moe_slots(tm):
    n = 2 * tm + N_EXPERTS * MOE_SEG_ALIGN + MOE_ROW_BLOCK
    return -(-n // LANES) * LANES


def _moe_kernel(h_ref, route_ref, x1_ref, wgu_ref, wd_ref, y_ref, xs_sc, ys_sc, dcol_sc, offs_ref,
                *, tm, slots, n_chunks):
    g = pl.program_id(1)
    chunk = slots // n_chunks
    lb = min(2 * LANES, tm)
    nlb = tm // lb

    @pl.when(g == 0)
    def _sort_picks():
        rt = route_ref[...].T
        eid = lax.broadcasted_iota(jnp.int32, (N_EXPERTS, lb), 0).astype(F32)
        pieces = [jnp.where(rt[k:k + 1, b * lb:(b + 1) * lb] == eid, 1.0, 0.0)
                  for k in range(2) for b in range(nlb)]
        ind = jnp.concatenate(pieces, axis=0)
        r_i = lax.broadcasted_iota(jnp.int32, (lb, lb), 0)
        c_i = lax.broadcasted_iota(jnp.int32, (lb, lb), 1)
        before = _mm(ind, jnp.where(r_i < c_i, 1.0, 0.0))
        tot = jnp.sum(ind, axis=1, keepdims=True)
        run = jnp.zeros((N_EXPERTS, 1), F32)
        runs = []
        for i in range(len(pieces)):
            runs.append(run)
            run = run + tot[i * N_EXPERTS:(i + 1) * N_EXPERTS]
        seg = jnp.floor((run + (MOE_SEG_ALIGN - 1)) * (1.0 / MOE_SEG_ALIGN)) * MOE_SEG_ALIGN
        er = lax.broadcasted_iota(jnp.int32, (N_EXPERTS, N_EXPERTS), 0)
        ec = lax.broadcasted_iota(jnp.int32, (N_EXPERTS, N_EXPERTS), 1)
        start = _mm_exact_lhs(jnp.where(er > ec, 1.0, 0.0).astype(BF16),
                              jnp.broadcast_to(seg, (N_EXPERTS, LANES)))[:, 0:1]
        start_i = start.astype(jnp.int32)
        for e in range(N_EXPERTS):
            offs_ref[e] = start_i[e, 0]
        offs_ref[N_EXPERTS] = start_i[N_EXPERTS - 1, 0] + seg.astype(jnp.int32)[N_EXPERTS - 1, 0]

        dest = []
        for k in range(2):
            cols = []
            for b in range(nlb):
                i = k * nlb + b
                piece = ind[i * N_EXPERTS:(i + 1) * N_EXPERTS]
                pos = start + runs[i] + before[i * N_EXPERTS:(i + 1) * N_EXPERTS]
                cols.append(jnp.sum(piece * pos, axis=0, keepdims=True))
            dest.append(cols[0] if nlb == 1 else jnp.concatenate(cols, axis=1))
        rec = jnp.concatenate([dest[0], dest[1], rt[2:4], jnp.zeros((LANES - 4, tm), F32)], axis=0)
        dcol_sc[...] = rec.T

        h = h_ref[...]
        for c in range(n_chunks):
            s_i = (c * chunk + lax.broadcasted_iota(jnp.int32, (chunk, 1), 0)).astype(F32)
            sel = jnp.where(s_i == dest[0], 1.0, jnp.where(s_i == dest[1], 1.0, 0.0))
            xs_sc[c * chunk:(c + 1) * chunk, :] = _mm(sel, h).astype(BF16)
        ys_sc[...] = jnp.zeros(ys_sc.shape, BF16)

    steps = range(MOE_EXPERTS_PER_STEP)
    offs = [offs_ref[g * MOE_EXPERTS_PER_STEP + q] for q in steps] + [offs_ref[(g + 1) * MOE_EXPERTS_PER_STEP]]

    def rows(q, b):
        return pl.ds(pl.multiple_of(offs[q] + b * MOE_ROW_BLOCK, MOE_SEG_ALIGN), MOE_ROW_BLOCK)

    def gated(hid):
        return _silu(hid[:, :D_EXPERT]) * hid[:, D_EXPERT:]

    for q in steps:
        n_blocks = (offs[q + 1] - offs[q] + (MOE_ROW_BLOCK - 1)) // MOE_ROW_BLOCK

        def block(b, carry, q=q):
            act = gated(_mm(xs_sc[rows(q, b), :], wgu_ref[q]))
            ys_sc[rows(q, b), :] = _mm(act, wd_ref[q]).astype(BF16)
            return carry

        lax.fori_loop(1, n_blocks, block, 0)

    hid_ = [_mm(xs_sc[rows(q, 0), :], wgu_ref[q]) for q in steps]
    act_ = [gated(h) for h in hid_]
    out_ = [_mm(act_[q], wd_ref[q]).astype(BF16) for q in steps]
    for q in steps:
        ys_sc[rows(q, 0), :] = out_[q]

    @pl.when(g == pl.num_programs(1) - 1)
    def _combine():
        rec = dcol_sc[...]
        d0, d1, c0, c1 = rec[:, 0:1], rec[:, 1:2], rec[:, 2:3], rec[:, 3:4]
        y = x1_ref[...]
        for c in range(n_chunks):
            s_i = (c * chunk + lax.broadcasted_iota(jnp.int32, (1, chunk), 1)).astype(F32)
            wsel = jnp.where(s_i == d0, c0, jnp.where(s_i == d1, c1, 0.0))
            y = y + _mm(wsel, ys_sc[c * chunk:(c + 1) * chunk, :])
        y_ref[...] = y


def _moe(h2, route, x1, p, tm):
    t = h2.shape[0]
    slots = _moe_slots(tm)
    n_chunks = 3 if slots % (3 * LANES) == 0 else 1
    kern = functools.partial(_moe_kernel, tm=tm, slots=slots, n_chunks=n_chunks)
    once = pl.Buffered(1)
    row = lambda n: pl.BlockSpec((tm, n), lambda i, g: (i, 0), pipeline_mode=once)
    eps = MOE_EXPERTS_PER_STEP
    return pl.pallas_call(
        kern, grid=(t // tm, N_EXPERTS // eps),
        in_specs=[row(D_MODEL), row(LANES), row(D_MODEL),
                  pl.BlockSpec((eps, D_MODEL, 2 * D_EXPERT), lambda i, g: (g, 0, 0)),
                  pl.BlockSpec((eps, D_EXPERT, D_MODEL), lambda i, g: (g, 0, 0))],
        out_specs=pl.BlockSpec((tm, D_MODEL), lambda i, g: (i, 0)),
        out_shape=jax.ShapeDtypeStruct((t, D_MODEL), F32),
        scratch_shapes=[pltpu.VMEM((slots, D_MODEL), BF16), pltpu.VMEM((slots, D_MODEL), BF16),
                        pltpu.VMEM((tm, LANES), F32), pltpu.SMEM((N_EXPERTS + 1,), jnp.int32)],
        compiler_params=_params("parallel", "arbitrary"), name="moe",
    )(h2, route, x1, p['w_exp_gate_up'], p['w_exp_down'])


def _rope_tables(pos):
    inv = ROPE_THETA ** (-jnp.arange(ROPE_HALF, dtype=F32) / ROPE_HALF)
    ang = pos.astype(F32)[:, None] * inv[None, :]
    cos, sin = jnp.cos(ang), jnp.sin(ang)
    return jnp.concatenate([cos, cos] * 2, axis=-1), jnp.concatenate([-sin, sin] * 2, axis=-1)


def _tile_rows(tables, tm):
    n = tables[0].shape[0]
    if n >= tm:
        assert n % tm == 0
        return tables
    assert tm % n == 0
    return tuple(jnp.tile(t, (tm // n, 1)) for t in tables)


def _pad_lanes(v, offset):
    return jnp.zeros((1, LANES), F32).at[0, offset:offset + v.shape[0]].set(v.astype(F32))


def _q_layout_constants():
    n_nope = MLA_HEADS * QK_NOPE
    col = np.arange(n_nope + MLA_HEADS * QK_ROPE)
    head = np.where(col < n_nope, col // QK_NOPE, (col - n_nope) // QK_ROPE)
    same_head = (head[:, None] == head[None, :]).astype(np.float32)
    j = np.arange(MLA_HEADS * QK_ROPE)
    src = (j // QK_ROPE) * QK_ROPE + (j % QK_ROPE + ROPE_HALF) % QK_ROPE
    swap = (j[:, None] == src[None, :]).astype(np.float32)
    return same_head, swap


_Q_HEAD_SUM, _ROPE_SWAP = _q_layout_constants()


def _prepare_params(p):
    w_in = p['w_in']
    sizes = (GDN_CONV_DIM, GDN_V_DIM, GDN_HEADS, GDN_HEADS, Q_LORA, KV_LORA, QK_ROPE)
    offs = np.concatenate([[0], np.cumsum(sizes)])
    part = [w_in[:, int(offs[i]):int(offs[i + 1])] for i in range(len(sizes))]
    w_ba = jnp.zeros((D_MODEL, LANES), F32).at[:, :2 * GDN_HEADS].set(jnp.concatenate([part[2], part[3]], axis=1))

    uq = p['mla_w_uq'].reshape(Q_LORA, MLA_HEADS, QK_HEAD)
    w_uq = jnp.concatenate([uq[:, :, :QK_NOPE].reshape(Q_LORA, -1), uq[:, :, QK_NOPE:].reshape(Q_LORA, -1)], axis=1)
    ukv = p['mla_w_ukv'].reshape(KV_LORA, MLA_HEADS, QK_NOPE + V_HEAD)
    w_ukv = jnp.concatenate([ukv[:, :, :QK_NOPE].reshape(KV_LORA, -1), ukv[:, :, QK_NOPE:].reshape(KV_LORA, -1)], axis=1)

    w_router = jnp.zeros((D_MODEL, LANES), F32)
    w_router = w_router.at[:, :N_EXPERTS].set(p['w_expert_router'])
    w_router = w_router.at[:, N_EXPERTS:N_EXPERTS + N_GROUPS].set(p['w_group_router'])
    b_router = _pad_lanes(jnp.concatenate([p['b_expert_router'], p['b_group_router']]), 0)

    row = lambda v: v.astype(F32)[None, :]
    return {
        'ln_mix_w': row(p['ln_mix_w']),
        'w_qkv': part[0].astype(BF16), 'w_z': part[1].astype(BF16), 'w_ba': w_ba,
        'a_log': _pad_lanes(p['gdn_a_log'], GDN_HEADS), 'dt_bias': _pad_lanes(p['gdn_dt_bias'], GDN_HEADS),
        'w_cq': part[4].astype(BF16), 'w_ckv': part[5].astype(BF16), 'w_kpe': part[6].astype(BF16),
        'q_a_norm_w': row(p['mla_q_a_norm_w']), 'kv_a_norm_w': row(p['mla_kv_a_norm_w']),
        'w_uq': w_uq.astype(BF16), 'w_ukv': w_ukv.astype(BF16),
        'q_norm_w': row(jnp.concatenate([jnp.tile(p['mla_q_norm_w'][:QK_NOPE], MLA_HEADS),
                                         jnp.tile(p['mla_q_norm_w'][QK_NOPE:], MLA_HEADS)])),
        'q_head_sum': jnp.asarray(_Q_HEAD_SUM, BF16), 'rope_swap': jnp.asarray(_ROPE_SWAP, BF16),
        'k_norm_w': row(p['mla_k_norm_w']),
        'conv_w': p['gdn_conv_w'].astype(F32), 'gdn_norm_w': row(p['gdn_norm_w']),
        'w_out_gdn': p['w_out'][:GDN_V_DIM].astype(BF16), 'w_out_mla': p['w_out'][GDN_V_DIM:].astype(BF16),
        'ln_ffn_w': row(p['ln_ffn_w']), 'w_router': w_router, 'b_router': b_router,
        'w_exp_gate_up': jnp.concatenate([p['w_exp_gate'].astype(BF16), p['w_exp_up'].astype(BF16)], axis=-1),
        'w_exp_down': p['w_exp_down'].astype(BF16),
    }


def _tiles(bsz, seq, past):
    t = bsz * seq
    tm = min(1024, t)
    n_keys = past + seq
    if past == 0:
        tk = min(512, seq)
        tq = min(2 * tk, seq)
        lk = n_keys
    else:
        assert n_keys % CHUNK == 0
        tq = seq
        lk = -(-n_keys // LANES) * LANES
        tk = lk
    tb = min(256, seq)
    tmoe = min(1024, t)
    assert seq % tq == 0 and lk % tk == 0 and seq % tb == 0 and tb % CHUNK == 0 and t % tm == 0 and t % tmoe == 0
    return tm, tq, tk, lk, tb, tmoe


def _hybrid_layer(x, conv_hist, s0, ckv_past, kpe_past, p):
    bsz, seq, _ = x.shape
    past = ckv_past.shape[1]
    t = bsz * seq
    tm, tq, tk, lk, tb, tmoe = _tiles(bsz, seq, past)
    x2d = x.reshape(t, D_MODEL)

    rope_q = _tile_rows(_rope_tables(past + jnp.arange(seq, dtype=jnp.int32)), tm)
    qkv, z, gates, q, ckv, kpe = _inproj(x2d, p, rope_q, tm)

    n_keys = past + seq
    ckv3, kpe3 = ckv.reshape(bsz, seq, KV_LORA), kpe.reshape(bsz, seq, QK_ROPE)
    if lk == seq:
        ckv_all, kpe_all = ckv, kpe
    else:
        zpad = lambda n: jnp.zeros((bsz, lk - n_keys, n), F32)
        ckv_all = jnp.concatenate([ckv_past.astype(F32), ckv3, zpad(KV_LORA)], axis=1).reshape(bsz * lk, KV_LORA)
        kpe_all = jnp.concatenate([kpe_past.astype(F32), kpe3, zpad(QK_ROPE)], axis=1).reshape(bsz * lk, QK_ROPE)
    rope_k = _tile_rows(_rope_tables(jnp.arange(lk, dtype=jnp.int32)), tk)
    k, vt = _kvprep(ckv_all, kpe_all, p, rope_k, tk)
    mla_out = _flash(q, k, vt, bsz, seq, lk, past, tq, tk, FLASH_HEADS_PER_STEP, tq // min(tq, tk))

    hist_pad = jnp.concatenate([jnp.zeros((bsz, 8 - (CONV_W - 1), GDN_CONV_DIM), F32), conv_hist.astype(F32)], axis=1)
    gdn_out, s_new = _gdn(qkv, z, gates, hist_pad, s0.astype(F32), p, bsz, seq, tb)
    assert seq >= CONV_W - 1
    conv_new = qkv.reshape(bsz, seq, GDN_CONV_DIM)[:, seq - (CONV_W - 1):]

    x1, h2, comb = _outproj(x2d, gdn_out, mla_out, p, tm)
    y = _moe(h2, comb, x1, p, tmoe)
    return y.reshape(bsz, seq, D_MODEL), ckv3, kpe3, s_new.astype(s0.dtype), conv_new


def kernel(x_prompt, x_sample, cache_mla_ckv, cache_mla_kpe, state_gdn, state_gdn_conv, ln_mix_w, w_in, gdn_conv_w, gdn_a_log, gdn_dt_bias, gdn_norm_w, mla_q_a_norm_w, mla_w_uq, mla_kv_a_norm_w, mla_w_ukv, mla_q_norm_w, mla_k_norm_w, w_out, ln_ffn_w, w_group_router, b_group_router, w_expert_router, b_expert_router, w_exp_gate, w_exp_up, w_exp_down):
    depth = w_in.shape[0]
    bsz = x_prompt.shape[0]
    y_p, y_s = x_prompt, x_sample
    outs_p, outs_s = [], []
    for l in range(depth):
        p = _prepare_params({
            'ln_mix_w': ln_mix_w[l], 'w_in': w_in[l], 'gdn_conv_w': gdn_conv_w[l],
            'gdn_a_log': gdn_a_log[l], 'gdn_dt_bias': gdn_dt_bias[l], 'gdn_norm_w': gdn_norm_w[l],
            'mla_q_a_norm_w': mla_q_a_norm_w[l], 'mla_w_uq': mla_w_uq[l],
            'mla_kv_a_norm_w': mla_kv_a_norm_w[l], 'mla_w_ukv': mla_w_ukv[l],
            'mla_q_norm_w': mla_q_norm_w[l], 'mla_k_norm_w': mla_k_norm_w[l],
            'w_out': w_out[l], 'ln_ffn_w': ln_ffn_w[l],
            'w_group_router': w_group_router[l], 'b_group_router': b_group_router[l],
            'w_expert_router': w_expert_router[l], 'b_expert_router': b_expert_router[l],
            'w_exp_gate': w_exp_gate[l], 'w_exp_up': w_exp_up[l], 'w_exp_down': w_exp_down[l],
        })
        y_p, *st_p = _hybrid_layer(
            y_p,
            jnp.zeros((bsz, CONV_W - 1, GDN_CONV_DIM), y_p.dtype),
            jnp.zeros((bsz, GDN_HEADS, GDN_DK, GDN_DV), state_gdn.dtype),
            jnp.zeros((bsz, 0, KV_LORA), cache_mla_ckv.dtype),
            jnp.zeros((bsz, 0, QK_ROPE), cache_mla_kpe.dtype),
            p)
        y_s, *st_s = _hybrid_layer(y_s, state_gdn_conv[l], state_gdn[l], cache_mla_ckv[l], cache_mla_kpe[l], p)
        outs_p.append(st_p)
        outs_s.append(st_s)
    stack = lambda outs, i: jnp.stack([o[i] for o in outs])
    return (y_p, y_s,
            stack(outs_p, 0), stack(outs_p, 1), stack(outs_p, 2), stack(outs_p, 3),
            stack(outs_s, 0), stack(outs_s, 1), stack(outs_s, 2), stack(outs_s, 3))
```

```python
import functools
import math

import jax
import jax.numpy as jnp
import numpy as np
from jax import lax
from jax.experimental import pallas as pl
from jax.experimental.pallas import tpu as pltpu

F32 = jnp.float32
BF16 = jnp.bfloat16

D_MODEL = 1024
CHUNK = 64
RMS_EPS = 1e-6

GDN_HEADS = 4
GDN_DK = 128
GDN_DV = 128
GDN_QK_DIM = GDN_HEADS * GDN_DK
GDN_V_DIM = GDN_HEADS * GDN_DV
GDN_CONV_DIM = 2 * GDN_QK_DIM + GDN_V_DIM
CONV_W = 4

MLA_HEADS = 4
Q_LORA = 512
KV_LORA = 256
QK_NOPE = 128
QK_ROPE = 64
ROPE_HALF = QK_ROPE // 2
V_HEAD = 128
V_ROWS = V_HEAD + 16
QK_HEAD = QK_NOPE + QK_ROPE
QK_PAD = QK_NOPE + 2 * QK_ROPE
ROPE_THETA = 10000.0

N_GROUPS = 4
EXPERTS_PER_GROUP = 8
N_EXPERTS = N_GROUPS * EXPERTS_PER_GROUP
D_EXPERT = 256

LANES = 128
VMEM_LIMIT_BYTES = 58 * 1024 * 1024

Q_SCALE = (QK_HEAD ** -0.5) * math.log2(math.e)
MASK_VALUE = -1e30
FLASH_HEADS_PER_STEP = 2
GDN_SOLVE_PASSES = 1
TOKEN_SUBTILES = 2
MOE_SEG_ALIGN = 16
MOE_ROW_BLOCK = 128
MOE_EXPERTS_PER_STEP = 4


def _mm(a, b):
    return jnp.dot(a.astype(BF16), b.astype(BF16), preferred_element_type=F32)


def _mm_nt(a, b):
    return lax.dot_general(a.astype(BF16), b.astype(BF16), (((1,), (1,)), ((), ())),
                           preferred_element_type=F32)


def _mm_tn(a, b):
    return lax.dot_general(a.astype(BF16), b.astype(BF16), (((0,), (0,)), ((), ())),
                           preferred_element_type=F32)


def _split(a):
    hi = a.astype(BF16)
    lo = (a - hi.astype(F32)).astype(BF16)
    return hi, lo


def _mm3(a, b):
    ah, al = _split(a)
    bh, bl = _split(b)
    return _mm(ah, bh) + (_mm(ah, bl) + _mm(al, bh))


def _mm_solve(a, b):
    return _mm3(a, b) if GDN_SOLVE_PASSES == 3 else _mm(a, b)


def _mm_exact_rhs(a, b01):
    a1 = a.astype(BF16)
    r1 = a - a1.astype(F32)
    a2 = r1.astype(BF16)
    a3 = (r1 - a2.astype(F32)).astype(BF16)
    return _mm(a1, b01) + (_mm(a2, b01) + _mm(a3, b01))


def _mm2_exact_rhs(a, b01):
    hi, lo = _split(a)
    return _mm(hi, b01) + _mm(lo, b01)


def _mm_exact_lhs(a01, b):
    b1 = b.astype(BF16)
    r1 = b - b1.astype(F32)
    b2 = r1.astype(BF16)
    b3 = (r1 - b2.astype(F32)).astype(BF16)
    return _mm(a01, b1) + (_mm(a01, b2) + _mm(a01, b3))


def _rms(x, w):
    return x * lax.rsqrt(jnp.mean(x * x, axis=-1, keepdims=True) + RMS_EPS) * w


def _silu(x):
    return x * jax.nn.sigmoid(x)


def _swap_halves(x):
    h = x.shape[-1] // 2
    return jnp.concatenate([x[:, h:], x[:, :h]], axis=-1)


def _subtiles(tm):
    n = tm // TOKEN_SUBTILES
    return [pl.ds(i * n, n) for i in range(TOKEN_SUBTILES)]


def _const_spec(shape):
    nd = len(shape)
    return pl.BlockSpec(shape, lambda *_: (0,) * nd, pipeline_mode=pl.Buffered(1))


def _params(*sem):
    return pltpu.CompilerParams(dimension_semantics=sem, vmem_limit_bytes=VMEM_LIMIT_BYTES)


def _inproj_kernel(x_ref, lnw_ref, wqkv_ref, wz_ref, wba_ref, alog_ref, dtb_ref, wcq_ref, qanw_ref,
                   wuq_ref, qnw_ref, hsum_ref, swap_ref, cos_ref, sin_ref, wckv_ref, kvanw_ref, wkpe_ref,
                   wukv_ref, knw_ref, qkv_ref, z_ref, gates_ref, q_ref, ckv_ref, kpe_ref, *kv_refs):
    subs = _subtiles(x_ref.shape[0])
    h_ = [_rms(x_ref[r, :], lnw_ref[...]) for r in subs]
    hb_ = [h.astype(BF16) for h in h_]
    raw_, ckv_, cq_, kpe_ = [], [], [], []
    for r, h, hb in zip(subs, h_, hb_):
        qkv_ref[r, :] = _mm(hb, wqkv_ref[...])
        z_ref[r, :] = _mm(hb, wz_ref[...])
        kpe_.append(_mm(hb, wkpe_ref[...]))
        kpe_ref[r, :] = kpe_[-1]
        raw_.append(_mm3(h, wba_ref[...]))
        ckv_.append(_mm(hb, wckv_ref[...]))
        cq_.append(_mm(hb, wcq_ref[...]))

    qall_ = [_mm(_rms(cq, qanw_ref[...]), wuq_ref[...]) for cq in cq_]

    for i, (r, raw, ckv) in enumerate(zip(subs, raw_, ckv_)):
        lane = lax.broadcasted_iota(jnp.int32, raw.shape, 1)
        pre = raw + dtb_ref[...]
        softplus = jnp.maximum(pre, 0.0) + jnp.log1p(jnp.exp(-jnp.abs(pre)))
        g = -jnp.exp(alog_ref[...]) * softplus
        gates_ref[r, :] = jnp.where(lane < GDN_HEADS, jax.nn.sigmoid(raw), g)
        ckv_n = _rms(ckv, kvanw_ref[...])
        ckv_ref[r, :] = ckv_n
        if kv_refs:
            k_ref, vt_ref = kv_refs
            heads = _keys_values(ckv_n, kpe_[i], cos_ref[r, :], sin_ref[r, :], wukv_ref[...], knw_ref[...])
            for hd, (kh, vt) in enumerate(heads):
                k_ref[hd, r, :] = kh
                vt_ref[hd, i] = vt

    rope0 = MLA_HEADS * QK_NOPE
    lane = lax.broadcasted_iota(jnp.int32, (1, 2 * QK_ROPE), 1)
    for r, qall in zip(subs, qall_):
        ssq = _mm(qall * qall, hsum_ref[...])
        qn = qall * (lax.rsqrt(ssq * (1.0 / QK_HEAD) + RMS_EPS) * Q_SCALE) * qnw_ref[...]
        rw = qn[:, rope0:]
        cos2 = jnp.concatenate([cos_ref[r, :]] * 2, axis=-1)
        sin2 = jnp.concatenate([sin_ref[r, :]] * 2, axis=-1)
        rot = rw * cos2 + _mm2_exact_rhs(rw, swap_ref[...]) * sin2
        for hd in range(MLA_HEADS):
            pair = rot[:, (hd // 2) * 2 * QK_ROPE:(hd // 2 + 1) * 2 * QK_ROPE]
            mine = jnp.where((lane // QK_ROPE) == (hd % 2), pair, 0.0)
            q_ref[hd, r, :] = jnp.concatenate([qn[:, hd * QK_NOPE:(hd + 1) * QK_NOPE], mine], axis=-1).astype(BF16)


def _inproj(x2d, p, rope_q, tm, key_tile=None):
    t = x2d.shape[0]
    nt = t // tm
    cos_t, sin_t = rope_q
    n_rope_blocks = cos_t.shape[0] // tm
    rope_spec = pl.BlockSpec((tm, 2 * QK_ROPE), lambda i: (i % n_rope_blocks, 0))
    row = lambda n: pl.BlockSpec((tm, n), lambda i: (i, 0))
    consts = [p['ln_mix_w'], p['w_qkv'], p['w_z'], p['w_ba'], p['a_log'], p['dt_bias'], p['w_cq'],
              p['q_a_norm_w'], p['w_uq'], p['q_norm_w'], p['q_head_sum'], p['rope_swap']]
    consts2 = [p['w_ckv'], p['kv_a_norm_w'], p['w_kpe'], p['w_ukv'], p['k_norm_w']]
    in_specs = ([row(D_MODEL)] + [_const_spec(c.shape) for c in consts] + [rope_spec, rope_spec]
                + [_const_spec(c.shape) for c in consts2])
    out_shape = (
        jax.ShapeDtypeStruct((t, GDN_CONV_DIM), F32),
        jax.ShapeDtypeStruct((t, GDN_V_DIM), F32),
        jax.ShapeDtypeStruct((t, LANES), F32),
        jax.ShapeDtypeStruct((MLA_HEADS, t, QK_PAD), BF16),
        jax.ShapeDtypeStruct((t, KV_LORA), F32),
        jax.ShapeDtypeStruct((t, QK_ROPE), F32),
    )
    out_specs = (row(GDN_CONV_DIM), row(GDN_V_DIM), row(LANES),
                 pl.BlockSpec((MLA_HEADS, tm, QK_PAD), lambda i: (0, i, 0)),
                 row(KV_LORA), row(QK_ROPE))
    if key_tile is not None:
        assert tm // TOKEN_SUBTILES == key_tile
        out_shape += (jax.ShapeDtypeStruct((MLA_HEADS, t, QK_PAD), BF16),
                      jax.ShapeDtypeStruct((MLA_HEADS, t // key_tile, V_ROWS, key_tile), BF16))
        out_specs += (pl.BlockSpec((MLA_HEADS, tm, QK_PAD), lambda i: (0, i, 0)),
                      pl.BlockSpec((MLA_HEADS, TOKEN_SUBTILES, V_ROWS, key_tile), lambda i: (0, i, 0, 0)))
    return pl.pallas_call(
        _inproj_kernel, grid=(nt,), in_specs=in_specs, out_specs=out_specs, out_shape=out_shape,
        compiler_params=_params("parallel"), name="inproj",
    )(x2d, *consts, cos_t, sin_t, *consts2)


def _keys_values(ckv, kpe, cosf, sinf, wukv, knw):
    kv = _mm(ckv, wukv)
    kw = kpe * knw[:, QK_NOPE:]
    krot = kw * cosf[:, :QK_ROPE] + _swap_halves(kw) * sinf[:, :QK_ROPE]
    krot = jnp.concatenate([krot, krot], axis=-1)
    s_kpe = jnp.sum(kpe * kpe, axis=-1, keepdims=True)
    v0 = MLA_HEADS * QK_NOPE
    pad_row = lax.broadcasted_iota(jnp.int32, (V_ROWS - V_HEAD, kv.shape[0]), 0)
    ones_rows = jnp.where(pad_row == 0, 1.0, 0.0)
    out = []
    for hd in range(MLA_HEADS):
        kn = kv[:, hd * QK_NOPE:(hd + 1) * QK_NOPE]
        inv = lax.rsqrt((jnp.sum(kn * kn, axis=-1, keepdims=True) + s_kpe) * (1.0 / QK_HEAD) + RMS_EPS)
        kh = jnp.concatenate([kn * knw[:, :QK_NOPE], krot], axis=-1) * inv
        vt = kv[:, v0 + hd * V_HEAD: v0 + (hd + 1) * V_HEAD].T
        out.append((kh.astype(BF16), jnp.concatenate([vt, ones_rows], axis=0).astype(BF16)))
    return out


def _kvprep_kernel(ckv_ref, kpe_ref, cos_ref, sin_ref, wukv_ref, knw_ref, k_ref, vt_ref):
    heads = _keys_values(ckv_ref[...], kpe_ref[...], cos_ref[...], sin_ref[...], wukv_ref[...], knw_ref[...])
    for hd, (kh, vt) in enumerate(heads):
        k_ref[hd] = kh
        vt_ref[hd, 0] = vt


def _kvprep(ckv2d, kpe2d, p, rope_k, tm):
    t = ckv2d.shape[0]
    cos_t, sin_t = rope_k
    n_rope_blocks = cos_t.shape[0] // tm
    rope_spec = pl.BlockSpec((tm, 2 * QK_ROPE), lambda i: (i % n_rope_blocks, 0))
    row = lambda n: pl.BlockSpec((tm, n), lambda i: (i, 0))
    return pl.pallas_call(
        _kvprep_kernel, grid=(t // tm,),
        in_specs=[row(KV_LORA), row(QK_ROPE), rope_spec, rope_spec,
                  _const_spec(p['w_ukv'].shape), _const_spec(p['k_norm_w'].shape)],
        out_specs=(pl.BlockSpec((MLA_HEADS, tm, QK_PAD), lambda i: (0, i, 0)),
                   pl.BlockSpec((MLA_HEADS, 1, V_ROWS, tm), lambda i: (0, i, 0, 0))),
        out_shape=(jax.ShapeDtypeStruct((MLA_HEADS, t, QK_PAD), BF16),
                   jax.ShapeDtypeStruct((MLA_HEADS, t // tm, V_ROWS, tm), BF16)),
        compiler_params=_params("parallel"), name="kvprep",
    )(ckv2d, kpe2d, cos_t, sin_t, p['w_ukv'], p['k_norm_w'])


def _flash_kernel(q_ref, k_ref, vt_ref, o_ref, m_sc, acc_sc, *, tq, tk, nk, past, hps, qsplit):
    i = pl.program_id(2)
    q_first = past + i * tq
    q_last = q_first + tq - 1
    n_full = jnp.minimum(nk, ((q_first // CHUNK + 1) * CHUNK) // tk)
    n_vis = jnp.minimum(nk, ((q_last // CHUNK + 1) * CHUNK + tk - 1) // tk)

    m_sc[...] = jnp.full(m_sc.shape, MASK_VALUE, F32)
    acc_sc[...] = jnp.zeros(acc_sc.shape, F32)

    tqs = tq // qsplit
    streams = [(hd, pl.ds(c * tqs, tqs)) for hd in range(hps) for c in range(qsplit)]

    def step(j, masked):
        k0 = pl.multiple_of(j * tk, tk)
        s_ = [_mm_nt(k_ref[hd, pl.ds(k0, tk), :], q_ref[hd, cols, :]) for hd, cols in streams]
        if masked:
            kpos = k0 + lax.broadcasted_iota(jnp.int32, (tk, 1), 0)
            for n, (hd, cols) in enumerate(streams):
                qpos = q_first + (n % qsplit) * tqs + lax.broadcasted_iota(jnp.int32, (1, tqs), 1)
                s_[n] = jnp.where((kpos // CHUNK) <= (qpos // CHUNK), s_[n], MASK_VALUE)
        p_, alpha_ = [], []
        for n, (hd, cols) in enumerate(streams):
            m_old = m_sc[hd, :, cols]
            m_new = jnp.maximum(m_old, jnp.max(s_[n], axis=0, keepdims=True))
            alpha_.append(jnp.exp2(m_old - m_new))
            p_.append(jnp.exp2(s_[n] - m_new).astype(BF16))
            m_sc[hd, :, cols] = m_new
        for n, (hd, cols) in enumerate(streams):
            acc_sc[hd, :, cols] = alpha_[n] * acc_sc[hd, :, cols] + _mm(vt_ref[hd, j], p_[n])

    def full_body(j, c):
        step(j, False)
        return c

    def masked_body(j, c):
        step(j, True)
        return c

    lax.fori_loop(0, n_full, full_body, 0)
    lax.fori_loop(n_full, n_vis, masked_body, 0)
    for hd in range(hps):
        o_t = acc_sc[hd, :V_HEAD, :] / acc_sc[hd, V_HEAD:V_HEAD + 1, :]
        o_ref[:, hd * V_HEAD:(hd + 1) * V_HEAD] = o_t.T.astype(o_ref.dtype)


def _flash(q, k, vt, bsz, lq, lk, past, tq, tk, hps, qsplit):
    nq = lq // tq
    nk = lk // tk
    kern = functools.partial(_flash_kernel, tq=tq, tk=tk, nk=nk, past=past, hps=hps, qsplit=qsplit)
    once = pl.Buffered(1 if nq > 1 else 2)
    return pl.pallas_call(
        kern, grid=(bsz, MLA_HEADS // hps, nq),
        in_specs=[pl.BlockSpec((hps, tq, QK_PAD), lambda b, h, i: (h, b * nq + i, 0)),
                  pl.BlockSpec((hps, lk, QK_PAD), lambda b, h, i: (h, b, 0), pipeline_mode=once),
                  pl.BlockSpec((hps, nk, V_ROWS, tk), lambda b, h, i: (h, b, 0, 0), pipeline_mode=once)],
        out_specs=pl.BlockSpec((tq, hps * V_HEAD), lambda b, h, i: (b * nq + i, h)),
        out_shape=jax.ShapeDtypeStruct((bsz * lq, MLA_HEADS * V_HEAD), BF16),
        scratch_shapes=[pltpu.VMEM((hps, 1, tq), F32), pltpu.VMEM((hps, V_ROWS, tq), F32)],
        compiler_params=_params("parallel", "parallel", "arbitrary"), name="flash",
    )(q, k, vt)


def _gdn_kernel(qkv_ref, z_ref, gates_ref, hist_ref, s0_ref, convw_ref, normw_ref,
                o_ref, s_ref, xbuf, *, tb):
    j = pl.program_id(1)
    nc = tb // CHUNK
    pad = 8

    @pl.when(j == 0)
    def _():
        xbuf[0:pad, :] = hist_ref[0]
        s_ref[...] = s0_ref[...]

    x = qkv_ref[...]
    xbuf[pad:pad + tb, :] = x
    cw = convw_ref[...]
    y = xbuf[pad - 3:pad - 3 + tb, :] * cw[0:1, :]
    for i in range(1, CONV_W):
        y = y + xbuf[pad - 3 + i:pad - 3 + i + tb, :] * cw[i:i + 1, :]
    xbuf[0:pad, :] = xbuf[tb:tb + pad, :]
    y = _silu(y)

    gates = gates_ref[...]
    gates_t = gates.T[0:8, :]
    row = lax.broadcasted_iota(jnp.int32, (tb, tb), 0)
    col = lax.broadcasted_iota(jnp.int32, (tb, tb), 1)
    same = (row // CHUNK) == (col // CHUNK)
    incl = same & (row >= col)
    strict = same & (row > col)
    tril01 = jnp.where(incl, 1.0, 0.0).astype(BF16)
    triu01 = jnp.where(same & (row <= col), 1.0, 0.0).astype(BF16)
    gc_cols = _mm_exact_lhs(tril01, gates)
    gc_rows = _mm_exact_rhs(gates_t, triu01)
    eye = jnp.where(row == col, 1.0, 0.0)

    heads = range(GDN_HEADS)
    q_, k_, v_, beta_, gc_, decay_, kb_ = [], [], [], [], [], [], []
    for hd in heads:
        qh = y[:, hd * GDN_DK:(hd + 1) * GDN_DK]
        kh = y[:, GDN_QK_DIM + hd * GDN_DK: GDN_QK_DIM + (hd + 1) * GDN_DK]
        q_.append(qh * (lax.rsqrt(jnp.sum(qh * qh, axis=-1, keepdims=True) + RMS_EPS) * (GDN_DK ** -0.5)))
        k_.append(kh * lax.rsqrt(jnp.sum(kh * kh, axis=-1, keepdims=True) + RMS_EPS))
        v_.append(y[:, 2 * GDN_QK_DIM + hd * GDN_DV: 2 * GDN_QK_DIM + (hd + 1) * GDN_DV])
        beta_.append(gates[:, hd:hd + 1])
        gc = gc_cols[:, GDN_HEADS + hd: GDN_HEADS + hd + 1]
        gr = gc_rows[GDN_HEADS + hd: GDN_HEADS + hd + 1, :]
        gc_.append(gc)
        decay_.append(jnp.where(incl, jnp.exp(jnp.where(incl, gc - gr, 0.0)), 0.0))
        kb_.append(k_[hd] * beta_[hd])
    khb_ = [k.astype(BF16) for k in k_]
    pw_ = [jnp.where(strict, -(_mm_nt(kb_[hd], khb_[hd]) * decay_[hd]), 0.0) for hd in heads]
    attn_ = [_mm_nt(q_[hd], khb_[hd]) * decay_[hd] for hd in heads]

    tinv_ = [eye + pw for pw in pw_]
    for _ in range(int(math.log2(CHUNK)) - 1):
        pw_ = [_mm_solve(pw, pw) for pw in pw_]
        tinv_ = [tinv_[hd] + _mm_solve(tinv_[hd], pw_[hd]) for hd in heads]

    eg_ = [jnp.exp(gc) for gc in gc_]
    sol_ = [_mm_solve(tinv_[hd], jnp.concatenate([v_[hd] * beta_[hd], kb_[hd] * eg_[hd]], axis=-1)) for hd in heads]
    qdec_ = [q_[hd] * eg_[hd] for hd in heads]

    s_ = [s_ref[0, hd] for hd in heads]
    us_ = [[] for _ in heads]
    oi_ = [[] for _ in heads]
    for c in range(nc):
        r0, r1 = c * CHUNK, (c + 1) * CHUNK
        for hd in heads:
            g_end = gc_[hd][r1 - 1:r1, :]
            k_end = k_[hd][r0:r1] * jnp.exp(g_end - gc_[hd][r0:r1])
            both = _mm(jnp.concatenate([sol_[hd][r0:r1, GDN_DV:], qdec_[hd][r0:r1]], axis=0), s_[hd])
            u = sol_[hd][r0:r1, :GDN_DV] - both[:CHUNK]
            oi_[hd].append(both[CHUNK:])
            us_[hd].append(u)
            s_[hd] = s_[hd] * jnp.exp(g_end) + _mm_tn(k_end, u)
    for hd in heads:
        s_ref[0, hd] = s_[hd]
        u_all = us_[hd][0] if nc == 1 else jnp.concatenate(us_[hd], axis=0)
        o_all = oi_[hd][0] if nc == 1 else jnp.concatenate(oi_[hd], axis=0)
        o = o_all + _mm(attn_[hd], u_all)
        o = _rms(o, normw_ref[...]) * _silu(z_ref[:, hd * GDN_DV:(hd + 1) * GDN_DV])
        o_ref[:, hd * GDN_DV:(hd + 1) * GDN_DV] = o.astype(o_ref.dtype)


def _gdn(qkv2d, z2d, gates2d, hist_pad, s0, p, bsz, seq, tb):
    nb = seq // tb
    kern = functools.partial(_gdn_kernel, tb=tb)
    row = lambda n: pl.BlockSpec((tb, n), lambda b, j: (b * nb + j, 0))
    state_spec = pl.BlockSpec((1, GDN_HEADS, GDN_DK, GDN_DV), lambda b, j: (b, 0, 0, 0))
    return pl.pallas_call(
        kern, grid=(bsz, nb),
        in_specs=[row(GDN_CONV_DIM), row(GDN_V_DIM), row(LANES),
                  pl.BlockSpec((1, 8, GDN_CONV_DIM), lambda b, j: (b, 0, 0)),
                  state_spec, _const_spec(p['conv_w'].shape), _const_spec(p['gdn_norm_w'].shape)],
        out_specs=(row(GDN_V_DIM), state_spec),
        out_shape=(jax.ShapeDtypeStruct((bsz * seq, GDN_V_DIM), BF16),
                   jax.ShapeDtypeStruct((bsz, GDN_HEADS, GDN_DK, GDN_DV), F32)),
        scratch_shapes=[pltpu.VMEM((tb + 8, GDN_CONV_DIM), F32)],
        compiler_params=_params("parallel", "arbitrary"), name="gdn",
    )(qkv2d, z2d, gates2d, hist_pad, s0, p['conv_w'], p['gdn_norm_w'])


def _outproj_kernel(x_ref, g_ref, a_ref, wog_ref, woa_ref, lnw_ref, wr_ref, br_ref,
                    x1_ref, h2_ref, comb_ref):
    subs = _subtiles(x_ref.shape[0])
    h2_ = []
    for r in subs:
        x1 = x_ref[r, :] + _mm(g_ref[r, :], wog_ref[...]) + _mm(a_ref[r, :], woa_ref[...])
        x1_ref[r, :] = x1
        h2_.append(_rms(x1, lnw_ref[...]))
        h2_ref[r, :] = h2_[-1].astype(BF16)
    logits_ = [_mm3(h2, wr_ref[...]) + br_ref[...] for h2 in h2_]
    for r, logits in zip(subs, logits_):
        comb_ref[r, :] = _route(logits)


def _route(logits):
    lane = lax.broadcasted_iota(jnp.int32, logits.shape, 1)
    big = jnp.int32(LANES)
    is_grp = (lane >= N_EXPERTS) & (lane < N_EXPERTS + N_GROUPS)
    gl = jnp.where(is_grp, logits, -jnp.inf)
    gmax = jnp.max(gl, axis=-1, keepdims=True)
    grp_p = 1.0 / jnp.sum(jnp.exp(gl - gmax), axis=-1, keepdims=True)
    grp_idx = jnp.min(jnp.where(gl == gmax, lane - N_EXPERTS, big), axis=-1, keepdims=True)

    in_grp = (lane < N_EXPERTS) & ((lane // EXPERTS_PER_GROUP) == grp_idx)
    el = jnp.where(in_grp, logits, -jnp.inf)
    m1 = jnp.max(el, axis=-1, keepdims=True)
    i1 = jnp.min(jnp.where(el == m1, lane, big), axis=-1, keepdims=True)
    el2 = jnp.where(lane == i1, -jnp.inf, el)
    m2 = jnp.max(el2, axis=-1, keepdims=True)
    i2 = jnp.min(jnp.where(el2 == m2, lane, big), axis=-1, keepdims=True)
    e2 = jnp.exp(m2 - m1)
    w1 = 1.0 / (1.0 + e2)
    w2 = e2 * w1
    return jnp.where(lane == 0, i1.astype(F32), jnp.where(lane == 1, i2.astype(F32),
                     jnp.where(lane == 2, grp_p * w1, jnp.where(lane == 3, grp_p * w2, 0.0))))


def _outproj(x2d, gdn_out, mla_out, p, tm):
    t = x2d.shape[0]
    row = lambda n: pl.BlockSpec((tm, n), lambda i: (i, 0))
    consts = [p['w_out_gdn'], p['w_out_mla'], p['ln_ffn_w'], p['w_router'], p['b_router']]
    return pl.pallas_call(
        _outproj_kernel, grid=(t // tm,),
        in_specs=[row(D_MODEL), row(GDN_V_DIM), row(MLA_HEADS * V_HEAD)] + [_const_spec(c.shape) for c in consts],
        out_specs=(row(D_MODEL), row(D_MODEL), row(LANES)),
        out_shape=(jax.ShapeDtypeStruct((t, D_MODEL), F32), jax.ShapeDtypeStruct((t, D_MODEL), BF16),
                   jax.ShapeDtypeStruct((t, LANES), F32)),
        compiler_params=_params("parallel"), name="outproj",
    )(x2d, gdn_out, mla_out, *consts)


def _moe_slots(tm):
    n = 2 * tm + N_EXPERTS * MOE_SEG_ALIGN + MOE_ROW_BLOCK
    return -(-n // LANES) * LANES


def _moe_kernel(h_ref, route_ref, x1_ref, wgu_ref, wd_ref, y_ref, xs_sc, ys_sc, dcol_sc, offs_ref,
                *, tm, slots, n_chunks):
    g = pl.program_id(1)
    chunk = slots // n_chunks
    lb = min(2 * LANES, tm)
    nlb = tm // lb

    @pl.when(g == 0)
    def _sort_picks():
        rt = route_ref[...].T
        eid = lax.broadcasted_iota(jnp.int32, (N_EXPERTS, lb), 0).astype(F32)
        pieces = [jnp.where(rt[k:k + 1, b * lb:(b + 1) * lb] == eid, 1.0, 0.0)
                  for k in range(2) for b in range(nlb)]
        ind = jnp.concatenate(pieces, axis=0)
        r_i = lax.broadcasted_iota(jnp.int32, (lb, lb), 0)
        c_i = lax.broadcasted_iota(jnp.int32, (lb, lb), 1)
        before = _mm(ind, jnp.where(r_i < c_i, 1.0, 0.0))
        tot = jnp.sum(ind, axis=1, keepdims=True)
        run = jnp.zeros((N_EXPERTS, 1), F32)
        runs = []
        for i in range(len(pieces)):
            runs.append(run)
            run = run + tot[i * N_EXPERTS:(i + 1) * N_EXPERTS]
        seg = jnp.floor((run + (MOE_SEG_ALIGN - 1)) * (1.0 / MOE_SEG_ALIGN)) * MOE_SEG_ALIGN
        er = lax.broadcasted_iota(jnp.int32, (N_EXPERTS, N_EXPERTS), 0)
        ec = lax.broadcasted_iota(jnp.int32, (N_EXPERTS, N_EXPERTS), 1)
        start = _mm_exact_lhs(jnp.where(er > ec, 1.0, 0.0).astype(BF16),
                              jnp.broadcast_to(seg, (N_EXPERTS, LANES)))[:, 0:1]
        start_i = start.astype(jnp.int32)
        for e in range(N_EXPERTS):
            offs_ref[e] = start_i[e, 0]
        offs_ref[N_EXPERTS] = start_i[N_EXPERTS - 1, 0] + seg.astype(jnp.int32)[N_EXPERTS - 1, 0]

        dest = []
        for k in range(2):
            cols = []
            for b in range(nlb):
                i = k * nlb + b
                piece = ind[i * N_EXPERTS:(i + 1) * N_EXPERTS]
                pos = start + runs[i] + before[i * N_EXPERTS:(i + 1) * N_EXPERTS]
                cols.append(jnp.sum(piece * pos, axis=0, keepdims=True))
            dest.append(cols[0] if nlb == 1 else jnp.concatenate(cols, axis=1))
        rec = jnp.concatenate([dest[0], dest[1], rt[2:4], jnp.zeros((LANES - 4, tm), F32)], axis=0)
        dcol_sc[...] = rec.T

        h = h_ref[...]
        for c in range(n_chunks):
            s_i = (c * chunk + lax.broadcasted_iota(jnp.int32, (chunk, 1), 0)).astype(F32)
            sel = jnp.where(s_i == dest[0], 1.0, jnp.where(s_i == dest[1], 1.0, 0.0))
            xs_sc[c * chunk:(c + 1) * chunk, :] = _mm(sel, h).astype(BF16)
        ys_sc[...] = jnp.zeros(ys_sc.shape, BF16)

    steps = range(MOE_EXPERTS_PER_STEP)
    offs = [offs_ref[g * MOE_EXPERTS_PER_STEP + q] for q in steps] + [offs_ref[(g + 1) * MOE_EXPERTS_PER_STEP]]

    def rows(q, b):
        return pl.ds(pl.multiple_of(offs[q] + b * MOE_ROW_BLOCK, MOE_SEG_ALIGN), MOE_ROW_BLOCK)

    def gated(hid):
        return _silu(hid[:, :D_EXPERT]) * hid[:, D_EXPERT:]

    for q in steps:
        n_blocks = (offs[q + 1] - offs[q] + (MOE_ROW_BLOCK - 1)) // MOE_ROW_BLOCK

        def block(b, carry, q=q):
            act = gated(_mm(xs_sc[rows(q, b), :], wgu_ref[q]))
            ys_sc[rows(q, b), :] = _mm(act, wd_ref[q]).astype(BF16)
            return carry

        lax.fori_loop(1, n_blocks, block, 0)

    hid_ = [_mm(xs_sc[rows(q, 0), :], wgu_ref[q]) for q in steps]
    act_ = [gated(h) for h in hid_]
    out_ = [_mm(act_[q], wd_ref[q]).astype(BF16) for q in steps]
    for q in steps:
        ys_sc[rows(q, 0), :] = out_[q]

    @pl.when(g == pl.num_programs(1) - 1)
    def _combine():
        rec = dcol_sc[...]
        d0, d1, c0, c1 = rec[:, 0:1], rec[:, 1:2], rec[:, 2:3], rec[:, 3:4]
        y = x1_ref[...]
        for c in range(n_chunks):
            s_i = (c * chunk + lax.broadcasted_iota(jnp.int32, (1, chunk), 1)).astype(F32)
            wsel = jnp.where(s_i == d0, c0, jnp.where(s_i == d1, c1, 0.0))
            y = y + _mm(wsel, ys_sc[c * chunk:(c + 1) * chunk, :])
        y_ref[...] = y


def _moe(h2, route, x1, p, tm):
    t = h2.shape[0]
    slots = _moe_slots(tm)
    n_chunks = 3 if slots % (3 * LANES) == 0 else 1
    kern = functools.partial(_moe_kernel, tm=tm, slots=slots, n_chunks=n_chunks)
    once = pl.Buffered(1)
    row = lambda n: pl.BlockSpec((tm, n), lambda i, g: (i, 0), pipeline_mode=once)
    eps = MOE_EXPERTS_PER_STEP
    return pl.pallas_call(
        kern, grid=(t // tm, N_EXPERTS // eps),
        in_specs=[row(D_MODEL), row(LANES), row(D_MODEL),
                  pl.BlockSpec((eps, D_MODEL, 2 * D_EXPERT), lambda i, g: (g, 0, 0)),
                  pl.BlockSpec((eps, D_EXPERT, D_MODEL), lambda i, g: (g, 0, 0))],
        out_specs=pl.BlockSpec((tm, D_MODEL), lambda i, g: (i, 0)),
        out_shape=jax.ShapeDtypeStruct((t, D_MODEL), F32),
        scratch_shapes=[pltpu.VMEM((slots, D_MODEL), BF16), pltpu.VMEM((slots, D_MODEL), BF16),
                        pltpu.VMEM((tm, LANES), F32), pltpu.SMEM((N_EXPERTS + 1,), jnp.int32)],
        compiler_params=_params("parallel", "arbitrary"), name="moe",
    )(h2, route, x1, p['w_exp_gate_up'], p['w_exp_down'])


def _rope_tables(pos):
    inv = ROPE_THETA ** (-jnp.arange(ROPE_HALF, dtype=F32) / ROPE_HALF)
    ang = pos.astype(F32)[:, None] * inv[None, :]
    cos, sin = jnp.cos(ang), jnp.sin(ang)
    return jnp.concatenate([cos, cos] * 2, axis=-1), jnp.concatenate([-sin, sin] * 2, axis=-1)


def _tile_rows(tables, tm):
    n = tables[0].shape[0]
    if n >= tm:
        assert n % tm == 0
        return tables
    assert tm % n == 0
    return tuple(jnp.tile(t, (tm // n, 1)) for t in tables)


def _pad_lanes(v, offset):
    return jnp.zeros((1, LANES), F32).at[0, offset:offset + v.shape[0]].set(v.astype(F32))


def _q_layout_constants():
    n_nope = MLA_HEADS * QK_NOPE
    col = np.arange(n_nope + MLA_HEADS * QK_ROPE)
    head = np.where(col < n_nope, col // QK_NOPE, (col - n_nope) // QK_ROPE)
    same_head = (head[:, None] == head[None, :]).astype(np.float32)
    j = np.arange(MLA_HEADS * QK_ROPE)
    src = (j // QK_ROPE) * QK_ROPE + (j % QK_ROPE + ROPE_HALF) % QK_ROPE
    swap = (j[:, None] == src[None, :]).astype(np.float32)
    return same_head, swap


_Q_HEAD_SUM, _ROPE_SWAP = _q_layout_constants()


def _prepare_params(p):
    w_in = p['w_in']
    sizes = (GDN_CONV_DIM, GDN_V_DIM, GDN_HEADS, GDN_HEADS, Q_LORA, KV_LORA, QK_ROPE)
    offs = np.concatenate([[0], np.cumsum(sizes)])
    part = [w_in[:, int(offs[i]):int(offs[i + 1])] for i in range(len(sizes))]
    w_ba = jnp.zeros((D_MODEL, LANES), F32).at[:, :2 * GDN_HEADS].set(jnp.concatenate([part[2], part[3]], axis=1))

    uq = p['mla_w_uq'].reshape(Q_LORA, MLA_HEADS, QK_HEAD)
    w_uq = jnp.concatenate([uq[:, :, :QK_NOPE].reshape(Q_LORA, -1), uq[:, :, QK_NOPE:].reshape(Q_LORA, -1)], axis=1)
    ukv = p['mla_w_ukv'].reshape(KV_LORA, MLA_HEADS, QK_NOPE + V_HEAD)
    w_ukv = jnp.concatenate([ukv[:, :, :QK_NOPE].reshape(KV_LORA, -1), ukv[:, :, QK_NOPE:].reshape(KV_LORA, -1)], axis=1)

    w_router = jnp.zeros((D_MODEL, LANES), F32)
    w_router = w_router.at[:, :N_EXPERTS].set(p['w_expert_router'])
    w_router = w_router.at[:, N_EXPERTS:N_EXPERTS + N_GROUPS].set(p['w_group_router'])
    b_router = _pad_lanes(jnp.concatenate([p['b_expert_router'], p['b_group_router']]), 0)

    row = lambda v: v.astype(F32)[None, :]
    return {
        'ln_mix_w': row(p['ln_mix_w']),
        'w_qkv': part[0].astype(BF16), 'w_z': part[1].astype(BF16), 'w_ba': w_ba,
        'a_log': _pad_lanes(p['gdn_a_log'], GDN_HEADS), 'dt_bias': _pad_lanes(p['gdn_dt_bias'], GDN_HEADS),
        'w_cq': part[4].astype(BF16), 'w_ckv': part[5].astype(BF16), 'w_kpe': part[6].astype(BF16),
        'q_a_norm_w': row(p['mla_q_a_norm_w']), 'kv_a_norm_w': row(p['mla_kv_a_norm_w']),
        'w_uq': w_uq.astype(BF16), 'w_ukv': w_ukv.astype(BF16),
        'q_norm_w': row(jnp.concatenate([jnp.tile(p['mla_q_norm_w'][:QK_NOPE], MLA_HEADS),
                                         jnp.tile(p['mla_q_norm_w'][QK_NOPE:], MLA_HEADS)])),
        'q_head_sum': jnp.asarray(_Q_HEAD_SUM, BF16), 'rope_swap': jnp.asarray(_ROPE_SWAP, BF16),
        'k_norm_w': row(p['mla_k_norm_w']),
        'conv_w': p['gdn_conv_w'].astype(F32), 'gdn_norm_w': row(p['gdn_norm_w']),
        'w_out_gdn': p['w_out'][:GDN_V_DIM].astype(BF16), 'w_out_mla': p['w_out'][GDN_V_DIM:].astype(BF16),
        'ln_ffn_w': row(p['ln_ffn_w']), 'w_router': w_router, 'b_router': b_router,
        'w_exp_gate_up': jnp.concatenate([p['w_exp_gate'].astype(BF16), p['w_exp_up'].astype(BF16)], axis=-1),
        'w_exp_down': p['w_exp_down'].astype(BF16),
    }


def _tiles(bsz, seq, past):
    t = bsz * seq
    tm = min(1024, t)
    n_keys = past + seq
    if past == 0:
        tk = min(512, seq)
        tq = min(2 * tk, seq)
        lk = n_keys
    else:
        assert n_keys % CHUNK == 0
        tq = seq
        lk = -(-n_keys // LANES) * LANES
        tk = lk
    tb = min(256, seq)
    tmoe = min(1024, t)
    assert seq % tq == 0 and lk % tk == 0 and seq % tb == 0 and tb % CHUNK == 0 and t % tm == 0 and t % tmoe == 0
    return tm, tq, tk, lk, tb, tmoe


def _hybrid_layer(x, conv_hist, s0, ckv_past, kpe_past, p):
    bsz, seq, _ = x.shape
    past = ckv_past.shape[1]
    t = bsz * seq
    tm, tq, tk, lk, tb, tmoe = _tiles(bsz, seq, past)
    x2d = x.reshape(t, D_MODEL)

    rope_q = _tile_rows(_rope_tables(past + jnp.arange(seq, dtype=jnp.int32)), tm)
    n_keys = past + seq
    if past == 0 and tm // TOKEN_SUBTILES == tk:
        qkv, z, gates, q, ckv, kpe, k, vt = _inproj(x2d, p, rope_q, tm, key_tile=tk)
    else:
        qkv, z, gates, q, ckv, kpe = _inproj(x2d, p, rope_q, tm)
        zpad = lambda n: jnp.zeros((bsz, lk - n_keys, n), F32)
        ckv_all = jnp.concatenate([ckv_past.astype(F32), ckv.reshape(bsz, seq, KV_LORA), zpad(KV_LORA)], axis=1)
        kpe_all = jnp.concatenate([kpe_past.astype(F32), kpe.reshape(bsz, seq, QK_ROPE), zpad(QK_ROPE)], axis=1)
        rope_k = _tile_rows(_rope_tables(jnp.arange(lk, dtype=jnp.int32)), tk)
        k, vt = _kvprep(ckv_all.reshape(bsz * lk, KV_LORA), kpe_all.reshape(bsz * lk, QK_ROPE), p, rope_k, tk)
    ckv3, kpe3 = ckv.reshape(bsz, seq, KV_LORA), kpe.reshape(bsz, seq, QK_ROPE)
    mla_out = _flash(q, k, vt, bsz, seq, lk, past, tq, tk, FLASH_HEADS_PER_STEP, tq // min(tq, tk))

    hist_pad = jnp.concatenate([jnp.zeros((bsz, 8 - (CONV_W - 1), GDN_CONV_DIM), F32), conv_hist.astype(F32)], axis=1)
    gdn_out, s_new = _gdn(qkv, z, gates, hist_pad, s0.astype(F32), p, bsz, seq, tb)
    assert seq >= CONV_W - 1
    conv_new = qkv.reshape(bsz, seq, GDN_CONV_DIM)[:, seq - (CONV_W - 1):]

    x1, h2, comb = _outproj(x2d, gdn_out, mla_out, p, tm)
    y = _moe(h2, comb, x1, p, tmoe)
    return y.reshape(bsz, seq, D_MODEL), ckv3, kpe3, s_new.astype(s0.dtype), conv_new


def kernel(x_prompt, x_sample, cache_mla_ckv, cache_mla_kpe, state_gdn, state_gdn_conv, ln_mix_w, w_in, gdn_conv_w, gdn_a_log, gdn_dt_bias, gdn_norm_w, mla_q_a_norm_w, mla_w_uq, mla_kv_a_norm_w, mla_w_ukv, mla_q_norm_w, mla_k_norm_w, w_out, ln_ffn_w, w_group_router, b_group_router, w_expert_router, b_expert_router, w_exp_gate, w_exp_up, w_exp_down):
    depth = w_in.shape[0]
    bsz = x_prompt.shape[0]
    y_p, y_s = x_prompt, x_sample
    outs_p, outs_s = [], []
    for l in range(depth):
        p = _prepare_params({
            'ln_mix_w': ln_mix_w[l], 'w_in': w_in[l], 'gdn_conv_w': gdn_conv_w[l],
            'gdn_a_log': gdn_a_log[l], 'gdn_dt_bias': gdn_dt_bias[l], 'gdn_norm_w': gdn_norm_w[l],
            'mla_q_a_norm_w': mla_q_a_norm_w[l], 'mla_w_uq': mla_w_uq[l],
            'mla_kv_a_norm_w': mla_kv_a_norm_w[l], 'mla_w_ukv': mla_w_ukv[l],
            'mla_q_norm_w': mla_q_norm_w[l], 'mla_k_norm_w': mla_k_norm_w[l],
            'w_out': w_out[l], 'ln_ffn_w': ln_ffn_w[l],
            'w_group_router': w_group_router[l], 'b_group_router': b_group_router[l],
            'w_expert_router': w_expert_router[l], 'b_expert_router': b_expert_router[l],
            'w_exp_gate': w_exp_gate[l], 'w_exp_up': w_exp_up[l], 'w_exp_down': w_exp_down[l],
        })
        y_p, *st_p = _hybrid_layer(
            y_p,
            jnp.zeros((bsz, CONV_W - 1, GDN_CONV_DIM), y_p.dtype),
            jnp.zeros((bsz, GDN_HEADS, GDN_DK, GDN_DV), state_gdn.dtype),
            jnp.zeros((bsz, 0, KV_LORA), cache_mla_ckv.dtype),
            jnp.zeros((bsz, 0, QK_ROPE), cache_mla_kpe.dtype),
            p)
        y_s, *st_s = _hybrid_layer(y_s, state_gdn_conv[l], state_gdn[l], cache_mla_ckv[l], cache_mla_kpe[l], p)
        outs_p.append(st_p)
        outs_s.append(st_s)
    stack = lambda outs, i: jnp.stack([o[i] for o in outs])
    return (y_p, y_s,
            stack(outs_p, 0), stack(outs_p, 1), stack(outs_p, 2), stack(outs_p, 3),
            stack(outs_s, 0), stack(outs_s, 1), stack(outs_s, 2), stack(outs_s, 3))
```

```python
import functools
import math

import jax
import jax.numpy as jnp
import numpy as np
from jax import lax
from jax.experimental import pallas as pl
from jax.experimental.pallas import tpu as pltpu

F32 = jnp.float32
BF16 = jnp.bfloat16

D_MODEL = 1024
CHUNK = 64
RMS_EPS = 1e-6

GDN_HEADS = 4
GDN_DK = 128
GDN_DV = 128
GDN_QK_DIM = GDN_HEADS * GDN_DK
GDN_V_DIM = GDN_HEADS * GDN_DV
GDN_CONV_DIM = 2 * GDN_QK_DIM + GDN_V_DIM
CONV_W = 4

MLA_HEADS = 4
Q_LORA = 512
KV_LORA = 256
QK_NOPE = 128
QK_ROPE = 64
ROPE_HALF = QK_ROPE // 2
V_HEAD = 128
V_ROWS = V_HEAD + 16
QK_HEAD = QK_NOPE + QK_ROPE
QK_PAD = QK_NOPE + 2 * QK_ROPE
ROPE_THETA = 10000.0

N_GROUPS = 4
EXPERTS_PER_GROUP = 8
N_EXPERTS = N_GROUPS * EXPERTS_PER_GROUP
D_EXPERT = 256

LANES = 128
VMEM_LIMIT_BYTES = 58 * 1024 * 1024

Q_SCALE = (QK_HEAD ** -0.5) * math.log2(math.e)
MASK_VALUE = -1e30
FLASH_HEADS_PER_STEP = 2
GDN_SOLVE_PASSES = 1
TOKEN_SUBTILES = 2
MOE_SEG_ALIGN = 16
MOE_ROW_BLOCK = 128
MOE_EXPERTS_PER_STEP = 4


def _mm(a, b):
    return jnp.dot(a.astype(BF16), b.astype(BF16), preferred_element_type=F32)


def _mm_nt(a, b):
    return lax.dot_general(a.astype(BF16), b.astype(BF16), (((1,), (1,)), ((), ())),
                           preferred_element_type=F32)


def _mm_tn(a, b):
    return lax.dot_general(a.astype(BF16), b.astype(BF16), (((0,), (0,)), ((), ())),
                           preferred_element_type=F32)


def _split(a):
    hi = a.astype(BF16)
    lo = (a - hi.astype(F32)).astype(BF16)
    return hi, lo


def _mm3(a, b):
    ah, al = _split(a)
    bh, bl = _split(b)
    return _mm(ah, bh) + (_mm(ah, bl) + _mm(al, bh))


def _mm_solve(a, b):
    return _mm3(a, b) if GDN_SOLVE_PASSES == 3 else _mm(a, b)


def _mm_exact_rhs(a, b01):
    a1 = a.astype(BF16)
    r1 = a - a1.astype(F32)
    a2 = r1.astype(BF16)
    a3 = (r1 - a2.astype(F32)).astype(BF16)
    return _mm(a1, b01) + (_mm(a2, b01) + _mm(a3, b01))


def _mm2_exact_rhs(a, b01):
    hi, lo = _split(a)
    return _mm(hi, b01) + _mm(lo, b01)


def _mm_exact_lhs(a01, b):
    b1 = b.astype(BF16)
    r1 = b - b1.astype(F32)
    b2 = r1.astype(BF16)
    b3 = (r1 - b2.astype(F32)).astype(BF16)
    return _mm(a01, b1) + (_mm(a01, b2) + _mm(a01, b3))


def _rms(x, w):
    return x * lax.rsqrt(jnp.mean(x * x, axis=-1, keepdims=True) + RMS_EPS) * w


def _silu(x):
    return x * jax.nn.sigmoid(x)


def _swap_halves(x):
    h = x.shape[-1] // 2
    return jnp.concatenate([x[:, h:], x[:, :h]], axis=-1)


def _subtiles(tm):
    n = tm // TOKEN_SUBTILES
    return [pl.ds(i * n, n) for i in range(TOKEN_SUBTILES)]


def _const_spec(shape):
    nd = len(shape)
    return pl.BlockSpec(shape, lambda *_: (0,) * nd, pipeline_mode=pl.Buffered(1))


def _params(*sem):
    return pltpu.CompilerParams(dimension_semantics=sem, vmem_limit_bytes=VMEM_LIMIT_BYTES)


def _inproj_kernel(x_ref, lnw_ref, wqkv_ref, wz_ref, wba_ref, alog_ref, dtb_ref, wcq_ref, qanw_ref,
                   wuq_ref, qnw_ref, hsum_ref, swap_ref, cos_ref, sin_ref, wckv_ref, kvanw_ref, wkpe_ref,
                   wukv_ref, knw_ref, qkv_ref, z_ref, gates_ref, q_ref, ckv_ref, kpe_ref, *kv_refs):
    subs = _subtiles(x_ref.shape[0])
    h_ = [_rms(x_ref[r, :], lnw_ref[...]) for r in subs]
    hb_ = [h.astype(BF16) for h in h_]
    raw_, ckv_, cq_, kpe_ = [], [], [], []
    for r, h, hb in zip(subs, h_, hb_):
        qkv_ref[r, :] = _mm(hb, wqkv_ref[...])
        z_ref[r, :] = _mm(hb, wz_ref[...])
        kpe_.append(_mm(hb, wkpe_ref[...]))
        kpe_ref[r, :] = kpe_[-1]
        raw_.append(_mm3(h, wba_ref[...]))
        ckv_.append(_mm(hb, wckv_ref[...]))
        cq_.append(_mm(hb, wcq_ref[...]))

    qall_ = [_mm(_rms(cq, qanw_ref[...]), wuq_ref[...]) for cq in cq_]

    for i, (r, raw, ckv) in enumerate(zip(subs, raw_, ckv_)):
        lane = lax.broadcasted_iota(jnp.int32, raw.shape, 1)
        pre = raw + dtb_ref[...]
        softplus = jnp.maximum(pre, 0.0) + jnp.log1p(jnp.exp(-jnp.abs(pre)))
        g = -jnp.exp(alog_ref[...]) * softplus
        gates_ref[r, :] = jnp.where(lane < GDN_HEADS, jax.nn.sigmoid(raw), g)
        ckv_n = _rms(ckv, kvanw_ref[...])
        ckv_ref[r, :] = ckv_n
        if kv_refs:
            k_ref, vt_ref = kv_refs
            heads = _keys_values(ckv_n, kpe_[i], cos_ref[r, :], sin_ref[r, :], wukv_ref[...], knw_ref[...])
            for hd, (kh, vt) in enumerate(heads):
                k_ref[hd, r, :] = kh
                vt_ref[hd, i] = vt

    rope0 = MLA_HEADS * QK_NOPE
    lane = lax.broadcasted_iota(jnp.int32, (1, 2 * QK_ROPE), 1)
    for r, qall in zip(subs, qall_):
        ssq = _mm(qall * qall, hsum_ref[...])
        qn = qall * (lax.rsqrt(ssq * (1.0 / QK_HEAD) + RMS_EPS) * Q_SCALE) * qnw_ref[...]
        rw = qn[:, rope0:]
        cos2 = jnp.concatenate([cos_ref[r, :]] * 2, axis=-1)
        sin2 = jnp.concatenate([sin_ref[r, :]] * 2, axis=-1)
        rot = rw * cos2 + _mm2_exact_rhs(rw, swap_ref[...]) * sin2
        for hd in range(MLA_HEADS):
            pair = rot[:, (hd // 2) * 2 * QK_ROPE:(hd // 2 + 1) * 2 * QK_ROPE]
            mine = jnp.where((lane // QK_ROPE) == (hd % 2), pair, 0.0)
            q_ref[hd, r, :] = jnp.concatenate([qn[:, hd * QK_NOPE:(hd + 1) * QK_NOPE], mine], axis=-1).astype(BF16)


def _inproj(x2d, p, rope_q, tm, key_tile=None):
    t = x2d.shape[0]
    nt = t // tm
    cos_t, sin_t = rope_q
    n_rope_blocks = cos_t.shape[0] // tm
    rope_spec = pl.BlockSpec((tm, 2 * QK_ROPE), lambda i: (i % n_rope_blocks, 0))
    row = lambda n: pl.BlockSpec((tm, n), lambda i: (i, 0))
    consts = [p['ln_mix_w'], p['w_qkv'], p['w_z'], p['w_ba'], p['a_log'], p['dt_bias'], p['w_cq'],
              p['q_a_norm_w'], p['w_uq'], p['q_norm_w'], p['q_head_sum'], p['rope_swap']]
    consts2 = [p['w_ckv'], p['kv_a_norm_w'], p['w_kpe'], p['w_ukv'], p['k_norm_w']]
    in_specs = ([row(D_MODEL)] + [_const_spec(c.shape) for c in consts] + [rope_spec, rope_spec]
                + [_const_spec(c.shape) for c in consts2])
    out_shape = (
        jax.ShapeDtypeStruct((t, GDN_CONV_DIM), F32),
        jax.ShapeDtypeStruct((t, GDN_V_DIM), F32),
        jax.ShapeDtypeStruct((t, LANES), F32),
        jax.ShapeDtypeStruct((MLA_HEADS, t, QK_PAD), BF16),
        jax.ShapeDtypeStruct((t, KV_LORA), F32),
        jax.ShapeDtypeStruct((t, QK_ROPE), F32),
    )
    out_specs = (row(GDN_CONV_DIM), row(GDN_V_DIM), row(LANES),
                 pl.BlockSpec((MLA_HEADS, tm, QK_PAD), lambda i: (0, i, 0)),
                 row(KV_LORA), row(QK_ROPE))
    if key_tile is not None:
        assert tm // TOKEN_SUBTILES == key_tile
        out_shape += (jax.ShapeDtypeStruct((MLA_HEADS, t, QK_PAD), BF16),
                      jax.ShapeDtypeStruct((MLA_HEADS, t // key_tile, V_ROWS, key_tile), BF16))
        out_specs += (pl.BlockSpec((MLA_HEADS, tm, QK_PAD), lambda i: (0, i, 0)),
                      pl.BlockSpec((MLA_HEADS, TOKEN_SUBTILES, V_ROWS, key_tile), lambda i: (0, i, 0, 0)))
    return pl.pallas_call(
        _inproj_kernel, grid=(nt,), in_specs=in_specs, out_specs=out_specs, out_shape=out_shape,
        compiler_params=_params("parallel"), name="inproj",
    )(x2d, *consts, cos_t, sin_t, *consts2)


def _keys_values(ckv, kpe, cosf, sinf, wukv, knw):
    kv = _mm(ckv, wukv)
    kw = kpe * knw[:, QK_NOPE:]
    krot = kw * cosf[:, :QK_ROPE] + _swap_halves(kw) * sinf[:, :QK_ROPE]
    krot = jnp.concatenate([krot, krot], axis=-1)
    s_kpe = jnp.sum(kpe * kpe, axis=-1, keepdims=True)
    v0 = MLA_HEADS * QK_NOPE
    pad_row = lax.broadcasted_iota(jnp.int32, (V_ROWS - V_HEAD, kv.shape[0]), 0)
    ones_rows = jnp.where(pad_row == 0, 1.0, 0.0)
    out = []
    for hd in range(MLA_HEADS):
        kn = kv[:, hd * QK_NOPE:(hd + 1) * QK_NOPE]
        inv = lax.rsqrt((jnp.sum(kn * kn, axis=-1, keepdims=True) + s_kpe) * (1.0 / QK_HEAD) + RMS_EPS)
        kh = jnp.concatenate([kn * knw[:, :QK_NOPE], krot], axis=-1) * inv
        vt = kv[:, v0 + hd * V_HEAD: v0 + (hd + 1) * V_HEAD].T
        out.append((kh.astype(BF16), jnp.concatenate([vt, ones_rows], axis=0).astype(BF16)))
    return out


def _kvprep_kernel(ckv_old_ref, kpe_old_ref, ckv_new_ref, kpe_new_ref, cos_ref, sin_ref, wukv_ref, knw_ref,
                   k_ref, vt_ref):
    n_pad = k_ref.shape[1] - ckv_old_ref.shape[1] - ckv_new_ref.shape[1]

    def rows(old_ref, new_ref):
        return jnp.concatenate([old_ref[0], new_ref[0], jnp.zeros((n_pad, old_ref.shape[2]), F32)], axis=0)

    heads = _keys_values(rows(ckv_old_ref, ckv_new_ref), rows(kpe_old_ref, kpe_new_ref),
                         cos_ref[...], sin_ref[...], wukv_ref[...], knw_ref[...])
    for hd, (kh, vt) in enumerate(heads):
        k_ref[hd] = kh
        vt_ref[hd, 0] = vt


def _kvprep(ckv_old, kpe_old, ckv_new, kpe_new, p, rope_k, lk):
    bsz = ckv_old.shape[0]
    cos_t, sin_t = rope_k
    per_seq = lambda a: pl.BlockSpec((1,) + a.shape[1:], lambda b: (b, 0, 0))
    return pl.pallas_call(
        _kvprep_kernel, grid=(bsz,),
        in_specs=[per_seq(ckv_old), per_seq(kpe_old), per_seq(ckv_new), per_seq(kpe_new),
                  _const_spec(cos_t.shape), _const_spec(sin_t.shape),
                  _const_spec(p['w_ukv'].shape), _const_spec(p['k_norm_w'].shape)],
        out_specs=(pl.BlockSpec((MLA_HEADS, lk, QK_PAD), lambda b: (0, b, 0)),
                   pl.BlockSpec((MLA_HEADS, 1, V_ROWS, lk), lambda b: (0, b, 0, 0))),
        out_shape=(jax.ShapeDtypeStruct((MLA_HEADS, bsz * lk, QK_PAD), BF16),
                   jax.ShapeDtypeStruct((MLA_HEADS, bsz, V_ROWS, lk), BF16)),
        compiler_params=_params("parallel"), name="kvprep",
    )(ckv_old, kpe_old, ckv_new, kpe_new, cos_t, sin_t, p['w_ukv'], p['k_norm_w'])


def _flash_kernel(q_ref, k_ref, vt_ref, o_ref, m_sc, acc_sc, *, tq, tk, nk, past, hps, qsplit):
    i = pl.program_id(2)
    q_first = past + i * tq
    q_last = q_first + tq - 1
    n_full = jnp.minimum(nk, ((q_first // CHUNK + 1) * CHUNK) // tk)
    n_vis = jnp.minimum(nk, ((q_last // CHUNK + 1) * CHUNK + tk - 1) // tk)

    m_sc[...] = jnp.full(m_sc.shape, MASK_VALUE, F32)
    acc_sc[...] = jnp.zeros(acc_sc.shape, F32)

    tqs = tq // qsplit
    streams = [(hd, pl.ds(c * tqs, tqs)) for hd in range(hps) for c in range(qsplit)]

    def step(j, masked):
        k0 = pl.multiple_of(j * tk, tk)
        s_ = [_mm_nt(k_ref[hd, pl.ds(k0, tk), :], q_ref[hd, cols, :]) for hd, cols in streams]
        if masked:
            kpos = k0 + lax.broadcasted_iota(jnp.int32, (tk, 1), 0)
            for n, (hd, cols) in enumerate(streams):
                qpos = q_first + (n % qsplit) * tqs + lax.broadcasted_iota(jnp.int32, (1, tqs), 1)
                s_[n] = jnp.where((kpos // CHUNK) <= (qpos // CHUNK), s_[n], MASK_VALUE)
        p_, alpha_ = [], []
        for n, (hd, cols) in enumerate(streams):
            m_old = m_sc[hd, :, cols]
            m_new = jnp.maximum(m_old, jnp.max(s_[n], axis=0, keepdims=True))
            alpha_.append(jnp.exp2(m_old - m_new))
            p_.append(jnp.exp2(s_[n] - m_new).astype(BF16))
            m_sc[hd, :, cols] = m_new
        for n, (hd, cols) in enumerate(streams):
            acc_sc[hd, :, cols] = alpha_[n] * acc_sc[hd, :, cols] + _mm(vt_ref[hd, j], p_[n])

    def full_body(j, c):
        step(j, False)
        return c

    def masked_body(j, c):
        step(j, True)
        return c

    lax.fori_loop(0, n_full, full_body, 0)
    lax.fori_loop(n_full, n_vis, masked_body, 0)
    for hd in range(hps):
        o_t = acc_sc[hd, :V_HEAD, :] / acc_sc[hd, V_HEAD:V_HEAD + 1, :]
        o_ref[:, hd * V_HEAD:(hd + 1) * V_HEAD] = o_t.T.astype(o_ref.dtype)


def _flash(q, k, vt, bsz, lq, lk, past, tq, tk, hps, qsplit):
    nq = lq // tq
    nk = lk // tk
    kern = functools.partial(_flash_kernel, tq=tq, tk=tk, nk=nk, past=past, hps=hps, qsplit=qsplit)
    once = pl.Buffered(1 if nq > 1 else 2)
    return pl.pallas_call(
        kern, grid=(bsz, MLA_HEADS // hps, nq),
        in_specs=[pl.BlockSpec((hps, tq, QK_PAD), lambda b, h, i: (h, b * nq + i, 0)),
                  pl.BlockSpec((hps, lk, QK_PAD), lambda b, h, i: (h, b, 0), pipeline_mode=once),
                  pl.BlockSpec((hps, nk, V_ROWS, tk), lambda b, h, i: (h, b, 0, 0), pipeline_mode=once)],
        out_specs=pl.BlockSpec((tq, hps * V_HEAD), lambda b, h, i: (b * nq + i, h)),
        out_shape=jax.ShapeDtypeStruct((bsz * lq, MLA_HEADS * V_HEAD), BF16),
        scratch_shapes=[pltpu.VMEM((hps, 1, tq), F32), pltpu.VMEM((hps, V_ROWS, tq), F32)],
        compiler_params=_params("parallel", "parallel", "arbitrary"), name="flash",
    )(q, k, vt)


def _gdn_kernel(qkv_ref, z_ref, gates_ref, hist_ref, s0_ref, convw_ref, normw_ref,
                o_ref, s_ref, xbuf, *, tb, nseq):
    j = pl.program_id(1)
    nc = tb // CHUNK
    pad = 8
    seqs = range(nseq)
    streams = [(b, hd) for b in seqs for hd in range(GDN_HEADS)]

    @pl.when(j == 0)
    def _():
        xbuf[:, 0:pad, :] = hist_ref[...]
        s_ref[...] = s0_ref[...]

    cw = convw_ref[...]
    y_ = []
    for b in seqs:
        xbuf[b, pad:pad + tb, :] = qkv_ref[b]
        y = xbuf[b, pad - 3:pad - 3 + tb, :] * cw[0:1, :]
        for i in range(1, CONV_W):
            y = y + xbuf[b, pad - 3 + i:pad - 3 + i + tb, :] * cw[i:i + 1, :]
        xbuf[b, 0:pad, :] = xbuf[b, tb:tb + pad, :]
        y_.append(_silu(y))

    row = lax.broadcasted_iota(jnp.int32, (tb, tb), 0)
    col = lax.broadcasted_iota(jnp.int32, (tb, tb), 1)
    same = (row // CHUNK) == (col // CHUNK)
    incl = same & (row >= col)
    strict = same & (row > col)
    tril01 = jnp.where(incl, 1.0, 0.0).astype(BF16)
    triu01 = jnp.where(same & (row <= col), 1.0, 0.0).astype(BF16)
    eye = jnp.where(row == col, 1.0, 0.0)
    gates_ = [gates_ref[b] for b in seqs]
    gc_cols_ = [_mm_exact_lhs(tril01, g) for g in gates_]
    gc_rows_ = [_mm_exact_rhs(g.T[0:8, :], triu01) for g in gates_]

    q_, k_, v_, beta_, gc_, decay_, kb_ = [], [], [], [], [], [], []
    for n, (b, hd) in enumerate(streams):
        y = y_[b]
        qh = y[:, hd * GDN_DK:(hd + 1) * GDN_DK]
        kh = y[:, GDN_QK_DIM + hd * GDN_DK: GDN_QK_DIM + (hd + 1) * GDN_DK]
        q_.append(qh * (lax.rsqrt(jnp.sum(qh * qh, axis=-1, keepdims=True) + RMS_EPS) * (GDN_DK ** -0.5)))
        k_.append(kh * lax.rsqrt(jnp.sum(kh * kh, axis=-1, keepdims=True) + RMS_EPS))
        v_.append(y[:, 2 * GDN_QK_DIM + hd * GDN_DV: 2 * GDN_QK_DIM + (hd + 1) * GDN_DV])
        beta_.append(gates_[b][:, hd:hd + 1])
        gc = gc_cols_[b][:, GDN_HEADS + hd: GDN_HEADS + hd + 1]
        gr = gc_rows_[b][GDN_HEADS + hd: GDN_HEADS + hd + 1, :]
        gc_.append(gc)
        decay_.append(jnp.where(incl, jnp.exp(jnp.where(incl, gc - gr, 0.0)), 0.0))
        kb_.append(k_[n] * beta_[n])
    ns = range(len(streams))
    khb_ = [k.astype(BF16) for k in k_]
    pw_ = [jnp.where(strict, -(_mm_nt(kb_[n], khb_[n]) * decay_[n]), 0.0) for n in ns]
    attn_ = [_mm_nt(q_[n], khb_[n]) * decay_[n] for n in ns]

    tinv_ = [eye + pw for pw in pw_]
    for _ in range(int(math.log2(CHUNK)) - 1):
        pw_ = [_mm_solve(pw, pw) for pw in pw_]
        tinv_ = [tinv_[n] + _mm_solve(tinv_[n], pw_[n]) for n in ns]

    eg_ = [jnp.exp(gc) for gc in gc_]
    sol_ = [_mm_solve(tinv_[n], jnp.concatenate([v_[n] * beta_[n], kb_[n] * eg_[n]], axis=-1)) for n in ns]
    qdec_ = [q_[n] * eg_[n] for n in ns]

    s_ = [s_ref[b, hd] for b, hd in streams]
    us_ = [[] for _ in ns]
    oi_ = [[] for _ in ns]
    for c in range(nc):
        r0, r1 = c * CHUNK, (c + 1) * CHUNK
        for n in ns:
            g_end = gc_[n][r1 - 1:r1, :]
            k_end = k_[n][r0:r1] * jnp.exp(g_end - gc_[n][r0:r1])
            both = _mm(jnp.concatenate([sol_[n][r0:r1, GDN_DV:], qdec_[n][r0:r1]], axis=0), s_[n])
            u = sol_[n][r0:r1, :GDN_DV] - both[:CHUNK]
            oi_[n].append(both[CHUNK:])
            us_[n].append(u)
            s_[n] = s_[n] * jnp.exp(g_end) + _mm_tn(k_end, u)
    for n, (b, hd) in enumerate(streams):
        s_ref[b, hd] = s_[n]
        u_all = us_[n][0] if nc == 1 else jnp.concatenate(us_[n], axis=0)
        o_all = oi_[n][0] if nc == 1 else jnp.concatenate(oi_[n], axis=0)
        o = o_all + _mm(attn_[n], u_all)
        o = _rms(o, normw_ref[...]) * _silu(z_ref[b, :, hd * GDN_DV:(hd + 1) * GDN_DV])
        o_ref[b, :, hd * GDN_DV:(hd + 1) * GDN_DV] = o.astype(o_ref.dtype)


def _gdn(qkv2d, z2d, gates2d, hist_pad, s0, p, bsz, seq, tb, nseq):
    nb = seq // tb
    kern = functools.partial(_gdn_kernel, tb=tb, nseq=nseq)
    row = lambda n: pl.BlockSpec((nseq, tb, n), lambda g, j: (g, j, 0))
    state_spec = pl.BlockSpec((nseq, GDN_HEADS, GDN_DK, GDN_DV), lambda g, j: (g, 0, 0, 0))
    view = lambda a: a.reshape(bsz, seq, a.shape[-1])
    out, s_new = pl.pallas_call(
        kern, grid=(bsz // nseq, nb),
        in_specs=[row(GDN_CONV_DIM), row(GDN_V_DIM), row(LANES),
                  pl.BlockSpec((nseq, 8, GDN_CONV_DIM), lambda g, j: (g, 0, 0)),
                  state_spec, _const_spec(p['conv_w'].shape), _const_spec(p['gdn_norm_w'].shape)],
        out_specs=(row(GDN_V_DIM), state_spec),
        out_shape=(jax.ShapeDtypeStruct((bsz, seq, GDN_V_DIM), BF16),
                   jax.ShapeDtypeStruct((bsz, GDN_HEADS, GDN_DK, GDN_DV), F32)),
        scratch_shapes=[pltpu.VMEM((nseq, tb + 8, GDN_CONV_DIM), F32)],
        compiler_params=_params("parallel", "arbitrary"), name="gdn",
    )(view(qkv2d), view(z2d), view(gates2d), hist_pad, s0, p['conv_w'], p['gdn_norm_w'])
    return out.reshape(bsz * seq, GDN_V_DIM), s_new


def _outproj_kernel(x_ref, g_ref, a_ref, wog_ref, woa_ref, lnw_ref, wr_ref, br_ref,
                    x1_ref, h2_ref, comb_ref):
    subs = _subtiles(x_ref.shape[0])
    h2_ = []
    for r in subs:
        x1 = x_ref[r, :] + _mm(g_ref[r, :], wog_ref[...]) + _mm(a_ref[r, :], woa_ref[...])
        x1_ref[r, :] = x1
        h2_.append(_rms(x1, lnw_ref[...]))
        h2_ref[r, :] = h2_[-1].astype(BF16)
    logits_ = [_mm3(h2, wr_ref[...]) + br_ref[...] for h2 in h2_]
    for r, logits in zip(subs, logits_):
        comb_ref[r, :] = _route(logits)


def _route(logits):
    lane = lax.broadcasted_iota(jnp.int32, logits.shape, 1)
    big = jnp.int32(LANES)
    is_grp = (lane >= N_EXPERTS) & (lane < N_EXPERTS + N_GROUPS)
    gl = jnp.where(is_grp, logits, -jnp.inf)
    gmax = jnp.max(gl, axis=-1, keepdims=True)
    grp_p = 1.0 / jnp.sum(jnp.exp(gl - gmax), axis=-1, keepdims=True)
    grp_idx = jnp.min(jnp.where(gl == gmax, lane - N_EXPERTS, big), axis=-1, keepdims=True)

    in_grp = (lane < N_EXPERTS) & ((lane // EXPERTS_PER_GROUP) == grp_idx)
    el = jnp.where(in_grp, logits, -jnp.inf)
    m1 = jnp.max(el, axis=-1, keepdims=True)
    i1 = jnp.min(jnp.where(el == m1, lane, big), axis=-1, keepdims=True)
    el2 = jnp.where(lane == i1, -jnp.inf, el)
    m2 = jnp.max(el2, axis=-1, keepdims=True)
    i2 = jnp.min(jnp.where(el2 == m2, lane, big), axis=-1, keepdims=True)
    e2 = jnp.exp(m2 - m1)
    w1 = 1.0 / (1.0 + e2)
    w2 = e2 * w1
    return jnp.where(lane == 0, i1.astype(F32), jnp.where(lane == 1, i2.astype(F32),
                     jnp.where(lane == 2, grp_p * w1, jnp.where(lane == 3, grp_p * w2, 0.0))))


def _outproj(x2d, gdn_out, mla_out, p, tm):
    t = x2d.shape[0]
    row = lambda n: pl.BlockSpec((tm, n), lambda i: (i, 0))
    consts = [p['w_out_gdn'], p['w_out_mla'], p['ln_ffn_w'], p['w_router'], p['b_router']]
    return pl.pallas_call(
        _outproj_kernel, grid=(t // tm,),
        in_specs=[row(D_MODEL), row(GDN_V_DIM), row(MLA_HEADS * V_HEAD)] + [_const_spec(c.shape) for c in consts],
        out_specs=(row(D_MODEL), row(D_MODEL), row(LANES)),
        out_shape=(jax.ShapeDtypeStruct((t, D_MODEL), F32), jax.ShapeDtypeStruct((t, D_MODEL), BF16),
                   jax.ShapeDtypeStruct((t, LANES), F32)),
        compiler_params=_params("parallel"), name="outproj",
    )(x2d, gdn_out, mla_out, *consts)


def _moe_slots(tm):
    n = 2 * tm + N_EXPERTS * MOE_SEG_ALIGN + MOE_ROW_BLOCK
    return -(-n // LANES) * LANES


def _moe_kernel(h_ref, route_ref, x1_ref, wgu_ref, wd_ref, y_ref, xs_sc, ys_sc, dcol_sc, offs_ref,
                *, tm, slots, n_chunks):
    g = pl.program_id(1)
    chunk = slots // n_chunks
    lb = min(2 * LANES, tm)
    nlb = tm // lb

    @pl.when(g == 0)
    def _sort_picks():
        rt = route_ref[...].T
        eid = lax.broadcasted_iota(jnp.int32, (N_EXPERTS, lb), 0).astype(F32)
        pieces = [jnp.where(rt[k:k + 1, b * lb:(b + 1) * lb] == eid, 1.0, 0.0)
                  for k in range(2) for b in range(nlb)]
        ind = jnp.concatenate(pieces, axis=0)
        r_i = lax.broadcasted_iota(jnp.int32, (lb, lb), 0)
        c_i = lax.broadcasted_iota(jnp.int32, (lb, lb), 1)
        before = _mm(ind, jnp.where(r_i < c_i, 1.0, 0.0))
        tot = jnp.sum(ind, axis=1, keepdims=True)
        run = jnp.zeros((N_EXPERTS, 1), F32)
        runs = []
        for i in range(len(pieces)):
            runs.append(run)
            run = run + tot[i * N_EXPERTS:(i + 1) * N_EXPERTS]
        seg = jnp.floor((run + (MOE_SEG_ALIGN - 1)) * (1.0 / MOE_SEG_ALIGN)) * MOE_SEG_ALIGN
        er = lax.broadcasted_iota(jnp.int32, (N_EXPERTS, N_EXPERTS), 0)
        ec = lax.broadcasted_iota(jnp.int32, (N_EXPERTS, N_EXPERTS), 1)
        start = _mm_exact_lhs(jnp.where(er > ec, 1.0, 0.0).astype(BF16),
                              jnp.broadcast_to(seg, (N_EXPERTS, LANES)))[:, 0:1]
        start_i = start.astype(jnp.int32)
        for e in range(N_EXPERTS):
            offs_ref[e] = start_i[e, 0]
        offs_ref[N_EXPERTS] = start_i[N_EXPERTS - 1, 0] + seg.astype(jnp.int32)[N_EXPERTS - 1, 0]

        dest = []
        for k in range(2):
            cols = []
            for b in range(nlb):
                i = k * nlb + b
                piece = ind[i * N_EXPERTS:(i + 1) * N_EXPERTS]
                pos = start + runs[i] + before[i * N_EXPERTS:(i + 1) * N_EXPERTS]
                cols.append(jnp.sum(piece * pos, axis=0, keepdims=True))
            dest.append(cols[0] if nlb == 1 else jnp.concatenate(cols, axis=1))
        rec = jnp.concatenate([dest[0], dest[1], rt[2:4], jnp.zeros((LANES - 4, tm), F32)], axis=0)
        dcol_sc[...] = rec.T

        h = h_ref[...]
        for c in range(n_chunks):
            s_i = (c * chunk + lax.broadcasted_iota(jnp.int32, (chunk, 1), 0)).astype(F32)
            sel = jnp.where(s_i == dest[0], 1.0, jnp.where(s_i == dest[1], 1.0, 0.0))
            xs_sc[c * chunk:(c + 1) * chunk, :] = _mm(sel, h).astype(BF16)
        ys_sc[...] = jnp.zeros(ys_sc.shape, BF16)

    steps = range(MOE_EXPERTS_PER_STEP)
    offs = [offs_ref[g * MOE_EXPERTS_PER_STEP + q] for q in steps] + [offs_ref[(g + 1) * MOE_EXPERTS_PER_STEP]]

    def rows(q, b):
        return pl.ds(pl.multiple_of(offs[q] + b * MOE_ROW_BLOCK, MOE_SEG_ALIGN), MOE_ROW_BLOCK)

    def gated(hid):
        return _silu(hid[:, :D_EXPERT]) * hid[:, D_EXPERT:]

    for q in steps:
        n_blocks = (offs[q + 1] - offs[q] + (MOE_ROW_BLOCK - 1)) // MOE_ROW_BLOCK

        def block(b, carry, q=q):
            act = gated(_mm(xs_sc[rows(q, b), :], wgu_ref[q]))
            ys_sc[rows(q, b), :] = _mm(act, wd_ref[q]).astype(BF16)
            return carry

        lax.fori_loop(1, n_blocks, block, 0)

    hid_ = [_mm(xs_sc[rows(q, 0), :], wgu_ref[q]) for q in steps]
    act_ = [gated(h) for h in hid_]
    out_ = [_mm(act_[q], wd_ref[q]).astype(BF16) for q in steps]
    for q in steps:
        ys_sc[rows(q, 0), :] = out_[q]

    @pl.when(g == pl.num_programs(1) - 1)
    def _combine():
        rec = dcol_sc[...]
        d0, d1, c0, c1 = rec[:, 0:1], rec[:, 1:2], rec[:, 2:3], rec[:, 3:4]
        y = x1_ref[...]
        for c in range(n_chunks):
            s_i = (c * chunk + lax.broadcasted_iota(jnp.int32, (1, chunk), 1)).astype(F32)
            wsel = jnp.where(s_i == d0, c0, jnp.where(s_i == d1, c1, 0.0))
            y = y + _mm(wsel, ys_sc[c * chunk:(c + 1) * chunk, :])
        y_ref[...] = y


def _moe(h2, route, x1, p, tm):
    t = h2.shape[0]
    slots = _moe_slots(tm)
    n_chunks = 3 if slots % (3 * LANES) == 0 else 1
    kern = functools.partial(_moe_kernel, tm=tm, slots=slots, n_chunks=n_chunks)
    once = pl.Buffered(1)
    row = lambda n: pl.BlockSpec((tm, n), lambda i, g: (i, 0), pipeline_mode=once)
    eps = MOE_EXPERTS_PER_STEP
    return pl.pallas_call(
        kern, grid=(t // tm, N_EXPERTS // eps),
        in_specs=[row(D_MODEL), row(LANES), row(D_MODEL),
                  pl.BlockSpec((eps, D_MODEL, 2 * D_EXPERT), lambda i, g: (g, 0, 0)),
                  pl.BlockSpec((eps, D_EXPERT, D_MODEL), lambda i, g: (g, 0, 0))],
        out_specs=pl.BlockSpec((tm, D_MODEL), lambda i, g: (i, 0)),
        out_shape=jax.ShapeDtypeStruct((t, D_MODEL), F32),
        scratch_shapes=[pltpu.VMEM((slots, D_MODEL), BF16), pltpu.VMEM((slots, D_MODEL), BF16),
                        pltpu.VMEM((tm, LANES), F32), pltpu.SMEM((N_EXPERTS + 1,), jnp.int32)],
        compiler_params=_params("parallel", "arbitrary"), name="moe",
    )(h2, route, x1, p['w_exp_gate_up'], p['w_exp_down'])


def _rope_tables(pos):
    inv = ROPE_THETA ** (-jnp.arange(ROPE_HALF, dtype=F32) / ROPE_HALF)
    ang = pos.astype(F32)[:, None] * inv[None, :]
    cos, sin = jnp.cos(ang), jnp.sin(ang)
    return jnp.concatenate([cos, cos] * 2, axis=-1), jnp.concatenate([-sin, sin] * 2, axis=-1)


def _tile_rows(tables, tm):
    n = tables[0].shape[0]
    if n >= tm:
        assert n % tm == 0
        return tables
    assert tm % n == 0
    return tuple(jnp.tile(t, (tm // n, 1)) for t in tables)


def _pad_lanes(v, offset):
    return jnp.zeros((1, LANES), F32).at[0, offset:offset + v.shape[0]].set(v.astype(F32))


def _q_layout_constants():
    n_nope = MLA_HEADS * QK_NOPE
    col = np.arange(n_nope + MLA_HEADS * QK_ROPE)
    head = np.where(col < n_nope, col // QK_NOPE, (col - n_nope) // QK_ROPE)
    same_head = (head[:, None] == head[None, :]).astype(np.float32)
    j = np.arange(MLA_HEADS * QK_ROPE)
    src = (j // QK_ROPE) * QK_ROPE + (j % QK_ROPE + ROPE_HALF) % QK_ROPE
    swap = (j[:, None] == src[None, :]).astype(np.float32)
    return same_head, swap


_Q_HEAD_SUM, _ROPE_SWAP = _q_layout_constants()


def _prepare_params(p):
    w_in = p['w_in']
    sizes = (GDN_CONV_DIM, GDN_V_DIM, GDN_HEADS, GDN_HEADS, Q_LORA, KV_LORA, QK_ROPE)
    offs = np.concatenate([[0], np.cumsum(sizes)])
    part = [w_in[:, int(offs[i]):int(offs[i + 1])] for i in range(len(sizes))]
    w_ba = jnp.zeros((D_MODEL, LANES), F32).at[:, :2 * GDN_HEADS].set(jnp.concatenate([part[2], part[3]], axis=1))

    uq = p['mla_w_uq'].reshape(Q_LORA, MLA_HEADS, QK_HEAD)
    w_uq = jnp.concatenate([uq[:, :, :QK_NOPE].reshape(Q_LORA, -1), uq[:, :, QK_NOPE:].reshape(Q_LORA, -1)], axis=1)
    ukv = p['mla_w_ukv'].reshape(KV_LORA, MLA_HEADS, QK_NOPE + V_HEAD)
    w_ukv = jnp.concatenate([ukv[:, :, :QK_NOPE].reshape(KV_LORA, -1), ukv[:, :, QK_NOPE:].reshape(KV_LORA, -1)], axis=1)

    w_router = jnp.zeros((D_MODEL, LANES), F32)
    w_router = w_router.at[:, :N_EXPERTS].set(p['w_expert_router'])
    w_router = w_router.at[:, N_EXPERTS:N_EXPERTS + N_GROUPS].set(p['w_group_router'])
    b_router = _pad_lanes(jnp.concatenate([p['b_expert_router'], p['b_group_router']]), 0)

    row = lambda v: v.astype(F32)[None, :]
    return {
        'ln_mix_w': row(p['ln_mix_w']),
        'w_qkv': part[0].astype(BF16), 'w_z': part[1].astype(BF16), 'w_ba': w_ba,
        'a_log': _pad_lanes(p['gdn_a_log'], GDN_HEADS), 'dt_bias': _pad_lanes(p['gdn_dt_bias'], GDN_HEADS),
        'w_cq': part[4].astype(BF16), 'w_ckv': part[5].astype(BF16), 'w_kpe': part[6].astype(BF16),
        'q_a_norm_w': row(p['mla_q_a_norm_w']), 'kv_a_norm_w': row(p['mla_kv_a_norm_w']),
        'w_uq': w_uq.astype(BF16), 'w_ukv': w_ukv.astype(BF16),
        'q_norm_w': row(jnp.concatenate([jnp.tile(p['mla_q_norm_w'][:QK_NOPE], MLA_HEADS),
                                         jnp.tile(p['mla_q_norm_w'][QK_NOPE:], MLA_HEADS)])),
        'q_head_sum': jnp.asarray(_Q_HEAD_SUM, BF16), 'rope_swap': jnp.asarray(_ROPE_SWAP, BF16),
        'k_norm_w': row(p['mla_k_norm_w']),
        'conv_w': p['gdn_conv_w'].astype(F32), 'gdn_norm_w': row(p['gdn_norm_w']),
        'w_out_gdn': p['w_out'][:GDN_V_DIM].astype(BF16), 'w_out_mla': p['w_out'][GDN_V_DIM:].astype(BF16),
        'ln_ffn_w': row(p['ln_ffn_w']), 'w_router': w_router, 'b_router': b_router,
        'w_exp_gate_up': jnp.concatenate([p['w_exp_gate'].astype(BF16), p['w_exp_up'].astype(BF16)], axis=-1),
        'w_exp_down': p['w_exp_down'].astype(BF16),
    }


def _tiles(bsz, seq, past):
    t = bsz * seq
    tm = min(1024, t)
    n_keys = past + seq
    if past == 0:
        tk = min(tm // TOKEN_SUBTILES, seq)
        tq = min(2 * tk, seq)
        lk = n_keys
    else:
        assert n_keys % CHUNK == 0
        tq = seq
        lk = -(-n_keys // LANES) * LANES
        tk = lk
    tb = min(256, seq)
    gseq = 2 if bsz % 2 == 0 else 1
    tmoe = min(1024, t)
    assert seq % tq == 0 and lk % tk == 0 and seq % tb == 0 and tb % CHUNK == 0 and t % tm == 0 and t % tmoe == 0
    return tm, tq, tk, lk, tb, gseq, tmoe


def _hybrid_layer(x, conv_hist, s0, ckv_past, kpe_past, p):
    bsz, seq, _ = x.shape
    past = ckv_past.shape[1]
    t = bsz * seq
    tm, tq, tk, lk, tb, gseq, tmoe = _tiles(bsz, seq, past)
    x2d = x.reshape(t, D_MODEL)

    rope_q = _tile_rows(_rope_tables(past + jnp.arange(seq, dtype=jnp.int32)), tm)
    n_keys = past + seq
    if past == 0:
        qkv, z, gates, q, ckv, kpe, k, vt = _inproj(x2d, p, rope_q, tm, key_tile=tk)
    else:
        assert tk == lk
        qkv, z, gates, q, ckv, kpe = _inproj(x2d, p, rope_q, tm)
        rope_k = _rope_tables(jnp.arange(lk, dtype=jnp.int32))
        k, vt = _kvprep(ckv_past.astype(F32), kpe_past.astype(F32), ckv.reshape(bsz, seq, KV_LORA),
                        kpe.reshape(bsz, seq, QK_ROPE), p, rope_k, lk)
    ckv3, kpe3 = ckv.reshape(bsz, seq, KV_LORA), kpe.reshape(bsz, seq, QK_ROPE)
    mla_out = _flash(q, k, vt, bsz, seq, lk, past, tq, tk, FLASH_HEADS_PER_STEP, tq // min(tq, tk))

    hist_pad = jnp.concatenate([jnp.zeros((bsz, 8 - (CONV_W - 1), GDN_CONV_DIM), F32), conv_hist.astype(F32)], axis=1)
    gdn_out, s_new = _gdn(qkv, z, gates, hist_pad, s0.astype(F32), p, bsz, seq, tb, gseq)
    assert seq >= CONV_W - 1
    conv_new = qkv.reshape(bsz, seq, GDN_CONV_DIM)[:, seq - (CONV_W - 1):]

    x1, h2, comb = _outproj(x2d, gdn_out, mla_out, p, tm)
    y = _moe(h2, comb, x1, p, tmoe)
    return y.reshape(bsz, seq, D_MODEL), ckv3, kpe3, s_new.astype(s0.dtype), conv_new


def kernel(x_prompt, x_sample, cache_mla_ckv, cache_mla_kpe, state_gdn, state_gdn_conv, ln_mix_w, w_in, gdn_conv_w, gdn_a_log, gdn_dt_bias, gdn_norm_w, mla_q_a_norm_w, mla_w_uq, mla_kv_a_norm_w, mla_w_ukv, mla_q_norm_w, mla_k_norm_w, w_out, ln_ffn_w, w_group_router, b_group_router, w_expert_router, b_expert_router, w_exp_gate, w_exp_up, w_exp_down):
    depth = w_in.shape[0]
    bsz = x_prompt.shape[0]
    y_p, y_s = x_prompt, x_sample
    outs_p, outs_s = [], []
    for l in range(depth):
        p = _prepare_params({
            'ln_mix_w': ln_mix_w[l], 'w_in': w_in[l], 'gdn_conv_w': gdn_conv_w[l],
            'gdn_a_log': gdn_a_log[l], 'gdn_dt_bias': gdn_dt_bias[l], 'gdn_norm_w': gdn_norm_w[l],
            'mla_q_a_norm_w': mla_q_a_norm_w[l], 'mla_w_uq': mla_w_uq[l],
            'mla_kv_a_norm_w': mla_kv_a_norm_w[l], 'mla_w_ukv': mla_w_ukv[l],
            'mla_q_norm_w': mla_q_norm_w[l], 'mla_k_norm_w': mla_k_norm_w[l],
            'w_out': w_out[l], 'ln_ffn_w': ln_ffn_w[l],
            'w_group_router': w_group_router[l], 'b_group_router': b_group_router[l],
            'w_expert_router': w_expert_router[l], 'b_expert_router': b_expert_router[l],
            'w_exp_gate': w_exp_gate[l], 'w_exp_up': w_exp_up[l], 'w_exp_down': w_exp_down[l],
        })
        y_p, *st_p = _hybrid_layer(
            y_p,
            jnp.zeros((bsz, CONV_W - 1, GDN_CONV_DIM), y_p.dtype),
            jnp.zeros((bsz, GDN_HEADS, GDN_DK, GDN_DV), state_gdn.dtype),
            jnp.zeros((bsz, 0, KV_LORA), cache_mla_ckv.dtype),
            jnp.zeros((bsz, 0, QK_ROPE), cache_mla_kpe.dtype),
            p)
        y_s, *st_s = _hybrid_layer(y_s, state_gdn_conv[l], state_gdn[l], cache_mla_ckv[l], cache_mla_kpe[l], p)
        outs_p.append(st_p)
        outs_s.append(st_s)
    stack = lambda outs, i: jnp.stack([o[i] for o in outs])
    return (y_p, y_s,
            stack(outs_p, 0), stack(outs_p, 1), stack(outs_p, 2), stack(outs_p, 3),
            stack(outs_s, 0), stack(outs_s, 1), stack(outs_s, 2), stack(outs_s, 3))
```

```python
import functools
import math

import jax
import jax.numpy as jnp
import numpy as np
from jax import lax
from jax.experimental import pallas as pl
from jax.experimental.pallas import tpu as pltpu

F32 = jnp.float32
BF16 = jnp.bfloat16

D_MODEL = 1024
CHUNK = 64
RMS_EPS = 1e-6

GDN_HEADS = 4
GDN_DK = 128
GDN_DV = 128
GDN_QK_DIM = GDN_HEADS * GDN_DK
GDN_V_DIM = GDN_HEADS * GDN_DV
GDN_CONV_DIM = 2 * GDN_QK_DIM + GDN_V_DIM
CONV_W = 4

MLA_HEADS = 4
Q_LORA = 512
KV_LORA = 256
QK_NOPE = 128
QK_ROPE = 64
ROPE_HALF = QK_ROPE // 2
V_HEAD = 128
V_ROWS = V_HEAD + 16
QK_HEAD = QK_NOPE + QK_ROPE
QK_PAD = QK_NOPE + 2 * QK_ROPE
ROPE_THETA = 10000.0

N_GROUPS = 4
EXPERTS_PER_GROUP = 8
N_EXPERTS = N_GROUPS * EXPERTS_PER_GROUP
D_EXPERT = 256

LANES = 128
VMEM_LIMIT_BYTES = 58 * 1024 * 1024

Q_SCALE = (QK_HEAD ** -0.5) * math.log2(math.e)
MASK_VALUE = -1e30
FLASH_HEADS_PER_STEP = 2
GDN_SOLVE_PASSES = 1
TOKEN_SUBTILES = 2
MOE_SEG_ALIGN = 16
MOE_ROW_BLOCK = 128
MOE_EXPERTS_PER_STEP = 4


def _mm(a, b):
    return jnp.dot(a.astype(BF16), b.astype(BF16), preferred_element_type=F32)


def _mm_nt(a, b):
    return lax.dot_general(a.astype(BF16), b.astype(BF16), (((1,), (1,)), ((), ())),
                           preferred_element_type=F32)


def _mm_tn(a, b):
    return lax.dot_general(a.astype(BF16), b.astype(BF16), (((0,), (0,)), ((), ())),
                           preferred_element_type=F32)


def _split(a):
    hi = a.astype(BF16)
    lo = (a - hi.astype(F32)).astype(BF16)
    return hi, lo


def _mm3(a, b):
    ah, al = _split(a)
    bh, bl = _split(b)
    return _mm(ah, bh) + (_mm(ah, bl) + _mm(al, bh))


def _mm_solve(a, b):
    return _mm3(a, b) if GDN_SOLVE_PASSES == 3 else _mm(a, b)


def _mm_exact_rhs(a, b01):
    a1 = a.astype(BF16)
    r1 = a - a1.astype(F32)
    a2 = r1.astype(BF16)
    a3 = (r1 - a2.astype(F32)).astype(BF16)
    return _mm(a1, b01) + (_mm(a2, b01) + _mm(a3, b01))


def _mm2_exact_rhs(a, b01):
    hi, lo = _split(a)
    return _mm(hi, b01) + _mm(lo, b01)


def _mm_exact_lhs(a01, b):
    b1 = b.astype(BF16)
    r1 = b - b1.astype(F32)
    b2 = r1.astype(BF16)
    b3 = (r1 - b2.astype(F32)).astype(BF16)
    return _mm(a01, b1) + (_mm(a01, b2) + _mm(a01, b3))


def _rms(x, w):
    return x * lax.rsqrt(jnp.mean(x * x, axis=-1, keepdims=True) + RMS_EPS) * w


def _silu(x):
    return x * jax.nn.sigmoid(x)


def _swap_halves(x):
    h = x.shape[-1] // 2
    return jnp.concatenate([x[:, h:], x[:, :h]], axis=-1)


def _subtiles(tm):
    n = tm // TOKEN_SUBTILES
    return [pl.ds(i * n, n) for i in range(TOKEN_SUBTILES)]


def _const_spec(shape):
    nd = len(shape)
    return pl.BlockSpec(shape, lambda *_: (0,) * nd, pipeline_mode=pl.Buffered(1))


def _params(*sem):
    return pltpu.CompilerParams(dimension_semantics=sem, vmem_limit_bytes=VMEM_LIMIT_BYTES)


def _inproj_kernel(x_ref, lnw_ref, wqkv_ref, wz_ref, wba_ref, alog_ref, dtb_ref, wcq_ref, qanw_ref,
                   wuq_ref, qnw_ref, hsum_ref, swap_ref, cos_ref, sin_ref, wckv_ref, kvanw_ref, wkpe_ref,
                   wukv_ref, knw_ref, qkv_ref, z_ref, gates_ref, q_ref, ckv_ref, kpe_ref, *kv_refs):
    subs = _subtiles(x_ref.shape[0])
    h_ = [_rms(x_ref[r, :], lnw_ref[...]) for r in subs]
    hb_ = [h.astype(BF16) for h in h_]
    raw_, ckv_, cq_, kpe_ = [], [], [], []
    for r, h, hb in zip(subs, h_, hb_):
        qkv_ref[r, :] = _mm(hb, wqkv_ref[...])
        z_ref[r, :] = _mm(hb, wz_ref[...])
        kpe_.append(_mm(hb, wkpe_ref[...]))
        kpe_ref[r, :] = kpe_[-1]
        raw_.append(_mm3(h, wba_ref[...]))
        ckv_.append(_mm(hb, wckv_ref[...]))
        cq_.append(_mm(hb, wcq_ref[...]))

    qall_ = [_mm(_rms(cq, qanw_ref[...]), wuq_ref[...]) for cq in cq_]

    for i, (r, raw, ckv) in enumerate(zip(subs, raw_, ckv_)):
        lane = lax.broadcasted_iota(jnp.int32, raw.shape, 1)
        pre = raw + dtb_ref[...]
        softplus = jnp.maximum(pre, 0.0) + jnp.log1p(jnp.exp(-jnp.abs(pre)))
        g = -jnp.exp(alog_ref[...]) * softplus
        gates_ref[r, :] = jnp.where(lane < GDN_HEADS, jax.nn.sigmoid(raw), g)
        ckv_n = _rms(ckv, kvanw_ref[...])
        ckv_ref[r, :] = ckv_n
        if kv_refs:
            k_ref, vt_ref = kv_refs
            heads = _keys_values(ckv_n, kpe_[i], cos_ref[r, :], sin_ref[r, :], wukv_ref[...], knw_ref[...])
            for hd, (kh, vt) in enumerate(heads):
                k_ref[hd, r, :] = kh
                vt_ref[hd, i] = vt

    rope0 = MLA_HEADS * QK_NOPE
    lane = lax.broadcasted_iota(jnp.int32, (1, 2 * QK_ROPE), 1)
    for r, qall in zip(subs, qall_):
        ssq = _mm(qall * qall, hsum_ref[...])
        qn = qall * (lax.rsqrt(ssq * (1.0 / QK_HEAD) + RMS_EPS) * Q_SCALE) * qnw_ref[...]
        rw = qn[:, rope0:]
        cos2 = jnp.concatenate([cos_ref[r, :]] * 2, axis=-1)
        sin2 = jnp.concatenate([sin_ref[r, :]] * 2, axis=-1)
        rot = rw * cos2 + _mm2_exact_rhs(rw, swap_ref[...]) * sin2
        for hd in range(MLA_HEADS):
            pair = rot[:, (hd // 2) * 2 * QK_ROPE:(hd // 2 + 1) * 2 * QK_ROPE]
            mine = jnp.where((lane // QK_ROPE) == (hd % 2), pair, 0.0)
            q_ref[hd, r, :] = jnp.concatenate([qn[:, hd * QK_NOPE:(hd + 1) * QK_NOPE], mine], axis=-1).astype(BF16)


def _inproj(x2d, p, rope_q, tm, key_tile=None):
    t = x2d.shape[0]
    nt = t // tm
    cos_t, sin_t = rope_q
    n_rope_blocks = cos_t.shape[0] // tm
    rope_spec = pl.BlockSpec((tm, 2 * QK_ROPE), lambda i: (i % n_rope_blocks, 0))
    row = lambda n: pl.BlockSpec((tm, n), lambda i: (i, 0))
    consts = [p['ln_mix_w'], p['w_qkv'], p['w_z'], p['w_ba'], p['a_log'], p['dt_bias'], p['w_cq'],
              p['q_a_norm_w'], p['w_uq'], p['q_norm_w'], p['q_head_sum'], p['rope_swap']]
    consts2 = [p['w_ckv'], p['kv_a_norm_w'], p['w_kpe'], p['w_ukv'], p['k_norm_w']]
    in_specs = ([row(D_MODEL)] + [_const_spec(c.shape) for c in consts] + [rope_spec, rope_spec]
                + [_const_spec(c.shape) for c in consts2])
    out_shape = (
        jax.ShapeDtypeStruct((t, GDN_CONV_DIM), F32),
        jax.ShapeDtypeStruct((t, GDN_V_DIM), F32),
        jax.ShapeDtypeStruct((t, LANES), F32),
        jax.ShapeDtypeStruct((MLA_HEADS, t, QK_PAD), BF16),
        jax.ShapeDtypeStruct((t, KV_LORA), F32),
        jax.ShapeDtypeStruct((t, QK_ROPE), F32),
    )
    out_specs = (row(GDN_CONV_DIM), row(GDN_V_DIM), row(LANES),
                 pl.BlockSpec((MLA_HEADS, tm, QK_PAD), lambda i: (0, i, 0)),
                 row(KV_LORA), row(QK_ROPE))
    if key_tile is not None:
        assert tm // TOKEN_SUBTILES == key_tile
        out_shape += (jax.ShapeDtypeStruct((MLA_HEADS, t, QK_PAD), BF16),
                      jax.ShapeDtypeStruct((MLA_HEADS, t // key_tile, V_ROWS, key_tile), BF16))
        out_specs += (pl.BlockSpec((MLA_HEADS, tm, QK_PAD), lambda i: (0, i, 0)),
                      pl.BlockSpec((MLA_HEADS, TOKEN_SUBTILES, V_ROWS, key_tile), lambda i: (0, i, 0, 0)))
    return pl.pallas_call(
        _inproj_kernel, grid=(nt,), in_specs=in_specs, out_specs=out_specs, out_shape=out_shape,
        compiler_params=_params("parallel"), name="inproj",
    )(x2d, *consts, cos_t, sin_t, *consts2)


def _keys_values(ckv, kpe, cosf, sinf, wukv, knw):
    kv = _mm(ckv, wukv)
    kw = kpe * knw[:, QK_NOPE:]
    krot = kw * cosf[:, :QK_ROPE] + _swap_halves(kw) * sinf[:, :QK_ROPE]
    krot = jnp.concatenate([krot, krot], axis=-1)
    s_kpe = jnp.sum(kpe * kpe, axis=-1, keepdims=True)
    v0 = MLA_HEADS * QK_NOPE
    pad_row = lax.broadcasted_iota(jnp.int32, (V_ROWS - V_HEAD, kv.shape[0]), 0)
    ones_rows = jnp.where(pad_row == 0, 1.0, 0.0)
    out = []
    for hd in range(MLA_HEADS):
        kn = kv[:, hd * QK_NOPE:(hd + 1) * QK_NOPE]
        inv = lax.rsqrt((jnp.sum(kn * kn, axis=-1, keepdims=True) + s_kpe) * (1.0 / QK_HEAD) + RMS_EPS)
        kh = jnp.concatenate([kn * knw[:, :QK_NOPE], krot], axis=-1) * inv
        vt = kv[:, v0 + hd * V_HEAD: v0 + (hd + 1) * V_HEAD].T
        out.append((kh.astype(BF16), jnp.concatenate([vt, ones_rows], axis=0).astype(BF16)))
    return out


def _kvprep_kernel(ckv_old_ref, kpe_old_ref, ckv_new_ref, kpe_new_ref, cos_ref, sin_ref, wukv_ref, knw_ref,
                   k_ref, vt_ref):
    n_pad = k_ref.shape[1] - ckv_old_ref.shape[1] - ckv_new_ref.shape[1]

    def rows(old_ref, new_ref):
        return jnp.concatenate([old_ref[0], new_ref[0], jnp.zeros((n_pad, old_ref.shape[2]), F32)], axis=0)

    heads = _keys_values(rows(ckv_old_ref, ckv_new_ref), rows(kpe_old_ref, kpe_new_ref),
                         cos_ref[...], sin_ref[...], wukv_ref[...], knw_ref[...])
    for hd, (kh, vt) in enumerate(heads):
        k_ref[hd] = kh
        vt_ref[hd, 0] = vt


def _kvprep(ckv_old, kpe_old, ckv_new, kpe_new, p, rope_k, lk):
    bsz = ckv_old.shape[0]
    cos_t, sin_t = rope_k
    per_seq = lambda a: pl.BlockSpec((1,) + a.shape[1:], lambda b: (b, 0, 0))
    return pl.pallas_call(
        _kvprep_kernel, grid=(bsz,),
        in_specs=[per_seq(ckv_old), per_seq(kpe_old), per_seq(ckv_new), per_seq(kpe_new),
                  _const_spec(cos_t.shape), _const_spec(sin_t.shape),
                  _const_spec(p['w_ukv'].shape), _const_spec(p['k_norm_w'].shape)],
        out_specs=(pl.BlockSpec((MLA_HEADS, lk, QK_PAD), lambda b: (0, b, 0)),
                   pl.BlockSpec((MLA_HEADS, 1, V_ROWS, lk), lambda b: (0, b, 0, 0))),
        out_shape=(jax.ShapeDtypeStruct((MLA_HEADS, bsz * lk, QK_PAD), BF16),
                   jax.ShapeDtypeStruct((MLA_HEADS, bsz, V_ROWS, lk), BF16)),
        compiler_params=_params("parallel"), name="kvprep",
    )(ckv_old, kpe_old, ckv_new, kpe_new, cos_t, sin_t, p['w_ukv'], p['k_norm_w'])


def _flash_kernel(q_ref, k_ref, vt_ref, o_ref, m_sc, acc_sc, *, tq, tk, nk, past, hps, qsplit):
    i = pl.program_id(2)
    q_first = past + i * tq
    q_last = q_first + tq - 1
    n_full = jnp.minimum(nk, ((q_first // CHUNK + 1) * CHUNK) // tk)
    n_vis = jnp.minimum(nk, ((q_last // CHUNK + 1) * CHUNK + tk - 1) // tk)

    m_sc[...] = jnp.full(m_sc.shape, MASK_VALUE, F32)
    acc_sc[...] = jnp.zeros(acc_sc.shape, F32)

    tqs = tq // qsplit
    streams = [(hd, pl.ds(c * tqs, tqs)) for hd in range(hps) for c in range(qsplit)]

    def step(j, masked):
        k0 = pl.multiple_of(j * tk, tk)
        s_ = [_mm_nt(k_ref[hd, pl.ds(k0, tk), :], q_ref[hd, cols, :]) for hd, cols in streams]
        if masked:
            kpos = k0 + lax.broadcasted_iota(jnp.int32, (tk, 1), 0)
            for n, (hd, cols) in enumerate(streams):
                qpos = q_first + (n % qsplit) * tqs + lax.broadcasted_iota(jnp.int32, (1, tqs), 1)
                s_[n] = jnp.where((kpos // CHUNK) <= (qpos // CHUNK), s_[n], MASK_VALUE)
        p_, alpha_ = [], []
        for n, (hd, cols) in enumerate(streams):
            m_old = m_sc[hd, :, cols]
            m_new = jnp.maximum(m_old, jnp.max(s_[n], axis=0, keepdims=True))
            alpha_.append(jnp.exp2(m_old - m_new))
            p_.append(jnp.exp2(s_[n] - m_new).astype(BF16))
            m_sc[hd, :, cols] = m_new
        for n, (hd, cols) in enumerate(streams):
            acc_sc[hd, :, cols] = alpha_[n] * acc_sc[hd, :, cols] + _mm(vt_ref[hd, j], p_[n])

    def full_body(j, c):
        step(j, False)
        return c

    def masked_body(j, c):
        step(j, True)
        return c

    lax.fori_loop(0, n_full, full_body, 0)
    lax.fori_loop(n_full, n_vis, masked_body, 0)
    for hd in range(hps):
        o_t = acc_sc[hd, :V_HEAD, :] / acc_sc[hd, V_HEAD:V_HEAD + 1, :]
        o_ref[:, hd * V_HEAD:(hd + 1) * V_HEAD] = o_t.T.astype(o_ref.dtype)


def _flash(q, k, vt, bsz, lq, lk, past, tq, tk, hps, qsplit):
    nq = lq // tq
    nk = lk // tk
    kern = functools.partial(_flash_kernel, tq=tq, tk=tk, nk=nk, past=past, hps=hps, qsplit=qsplit)
    once = pl.Buffered(1 if nq > 1 else 2)
    return pl.pallas_call(
        kern, grid=(bsz, MLA_HEADS // hps, nq),
        in_specs=[pl.BlockSpec((hps, tq, QK_PAD), lambda b, h, i: (h, b * nq + i, 0)),
                  pl.BlockSpec((hps, lk, QK_PAD), lambda b, h, i: (h, b, 0), pipeline_mode=once),
                  pl.BlockSpec((hps, nk, V_ROWS, tk), lambda b, h, i: (h, b, 0, 0), pipeline_mode=once)],
        out_specs=pl.BlockSpec((tq, hps * V_HEAD), lambda b, h, i: (b * nq + i, h)),
        out_shape=jax.ShapeDtypeStruct((bsz * lq, MLA_HEADS * V_HEAD), BF16),
        scratch_shapes=[pltpu.VMEM((hps, 1, tq), F32), pltpu.VMEM((hps, V_ROWS, tq), F32)],
        compiler_params=_params("parallel", "parallel", "arbitrary"), name="flash",
    )(q, k, vt)


def _gdn_kernel(qkv_ref, z_ref, gates_ref, hist_ref, s0_ref, convw_ref, normw_ref,
                o_ref, s_ref, xbuf, *, tb, nseq):
    j = pl.program_id(1)
    nc = tb // CHUNK
    pad = 8
    seqs = range(nseq)
    streams = [(b, hd) for b in seqs for hd in range(GDN_HEADS)]

    @pl.when(j == 0)
    def _():
        xbuf[:, 0:pad, :] = hist_ref[...]
        s_ref[...] = s0_ref[...]

    cw = convw_ref[...]
    y_ = []
    for b in seqs:
        xbuf[b, pad:pad + tb, :] = qkv_ref[b]
        y = xbuf[b, pad - 3:pad - 3 + tb, :] * cw[0:1, :]
        for i in range(1, CONV_W):
            y = y + xbuf[b, pad - 3 + i:pad - 3 + i + tb, :] * cw[i:i + 1, :]
        xbuf[b, 0:pad, :] = xbuf[b, tb:tb + pad, :]
        y_.append(_silu(y))

    row = lax.broadcasted_iota(jnp.int32, (tb, tb), 0)
    col = lax.broadcasted_iota(jnp.int32, (tb, tb), 1)
    same = (row // CHUNK) == (col // CHUNK)
    incl = same & (row >= col)
    strict = same & (row > col)
    tril01 = jnp.where(incl, 1.0, 0.0).astype(BF16)
    triu01 = jnp.where(same & (row <= col), 1.0, 0.0).astype(BF16)
    eye = jnp.where(row == col, 1.0, 0.0)
    gates_ = [gates_ref[b] for b in seqs]
    gc_cols_ = [_mm_exact_lhs(tril01, g) for g in gates_]
    gc_rows_ = [_mm_exact_rhs(g.T[0:8, :], triu01) for g in gates_]

    q_, k_, v_, beta_, gc_, decay_, kb_ = [], [], [], [], [], [], []
    for n, (b, hd) in enumerate(streams):
        y = y_[b]
        qh = y[:, hd * GDN_DK:(hd + 1) * GDN_DK]
        kh = y[:, GDN_QK_DIM + hd * GDN_DK: GDN_QK_DIM + (hd + 1) * GDN_DK]
        q_.append(qh * (lax.rsqrt(jnp.sum(qh * qh, axis=-1, keepdims=True) + RMS_EPS) * (GDN_DK ** -0.5)))
        k_.append(kh * lax.rsqrt(jnp.sum(kh * kh, axis=-1, keepdims=True) + RMS_EPS))
        v_.append(y[:, 2 * GDN_QK_DIM + hd * GDN_DV: 2 * GDN_QK_DIM + (hd + 1) * GDN_DV])
        beta_.append(gates_[b][:, hd:hd + 1])
        gc = gc_cols_[b][:, GDN_HEADS + hd: GDN_HEADS + hd + 1]
        gr = gc_rows_[b][GDN_HEADS + hd: GDN_HEADS + hd + 1, :]
        gc_.append(gc)
        decay_.append(jnp.where(incl, jnp.exp(jnp.where(incl, gc - gr, 0.0)), 0.0))
        kb_.append(k_[n] * beta_[n])
    ns = range(len(streams))
    khb_ = [k.astype(BF16) for k in k_]
    pw_ = [jnp.where(strict, -(_mm_nt(kb_[n], khb_[n]) * decay_[n]), 0.0) for n in ns]
    attn_ = [_mm_nt(q_[n], khb_[n]) * decay_[n] for n in ns]

    tinv_ = [eye + pw for pw in pw_]
    for _ in range(int(math.log2(CHUNK)) - 1):
        pw_ = [_mm_solve(pw, pw) for pw in pw_]
        tinv_ = [tinv_[n] + _mm_solve(tinv_[n], pw_[n]) for n in ns]

    eg_ = [jnp.exp(gc) for gc in gc_]
    sol_ = [_mm_solve(tinv_[n], jnp.concatenate([v_[n] * beta_[n], kb_[n] * eg_[n]], axis=-1)) for n in ns]
    qdec_ = [q_[n] * eg_[n] for n in ns]

    s_ = [s_ref[b, hd] for b, hd in streams]
    us_ = [[] for _ in ns]
    oi_ = [[] for _ in ns]
    for c in range(nc):
        r0, r1 = c * CHUNK, (c + 1) * CHUNK
        for n in ns:
            g_end = gc_[n][r1 - 1:r1, :]
            k_end = k_[n][r0:r1] * jnp.exp(g_end - gc_[n][r0:r1])
            both = _mm(jnp.concatenate([sol_[n][r0:r1, GDN_DV:], qdec_[n][r0:r1]], axis=0), s_[n])
            u = sol_[n][r0:r1, :GDN_DV] - both[:CHUNK]
            oi_[n].append(both[CHUNK:])
            us_[n].append(u)
            s_[n] = s_[n] * jnp.exp(g_end) + _mm_tn(k_end, u)
    for n, (b, hd) in enumerate(streams):
        s_ref[b, hd] = s_[n]
        u_all = us_[n][0] if nc == 1 else jnp.concatenate(us_[n], axis=0)
        o_all = oi_[n][0] if nc == 1 else jnp.concatenate(oi_[n], axis=0)
        o = o_all + _mm(attn_[n], u_all)
        o = _rms(o, normw_ref[...]) * _silu(z_ref[b, :, hd * GDN_DV:(hd + 1) * GDN_DV])
        o_ref[b, :, hd * GDN_DV:(hd + 1) * GDN_DV] = o.astype(o_ref.dtype)


def _gdn(qkv2d, z2d, gates2d, hist_pad, s0, p, bsz, seq, tb, nseq):
    nb = seq // tb
    kern = functools.partial(_gdn_kernel, tb=tb, nseq=nseq)
    row = lambda n: pl.BlockSpec((nseq, tb, n), lambda g, j: (g, j, 0))
    state_spec = pl.BlockSpec((nseq, GDN_HEADS, GDN_DK, GDN_DV), lambda g, j: (g, 0, 0, 0))
    view = lambda a: a.reshape(bsz, seq, a.shape[-1])
    out, s_new = pl.pallas_call(
        kern, grid=(bsz // nseq, nb),
        in_specs=[row(GDN_CONV_DIM), row(GDN_V_DIM), row(LANES),
                  pl.BlockSpec((nseq, 8, GDN_CONV_DIM), lambda g, j: (g, 0, 0)),
                  state_spec, _const_spec(p['conv_w'].shape), _const_spec(p['gdn_norm_w'].shape)],
        out_specs=(row(GDN_V_DIM), state_spec),
        out_shape=(jax.ShapeDtypeStruct((bsz, seq, GDN_V_DIM), BF16),
                   jax.ShapeDtypeStruct((bsz, GDN_HEADS, GDN_DK, GDN_DV), F32)),
        scratch_shapes=[pltpu.VMEM((nseq, tb + 8, GDN_CONV_DIM), F32)],
        compiler_params=_params("parallel", "arbitrary"), name="gdn",
    )(view(qkv2d), view(z2d), view(gates2d), hist_pad, s0, p['conv_w'], p['gdn_norm_w'])
    return out.reshape(bsz * seq, GDN_V_DIM), s_new


def _outproj_kernel(x_ref, g_ref, a_ref, wog_ref, woa_ref, lnw_ref, wr_ref, br_ref,
                    x1_ref, h2_ref, comb_ref):
    subs = _subtiles(x_ref.shape[0])
    h2_ = []
    for r in subs:
        x1 = x_ref[r, :] + _mm(g_ref[r, :], wog_ref[...]) + _mm(a_ref[r, :], woa_ref[...])
        x1_ref[r, :] = x1
        h2_.append(_rms(x1, lnw_ref[...]))
        h2_ref[r, :] = h2_[-1].astype(BF16)
    logits_ = [_mm3(h2, wr_ref[...]) + br_ref[...] for h2 in h2_]
    for r, logits in zip(subs, logits_):
        comb_ref[r, :] = _route(logits)


def _route(logits):
    lane = lax.broadcasted_iota(jnp.int32, logits.shape, 1)
    big = jnp.int32(LANES)
    is_grp = (lane >= N_EXPERTS) & (lane < N_EXPERTS + N_GROUPS)
    gl = jnp.where(is_grp, logits, -jnp.inf)
    gmax = jnp.max(gl, axis=-1, keepdims=True)
    grp_p = 1.0 / jnp.sum(jnp.exp(gl - gmax), axis=-1, keepdims=True)
    grp_idx = jnp.min(jnp.where(gl == gmax, lane - N_EXPERTS, big), axis=-1, keepdims=True)

    in_grp = (lane < N_EXPERTS) & ((lane // EXPERTS_PER_GROUP) == grp_idx)
    el = jnp.where(in_grp, logits, -jnp.inf)
    m1 = jnp.max(el, axis=-1, keepdims=True)
    i1 = jnp.min(jnp.where(el == m1, lane, big), axis=-1, keepdims=True)
    el2 = jnp.where(lane == i1, -jnp.inf, el)
    m2 = jnp.max(el2, axis=-1, keepdims=True)
    i2 = jnp.min(jnp.where(el2 == m2, lane, big), axis=-1, keepdims=True)
    e2 = jnp.exp(m2 - m1)
    w1 = 1.0 / (1.0 + e2)
    w2 = e2 * w1
    return jnp.where(lane == 0, i1.astype(F32), jnp.where(lane == 1, i2.astype(F32),
                     jnp.where(lane == 2, grp_p * w1, jnp.where(lane == 3, grp_p * w2, 0.0))))


def _outproj(x2d, gdn_out, mla_out, p, tm):
    t = x2d.shape[0]
    row = lambda n: pl.BlockSpec((tm, n), lambda i: (i, 0))
    consts = [p['w_out_gdn'], p['w_out_mla'], p['ln_ffn_w'], p['w_router'], p['b_router']]
    return pl.pallas_call(
        _outproj_kernel, grid=(t // tm,),
        in_specs=[row(D_MODEL), row(GDN_V_DIM), row(MLA_HEADS * V_HEAD)] + [_const_spec(c.shape) for c in consts],
        out_specs=(row(D_MODEL), row(D_MODEL), row(LANES)),
        out_shape=(jax.ShapeDtypeStruct((t, D_MODEL), F32), jax.ShapeDtypeStruct((t, D_MODEL), BF16),
                   jax.ShapeDtypeStruct((t, LANES), F32)),
        compiler_params=_params("parallel"), name="outproj",
    )(x2d, gdn_out, mla_out, *consts)


def _moe_slots(tm):
    n = 2 * tm + N_EXPERTS * MOE_SEG_ALIGN + MOE_ROW_BLOCK
    return -(-n // LANES) * LANES


def _moe_kernel(h_ref, route_ref, x1_ref, wg_ref, wu_ref, wd_ref, y_ref, xs_sc, ys_sc, dcol_sc, offs_ref,
                *, tm, slots, n_chunks):
    g = pl.program_id(1)
    chunk = slots // n_chunks
    lb = min(2 * LANES, tm)
    nlb = tm // lb

    @pl.when(g == 0)
    def _sort_picks():
        rt = route_ref[...].T
        eid = lax.broadcasted_iota(jnp.int32, (N_EXPERTS, lb), 0).astype(F32)
        pieces = [jnp.where(rt[k:k + 1, b * lb:(b + 1) * lb] == eid, 1.0, 0.0)
                  for k in range(2) for b in range(nlb)]
        ind = jnp.concatenate(pieces, axis=0)
        r_i = lax.broadcasted_iota(jnp.int32, (lb, lb), 0)
        c_i = lax.broadcasted_iota(jnp.int32, (lb, lb), 1)
        before = _mm(ind, jnp.where(r_i < c_i, 1.0, 0.0))
        tot = jnp.sum(ind, axis=1, keepdims=True)
        run = jnp.zeros((N_EXPERTS, 1), F32)
        runs = []
        for i in range(len(pieces)):
            runs.append(run)
            run = run + tot[i * N_EXPERTS:(i + 1) * N_EXPERTS]
        seg = jnp.floor((run + (MOE_SEG_ALIGN - 1)) * (1.0 / MOE_SEG_ALIGN)) * MOE_SEG_ALIGN
        er = lax.broadcasted_iota(jnp.int32, (N_EXPERTS, N_EXPERTS), 0)
        ec = lax.broadcasted_iota(jnp.int32, (N_EXPERTS, N_EXPERTS), 1)
        start = _mm_exact_lhs(jnp.where(er > ec, 1.0, 0.0).astype(BF16),
                              jnp.broadcast_to(seg, (N_EXPERTS, LANES)))[:, 0:1]
        start_i = start.astype(jnp.int32)
        for e in range(N_EXPERTS):
            offs_ref[e] = start_i[e, 0]
        offs_ref[N_EXPERTS] = start_i[N_EXPERTS - 1, 0] + seg.astype(jnp.int32)[N_EXPERTS - 1, 0]

        dest = []
        for k in range(2):
            cols = []
            for b in range(nlb):
                i = k * nlb + b
                piece = ind[i * N_EXPERTS:(i + 1) * N_EXPERTS]
                pos = start + runs[i] + before[i * N_EXPERTS:(i + 1) * N_EXPERTS]
                cols.append(jnp.sum(piece * pos, axis=0, keepdims=True))
            dest.append(cols[0] if nlb == 1 else jnp.concatenate(cols, axis=1))
        rec = jnp.concatenate([dest[0], dest[1], rt[2:4], jnp.zeros((LANES - 4, tm), F32)], axis=0)
        dcol_sc[...] = rec.T

        h = h_ref[...]
        for c in range(n_chunks):
            s_i = (c * chunk + lax.broadcasted_iota(jnp.int32, (chunk, 1), 0)).astype(F32)
            sel = jnp.where(s_i == dest[0], 1.0, jnp.where(s_i == dest[1], 1.0, 0.0))
            xs_sc[c * chunk:(c + 1) * chunk, :] = _mm(sel, h).astype(BF16)
        ys_sc[...] = jnp.zeros(ys_sc.shape, BF16)

    steps = range(MOE_EXPERTS_PER_STEP)
    offs = [offs_ref[g * MOE_EXPERTS_PER_STEP + q] for q in steps] + [offs_ref[(g + 1) * MOE_EXPERTS_PER_STEP]]

    def rows(q, b):
        return pl.ds(pl.multiple_of(offs[q] + b * MOE_ROW_BLOCK, MOE_SEG_ALIGN), MOE_ROW_BLOCK)

    for q in steps:
        n_blocks = (offs[q + 1] - offs[q] + (MOE_ROW_BLOCK - 1)) // MOE_ROW_BLOCK

        def block(b, carry, q=q):
            xb = xs_sc[rows(q, b), :]
            act = _silu(_mm(xb, wg_ref[q])) * _mm(xb, wu_ref[q])
            ys_sc[rows(q, b), :] = _mm(act, wd_ref[q]).astype(BF16)
            return carry

        lax.fori_loop(1, n_blocks, block, 0)

    xb_ = [xs_sc[rows(q, 0), :] for q in steps]
    gate_ = [_mm(xb_[q], wg_ref[q]) for q in steps]
    up_ = [_mm(xb_[q], wu_ref[q]) for q in steps]
    act_ = [_silu(gate_[q]) * up_[q] for q in steps]
    out_ = [_mm(act_[q], wd_ref[q]).astype(BF16) for q in steps]
    for q in steps:
        ys_sc[rows(q, 0), :] = out_[q]

    @pl.when(g == pl.num_programs(1) - 1)
    def _combine():
        rec = dcol_sc[...]
        d0, d1, c0, c1 = rec[:, 0:1], rec[:, 1:2], rec[:, 2:3], rec[:, 3:4]
        y = x1_ref[...]
        for c in range(n_chunks):
            s_i = (c * chunk + lax.broadcasted_iota(jnp.int32, (1, chunk), 1)).astype(F32)
            wsel = jnp.where(s_i == d0, c0, jnp.where(s_i == d1, c1, 0.0))
            y = y + _mm(wsel, ys_sc[c * chunk:(c + 1) * chunk, :])
        y_ref[...] = y


def _moe(h2, route, x1, p, tm):
    t = h2.shape[0]
    slots = _moe_slots(tm)
    n_chunks = 3 if slots % (3 * LANES) == 0 else 1
    kern = functools.partial(_moe_kernel, tm=tm, slots=slots, n_chunks=n_chunks)
    row = lambda n: pl.BlockSpec((tm, n), lambda i, g: (i, 0))
    eps = MOE_EXPERTS_PER_STEP
    return pl.pallas_call(
        kern, grid=(t // tm, N_EXPERTS // eps),
        in_specs=[row(D_MODEL), row(LANES), row(D_MODEL),
                  pl.BlockSpec((eps, D_MODEL, D_EXPERT), lambda i, g: (g, 0, 0)),
                  pl.BlockSpec((eps, D_MODEL, D_EXPERT), lambda i, g: (g, 0, 0)),
                  pl.BlockSpec((eps, D_EXPERT, D_MODEL), lambda i, g: (g, 0, 0))],
        out_specs=pl.BlockSpec((tm, D_MODEL), lambda i, g: (i, 0)),
        out_shape=jax.ShapeDtypeStruct((t, D_MODEL), F32),
        scratch_shapes=[pltpu.VMEM((slots, D_MODEL), BF16), pltpu.VMEM((slots, D_MODEL), BF16),
                        pltpu.VMEM((tm, LANES), F32), pltpu.SMEM((N_EXPERTS + 1,), jnp.int32)],
        compiler_params=_params("parallel", "arbitrary"), name="moe",
    )(h2, route, x1, p['w_exp_gate'], p['w_exp_up'], p['w_exp_down'])


def _rope_tables(pos):
    inv = ROPE_THETA ** (-jnp.arange(ROPE_HALF, dtype=F32) / ROPE_HALF)
    ang = pos.astype(F32)[:, None] * inv[None, :]
    cos, sin = jnp.cos(ang), jnp.sin(ang)
    return jnp.concatenate([cos, cos] * 2, axis=-1), jnp.concatenate([-sin, sin] * 2, axis=-1)


def _tile_rows(tables, tm):
    n = tables[0].shape[0]
    if n >= tm:
        assert n % tm == 0
        return tables
    assert tm % n == 0
    return tuple(jnp.tile(t, (tm // n, 1)) for t in tables)


def _pad_lanes(v, offset):
    return jnp.zeros((1, LANES), F32).at[0, offset:offset + v.shape[0]].set(v.astype(F32))


def _q_layout_constants():
    n_nope = MLA_HEADS * QK_NOPE
    col = np.arange(n_nope + MLA_HEADS * QK_ROPE)
    head = np.where(col < n_nope, col // QK_NOPE, (col - n_nope) // QK_ROPE)
    same_head = (head[:, None] == head[None, :]).astype(np.float32)
    j = np.arange(MLA_HEADS * QK_ROPE)
    src = (j // QK_ROPE) * QK_ROPE + (j % QK_ROPE + ROPE_HALF) % QK_ROPE
    swap = (j[:, None] == src[None, :]).astype(np.float32)
    return same_head, swap


_Q_HEAD_SUM, _ROPE_SWAP = _q_layout_constants()


def _prepare_params(p):
    w_in = p['w_in']
    sizes = (GDN_CONV_DIM, GDN_V_DIM, GDN_HEADS, GDN_HEADS, Q_LORA, KV_LORA, QK_ROPE)
    offs = np.concatenate([[0], np.cumsum(sizes)])
    part = [w_in[:, int(offs[i]):int(offs[i + 1])] for i in range(len(sizes))]
    w_ba = jnp.zeros((D_MODEL, LANES), F32).at[:, :2 * GDN_HEADS].set(jnp.concatenate([part[2], part[3]], axis=1))

    uq = p['mla_w_uq'].reshape(Q_LORA, MLA_HEADS, QK_HEAD)
    w_uq = jnp.concatenate([uq[:, :, :QK_NOPE].reshape(Q_LORA, -1), uq[:, :, QK_NOPE:].reshape(Q_LORA, -1)], axis=1)
    ukv = p['mla_w_ukv'].reshape(KV_LORA, MLA_HEADS, QK_NOPE + V_HEAD)
    w_ukv = jnp.concatenate([ukv[:, :, :QK_NOPE].reshape(KV_LORA, -1), ukv[:, :, QK_NOPE:].reshape(KV_LORA, -1)], axis=1)

    w_router = jnp.zeros((D_MODEL, LANES), F32)
    w_router = w_router.at[:, :N_EXPERTS].set(p['w_expert_router'])
    w_router = w_router.at[:, N_EXPERTS:N_EXPERTS + N_GROUPS].set(p['w_group_router'])
    b_router = _pad_lanes(jnp.concatenate([p['b_expert_router'], p['b_group_router']]), 0)

    row = lambda v: v.astype(F32)[None, :]
    return {
        'ln_mix_w': row(p['ln_mix_w']),
        'w_qkv': part[0].astype(BF16), 'w_z': part[1].astype(BF16), 'w_ba': w_ba,
        'a_log': _pad_lanes(p['gdn_a_log'], GDN_HEADS), 'dt_bias': _pad_lanes(p['gdn_dt_bias'], GDN_HEADS),
        'w_cq': part[4].astype(BF16), 'w_ckv': part[5].astype(BF16), 'w_kpe': part[6].astype(BF16),
        'q_a_norm_w': row(p['mla_q_a_norm_w']), 'kv_a_norm_w': row(p['mla_kv_a_norm_w']),
        'w_uq': w_uq.astype(BF16), 'w_ukv': w_ukv.astype(BF16),
        'q_norm_w': row(jnp.concatenate([jnp.tile(p['mla_q_norm_w'][:QK_NOPE], MLA_HEADS),
                                         jnp.tile(p['mla_q_norm_w'][QK_NOPE:], MLA_HEADS)])),
        'q_head_sum': jnp.asarray(_Q_HEAD_SUM, BF16), 'rope_swap': jnp.asarray(_ROPE_SWAP, BF16),
        'k_norm_w': row(p['mla_k_norm_w']),
        'conv_w': p['gdn_conv_w'].astype(F32), 'gdn_norm_w': row(p['gdn_norm_w']),
        'w_out_gdn': p['w_out'][:GDN_V_DIM].astype(BF16), 'w_out_mla': p['w_out'][GDN_V_DIM:].astype(BF16),
        'ln_ffn_w': row(p['ln_ffn_w']), 'w_router': w_router, 'b_router': b_router,
        'w_exp_gate': p['w_exp_gate'].astype(BF16), 'w_exp_up': p['w_exp_up'].astype(BF16),
        'w_exp_down': p['w_exp_down'].astype(BF16),
    }


def _tiles(bsz, seq, past):
    t = bsz * seq
    tm = min(1024, t)
    n_keys = past + seq
    if past == 0:
        tk = min(tm // TOKEN_SUBTILES, seq)
        tq = min(2 * tk, seq)
        lk = n_keys
    else:
        assert n_keys % CHUNK == 0
        tq = seq
        lk = -(-n_keys // LANES) * LANES
        tk = lk
    tb = min(256, seq)
    gseq = 2 if bsz % 2 == 0 else 1
    tmoe = min(1024, t)
    assert seq % tq == 0 and lk % tk == 0 and seq % tb == 0 and tb % CHUNK == 0 and t % tm == 0 and t % tmoe == 0
    return tm, tq, tk, lk, tb, gseq, tmoe


def _hybrid_layer(x, conv_hist, s0, ckv_past, kpe_past, p):
    bsz, seq, _ = x.shape
    past = ckv_past.shape[1]
    t = bsz * seq
    tm, tq, tk, lk, tb, gseq, tmoe = _tiles(bsz, seq, past)
    x2d = x.reshape(t, D_MODEL)

    rope_q = _tile_rows(_rope_tables(past + jnp.arange(seq, dtype=jnp.int32)), tm)
    n_keys = past + seq
    if past == 0:
        qkv, z, gates, q, ckv, kpe, k, vt = _inproj(x2d, p, rope_q, tm, key_tile=tk)
    else:
        assert tk == lk
        qkv, z, gates, q, ckv, kpe = _inproj(x2d, p, rope_q, tm)
        rope_k = _rope_tables(jnp.arange(lk, dtype=jnp.int32))
        k, vt = _kvprep(ckv_past.astype(F32), kpe_past.astype(F32), ckv.reshape(bsz, seq, KV_LORA),
                        kpe.reshape(bsz, seq, QK_ROPE), p, rope_k, lk)
    ckv3, kpe3 = ckv.reshape(bsz, seq, KV_LORA), kpe.reshape(bsz, seq, QK_ROPE)
    mla_out = _flash(q, k, vt, bsz, seq, lk, past, tq, tk, FLASH_HEADS_PER_STEP, tq // min(tq, tk))

    hist_pad = jnp.concatenate([jnp.zeros((bsz, 8 - (CONV_W - 1), GDN_CONV_DIM), F32), conv_hist.astype(F32)], axis=1)
    gdn_out, s_new = _gdn(qkv, z, gates, hist_pad, s0.astype(F32), p, bsz, seq, tb, gseq)
    assert seq >= CONV_W - 1
    conv_new = qkv.reshape(bsz, seq, GDN_CONV_DIM)[:, seq - (CONV_W - 1):]

    x1, h2, comb = _outproj(x2d, gdn_out, mla_out, p, tm)
    y = _moe(h2, comb, x1, p, tmoe)
    return y.reshape(bsz, seq, D_MODEL), ckv3, kpe3, s_new.astype(s0.dtype), conv_new


def kernel(x_prompt, x_sample, cache_mla_ckv, cache_mla_kpe, state_gdn, state_gdn_conv, ln_mix_w, w_in, gdn_conv_w, gdn_a_log, gdn_dt_bias, gdn_norm_w, mla_q_a_norm_w, mla_w_uq, mla_kv_a_norm_w, mla_w_ukv, mla_q_norm_w, mla_k_norm_w, w_out, ln_ffn_w, w_group_router, b_group_router, w_expert_router, b_expert_router, w_exp_gate, w_exp_up, w_exp_down):
    depth = w_in.shape[0]
    bsz = x_prompt.shape[0]
    y_p, y_s = x_prompt, x_sample
    outs_p, outs_s = [], []
    for l in range(depth):
        p = _prepare_params({
            'ln_mix_w': ln_mix_w[l], 'w_in': w_in[l], 'gdn_conv_w': gdn_conv_w[l],
            'gdn_a_log': gdn_a_log[l], 'gdn_dt_bias': gdn_dt_bias[l], 'gdn_norm_w': gdn_norm_w[l],
            'mla_q_a_norm_w': mla_q_a_norm_w[l], 'mla_w_uq': mla_w_uq[l],
            'mla_kv_a_norm_w': mla_kv_a_norm_w[l], 'mla_w_ukv': mla_w_ukv[l],
            'mla_q_norm_w': mla_q_norm_w[l], 'mla_k_norm_w': mla_k_norm_w[l],
            'w_out': w_out[l], 'ln_ffn_w': ln_ffn_w[l],
            'w_group_router': w_group_router[l], 'b_group_router': b_group_router[l],
            'w_expert_router': w_expert_router[l], 'b_expert_router': b_expert_router[l],
            'w_exp_gate': w_exp_gate[l], 'w_exp_up': w_exp_up[l], 'w_exp_down': w_exp_down[l],
        })
        y_p, *st_p = _hybrid_layer(
            y_p,
            jnp.zeros((bsz, CONV_W - 1, GDN_CONV_DIM), y_p.dtype),
            jnp.zeros((bsz, GDN_HEADS, GDN_DK, GDN_DV), state_gdn.dtype),
            jnp.zeros((bsz, 0, KV_LORA), cache_mla_ckv.dtype),
            jnp.zeros((bsz, 0, QK_ROPE), cache_mla_kpe.dtype),
            p)
        y_s, *st_s = _hybrid_layer(y_s, state_gdn_conv[l], state_gdn[l], cache_mla_ckv[l], cache_mla_kpe[l], p)
        outs_p.append(st_p)
        outs_s.append(st_s)
    stack = lambda outs, i: jnp.stack([o[i] for o in outs])
    return (y_p, y_s,
            stack(outs_p, 0), stack(outs_p, 1), stack(outs_p, 2), stack(outs_p, 3),
            stack(outs_s, 0), stack(outs_s, 1), stack(outs_s, 2), stack(outs_s, 3))
```

```python
import functools
import math

import jax
import jax.numpy as jnp
import numpy as np
from jax import lax
from jax.experimental import pallas as pl
from jax.experimental.pallas import tpu as pltpu

F32 = jnp.float32
BF16 = jnp.bfloat16

D_MODEL = 1024
CHUNK = 64
RMS_EPS = 1e-6

GDN_HEADS = 4
GDN_DK = 128
GDN_DV = 128
GDN_QK_DIM = GDN_HEADS * GDN_DK
GDN_V_DIM = GDN_HEADS * GDN_DV
GDN_CONV_DIM = 2 * GDN_QK_DIM + GDN_V_DIM
CONV_W = 4

MLA_HEADS = 4
Q_LORA = 512
KV_LORA = 256
QK_NOPE = 128
QK_ROPE = 64
ROPE_HALF = QK_ROPE // 2
V_HEAD = 128
V_ROWS = V_HEAD + 16
QK_HEAD = QK_NOPE + QK_ROPE
QK_PAD = QK_NOPE + 2 * QK_ROPE
ROPE_THETA = 10000.0

N_GROUPS = 4
EXPERTS_PER_GROUP = 8
N_EXPERTS = N_GROUPS * EXPERTS_PER_GROUP
D_EXPERT = 256

LANES = 128
VMEM_LIMIT_BYTES = 58 * 1024 * 1024

Q_SCALE = (QK_HEAD ** -0.5) * math.log2(math.e)
MASK_VALUE = -1e30
FLASH_HEADS_PER_STEP = 2
FLASH_RESIDENT_KV_BYTES = 8 * 1024 * 1024
GDN_SOLVE_PASSES = 1
TOKEN_SUBTILES = 2
MOE_SEG_ALIGN = 16
MOE_ROW_BLOCK = 128
MOE_EXPERTS_PER_STEP = 4


def _mm(a, b):
    return jnp.dot(a.astype(BF16), b.astype(BF16), preferred_element_type=F32)


def _mm_nt(a, b):
    return lax.dot_general(a.astype(BF16), b.astype(BF16), (((1,), (1,)), ((), ())),
                           preferred_element_type=F32)


def _mm_tn(a, b):
    return lax.dot_general(a.astype(BF16), b.astype(BF16), (((0,), (0,)), ((), ())),
                           preferred_element_type=F32)


def _split(a):
    hi = a.astype(BF16)
    lo = (a - hi.astype(F32)).astype(BF16)
    return hi, lo


def _mm3(a, b):
    ah, al = _split(a)
    bh, bl = _split(b)
    return _mm(ah, bh) + (_mm(ah, bl) + _mm(al, bh))


def _mm_solve(a, b):
    return _mm3(a, b) if GDN_SOLVE_PASSES == 3 else _mm(a, b)


def _mm_exact_rhs(a, b01):
    a1 = a.astype(BF16)
    r1 = a - a1.astype(F32)
    a2 = r1.astype(BF16)
    a3 = (r1 - a2.astype(F32)).astype(BF16)
    return _mm(a1, b01) + (_mm(a2, b01) + _mm(a3, b01))


def _mm2_exact_rhs(a, b01):
    hi, lo = _split(a)
    return _mm(hi, b01) + _mm(lo, b01)


def _mm_exact_lhs(a01, b):
    b1 = b.astype(BF16)
    r1 = b - b1.astype(F32)
    b2 = r1.astype(BF16)
    b3 = (r1 - b2.astype(F32)).astype(BF16)
    return _mm(a01, b1) + (_mm(a01, b2) + _mm(a01, b3))


def _rms(x, w):
    return x * lax.rsqrt(jnp.mean(x * x, axis=-1, keepdims=True) + RMS_EPS) * w


def _silu(x):
    return x * jax.nn.sigmoid(x)


def _swap_halves(x):
    h = x.shape[-1] // 2
    return jnp.concatenate([x[:, h:], x[:, :h]], axis=-1)


def _subtiles(tm):
    n = tm // TOKEN_SUBTILES
    return [pl.ds(i * n, n) for i in range(TOKEN_SUBTILES)]


def _const_spec(shape):
    nd = len(shape)
    return pl.BlockSpec(shape, lambda *_: (0,) * nd, pipeline_mode=pl.Buffered(1))


def _params(*sem):
    return pltpu.CompilerParams(dimension_semantics=sem, vmem_limit_bytes=VMEM_LIMIT_BYTES)


def _inproj_kernel(x_ref, lnw_ref, wqkv_ref, wz_ref, wba_ref, alog_ref, dtb_ref, wcq_ref, qanw_ref,
                   wuq_ref, qnw_ref, hsum_ref, swap_ref, cos_ref, sin_ref, wckv_ref, kvanw_ref, wkpe_ref,
                   wukv_ref, knw_ref, qkv_ref, z_ref, gates_ref, q_ref, ckv_ref, kpe_ref, *kv_refs):
    subs = _subtiles(x_ref.shape[0])
    h_ = [_rms(x_ref[r, :], lnw_ref[...]) for r in subs]
    hb_ = [h.astype(BF16) for h in h_]
    raw_, ckv_, cq_, kpe_ = [], [], [], []
    for r, h, hb in zip(subs, h_, hb_):
        qkv_ref[r, :] = _mm(hb, wqkv_ref[...])
        z_ref[r, :] = _mm(hb, wz_ref[...])
        kpe_.append(_mm(hb, wkpe_ref[...]))
        kpe_ref[r, :] = kpe_[-1]
        raw_.append(_mm3(h, wba_ref[...]))
        ckv_.append(_mm(hb, wckv_ref[...]))
        cq_.append(_mm(hb, wcq_ref[...]))

    qall_ = [_mm(_rms(cq, qanw_ref[...]), wuq_ref[...]) for cq in cq_]

    for i, (r, raw, ckv) in enumerate(zip(subs, raw_, ckv_)):
        lane = lax.broadcasted_iota(jnp.int32, raw.shape, 1)
        pre = raw + dtb_ref[...]
        softplus = jnp.maximum(pre, 0.0) + jnp.log1p(jnp.exp(-jnp.abs(pre)))
        g = -jnp.exp(alog_ref[...]) * softplus
        gates_ref[r, :] = jnp.where(lane < GDN_HEADS, jax.nn.sigmoid(raw), g)
        ckv_n = _rms(ckv, kvanw_ref[...])
        ckv_ref[r, :] = ckv_n
        if kv_refs:
            k_ref, vt_ref = kv_refs
            heads = _keys_values(ckv_n, kpe_[i], cos_ref[r, :], sin_ref[r, :], wukv_ref[...], knw_ref[...])
            for hd, (kh, vt) in enumerate(heads):
                k_ref[hd, r, :] = kh
                vt_ref[hd, i] = vt

    rope0 = MLA_HEADS * QK_NOPE
    lane = lax.broadcasted_iota(jnp.int32, (1, 2 * QK_ROPE), 1)
    for r, qall in zip(subs, qall_):
        ssq = _mm(qall * qall, hsum_ref[...])
        qn = qall * (lax.rsqrt(ssq * (1.0 / QK_HEAD) + RMS_EPS) * Q_SCALE) * qnw_ref[...]
        rw = qn[:, rope0:]
        cos2 = jnp.concatenate([cos_ref[r, :]] * 2, axis=-1)
        sin2 = jnp.concatenate([sin_ref[r, :]] * 2, axis=-1)
        rot = rw * cos2 + _mm2_exact_rhs(rw, swap_ref[...]) * sin2
        for hd in range(MLA_HEADS):
            pair = rot[:, (hd // 2) * 2 * QK_ROPE:(hd // 2 + 1) * 2 * QK_ROPE]
            mine = jnp.where((lane // QK_ROPE) == (hd % 2), pair, 0.0)
            q_ref[hd, r, :] = jnp.concatenate([qn[:, hd * QK_NOPE:(hd + 1) * QK_NOPE], mine], axis=-1).astype(BF16)


def _inproj(x2d, p, rope_q, tm, key_tile=None):
    t = x2d.shape[0]
    nt = t // tm
    cos_t, sin_t = rope_q
    n_rope_blocks = cos_t.shape[0] // tm
    rope_spec = pl.BlockSpec((tm, 2 * QK_ROPE), lambda i: (i % n_rope_blocks, 0))
    row = lambda n: pl.BlockSpec((tm, n), lambda i: (i, 0))
    consts = [p['ln_mix_w'], p['w_qkv'], p['w_z'], p['w_ba'], p['a_log'], p['dt_bias'], p['w_cq'],
              p['q_a_norm_w'], p['w_uq'], p['q_norm_w'], p['q_head_sum'], p['rope_swap']]
    consts2 = [p['w_ckv'], p['kv_a_norm_w'], p['w_kpe'], p['w_ukv'], p['k_norm_w']]
    in_specs = ([row(D_MODEL)] + [_const_spec(c.shape) for c in consts] + [rope_spec, rope_spec]
                + [_const_spec(c.shape) for c in consts2])
    out_shape = (
        jax.ShapeDtypeStruct((t, GDN_CONV_DIM), F32),
        jax.ShapeDtypeStruct((t, GDN_V_DIM), F32),
        jax.ShapeDtypeStruct((t, LANES), F32),
        jax.ShapeDtypeStruct((MLA_HEADS, t, QK_PAD), BF16),
        jax.ShapeDtypeStruct((t, KV_LORA), F32),
        jax.ShapeDtypeStruct((t, QK_ROPE), F32),
    )
    out_specs = (row(GDN_CONV_DIM), row(GDN_V_DIM), row(LANES),
                 pl.BlockSpec((MLA_HEADS, tm, QK_PAD), lambda i: (0, i, 0)),
                 row(KV_LORA), row(QK_ROPE))
    if key_tile is not None:
        assert tm // TOKEN_SUBTILES == key_tile
        out_shape += (jax.ShapeDtypeStruct((MLA_HEADS, t, QK_PAD), BF16),
                      jax.ShapeDtypeStruct((MLA_HEADS, t // key_tile, V_ROWS, key_tile), BF16))
        out_specs += (pl.BlockSpec((MLA_HEADS, tm, QK_PAD), lambda i: (0, i, 0)),
                      pl.BlockSpec((MLA_HEADS, TOKEN_SUBTILES, V_ROWS, key_tile), lambda i: (0, i, 0, 0)))
    return pl.pallas_call(
        _inproj_kernel, grid=(nt,), in_specs=in_specs, out_specs=out_specs, out_shape=out_shape,
        compiler_params=_params("parallel"), name="inproj",
    )(x2d, *consts, cos_t, sin_t, *consts2)


def _keys_values(ckv, kpe, cosf, sinf, wukv, knw):
    kv = _mm(ckv, wukv)
    kw = kpe * knw[:, QK_NOPE:]
    krot = kw * cosf[:, :QK_ROPE] + _swap_halves(kw) * sinf[:, :QK_ROPE]
    krot = jnp.concatenate([krot, krot], axis=-1)
    s_kpe = jnp.sum(kpe * kpe, axis=-1, keepdims=True)
    v0 = MLA_HEADS * QK_NOPE
    pad_row = lax.broadcasted_iota(jnp.int32, (V_ROWS - V_HEAD, kv.shape[0]), 0)
    ones_rows = jnp.where(pad_row == 0, 1.0, 0.0)
    out = []
    for hd in range(MLA_HEADS):
        kn = kv[:, hd * QK_NOPE:(hd + 1) * QK_NOPE]
        inv = lax.rsqrt((jnp.sum(kn * kn, axis=-1, keepdims=True) + s_kpe) * (1.0 / QK_HEAD) + RMS_EPS)
        kh = jnp.concatenate([kn * knw[:, :QK_NOPE], krot], axis=-1) * inv
        vt = kv[:, v0 + hd * V_HEAD: v0 + (hd + 1) * V_HEAD].T
        out.append((kh.astype(BF16), jnp.concatenate([vt, ones_rows], axis=0).astype(BF16)))
    return out


def _kvprep_kernel(ckv_old_ref, kpe_old_ref, ckv_new_ref, kpe_new_ref, cos_ref, sin_ref, wukv_ref, knw_ref,
                   k_ref, vt_ref):
    n_pad = k_ref.shape[1] - ckv_old_ref.shape[1] - ckv_new_ref.shape[1]

    def rows(old_ref, new_ref):
        return jnp.concatenate([old_ref[0], new_ref[0], jnp.zeros((n_pad, old_ref.shape[2]), F32)], axis=0)

    heads = _keys_values(rows(ckv_old_ref, ckv_new_ref), rows(kpe_old_ref, kpe_new_ref),
                         cos_ref[...], sin_ref[...], wukv_ref[...], knw_ref[...])
    for hd, (kh, vt) in enumerate(heads):
        k_ref[hd] = kh
        vt_ref[hd, 0] = vt


def _kvprep(ckv_old, kpe_old, ckv_new, kpe_new, p, rope_k, lk):
    bsz = ckv_old.shape[0]
    cos_t, sin_t = rope_k
    per_seq = lambda a: pl.BlockSpec((1,) + a.shape[1:], lambda b: (b, 0, 0))
    return pl.pallas_call(
        _kvprep_kernel, grid=(bsz,),
        in_specs=[per_seq(ckv_old), per_seq(kpe_old), per_seq(ckv_new), per_seq(kpe_new),
                  _const_spec(cos_t.shape), _const_spec(sin_t.shape),
                  _const_spec(p['w_ukv'].shape), _const_spec(p['k_norm_w'].shape)],
        out_specs=(pl.BlockSpec((MLA_HEADS, lk, QK_PAD), lambda b: (0, b, 0)),
                   pl.BlockSpec((MLA_HEADS, 1, V_ROWS, lk), lambda b: (0, b, 0, 0))),
        out_shape=(jax.ShapeDtypeStruct((MLA_HEADS, bsz * lk, QK_PAD), BF16),
                   jax.ShapeDtypeStruct((MLA_HEADS, bsz, V_ROWS, lk), BF16)),
        compiler_params=_params("parallel"), name="kvprep",
    )(ckv_old, kpe_old, ckv_new, kpe_new, cos_t, sin_t, p['w_ukv'], p['k_norm_w'])


def _flash_kernel(q_ref, k_ref, vt_ref, o_ref, m_sc, acc_sc, *, tq, tk, nk, past, hps, qsplit):
    i = pl.program_id(2)
    q_first = past + i * tq
    q_last = q_first + tq - 1
    n_full = jnp.minimum(nk, ((q_first // CHUNK + 1) * CHUNK) // tk)
    n_vis = jnp.minimum(nk, ((q_last // CHUNK + 1) * CHUNK + tk - 1) // tk)

    m_sc[...] = jnp.full(m_sc.shape, MASK_VALUE, F32)
    acc_sc[...] = jnp.zeros(acc_sc.shape, F32)

    tqs = tq // qsplit
    streams = [(hd, pl.ds(c * tqs, tqs)) for hd in range(hps) for c in range(qsplit)]

    def step(j, masked):
        k0 = pl.multiple_of(j * tk, tk)
        s_ = [_mm_nt(k_ref[hd, pl.ds(k0, tk), :], q_ref[hd, cols, :]) for hd, cols in streams]
        if masked:
            kpos = k0 + lax.broadcasted_iota(jnp.int32, (tk, 1), 0)
            for n, (hd, cols) in enumerate(streams):
                qpos = q_first + (n % qsplit) * tqs + lax.broadcasted_iota(jnp.int32, (1, tqs), 1)
                s_[n] = jnp.where((kpos // CHUNK) <= (qpos // CHUNK), s_[n], MASK_VALUE)
        p_, alpha_ = [], []
        for n, (hd, cols) in enumerate(streams):
            m_old = m_sc[hd, :, cols]
            m_new = jnp.maximum(m_old, jnp.max(s_[n], axis=0, keepdims=True))
            alpha_.append(jnp.exp2(m_old - m_new))
            p_.append(jnp.exp2(s_[n] - m_new).astype(BF16))
            m_sc[hd, :, cols] = m_new
        for n, (hd, cols) in enumerate(streams):
            acc_sc[hd, :, cols] = alpha_[n] * acc_sc[hd, :, cols] + _mm(vt_ref[hd, j], p_[n])

    def full_body(j, c):
        step(j, False)
        return c

    def masked_body(j, c):
        step(j, True)
        return c

    lax.fori_loop(0, n_full, full_body, 0)
    lax.fori_loop(n_full, n_vis, masked_body, 0)
    for hd in range(hps):
        o_t = acc_sc[hd, :V_HEAD, :] / acc_sc[hd, V_HEAD:V_HEAD + 1, :]
        o_ref[:, hd * V_HEAD:(hd + 1) * V_HEAD] = o_t.T.astype(o_ref.dtype)


def _flash(q, k, vt, bsz, lq, lk, past, tq, tk, hps, qsplit):
    nq = lq // tq
    nk = lk // tk
    kern = functools.partial(_flash_kernel, tq=tq, tk=tk, nk=nk, past=past, hps=hps, qsplit=qsplit)
    once = pl.Buffered(1 if nq > 1 else 2)
    return pl.pallas_call(
        kern, grid=(bsz, MLA_HEADS // hps, nq),
        in_specs=[pl.BlockSpec((hps, tq, QK_PAD), lambda b, h, i: (h, b * nq + i, 0)),
                  pl.BlockSpec((hps, lk, QK_PAD), lambda b, h, i: (h, b, 0), pipeline_mode=once),
                  pl.BlockSpec((hps, nk, V_ROWS, tk), lambda b, h, i: (h, b, 0, 0), pipeline_mode=once)],
        out_specs=pl.BlockSpec((tq, hps * V_HEAD), lambda b, h, i: (b * nq + i, h)),
        out_shape=jax.ShapeDtypeStruct((bsz * lq, MLA_HEADS * V_HEAD), BF16),
        scratch_shapes=[pltpu.VMEM((hps, 1, tq), F32), pltpu.VMEM((hps, V_ROWS, tq), F32)],
        compiler_params=_params("parallel", "parallel", "arbitrary"), name="flash",
    )(q, k, vt)


def _gdn_kernel(qkv_ref, z_ref, gates_ref, hist_ref, s0_ref, convw_ref, normw_ref,
                o_ref, s_ref, xbuf, *, tb, nseq):
    j = pl.program_id(1)
    nc = tb // CHUNK
    pad = 8
    seqs = range(nseq)
    streams = [(b, hd) for b in seqs for hd in range(GDN_HEADS)]

    @pl.when(j == 0)
    def _():
        xbuf[:, 0:pad, :] = hist_ref[...]
        s_ref[...] = s0_ref[...]

    cw = convw_ref[...]
    y_ = []
    for b in seqs:
        xbuf[b, pad:pad + tb, :] = qkv_ref[b]
        y = xbuf[b, pad - 3:pad - 3 + tb, :] * cw[0:1, :]
        for i in range(1, CONV_W):
            y = y + xbuf[b, pad - 3 + i:pad - 3 + i + tb, :] * cw[i:i + 1, :]
        xbuf[b, 0:pad, :] = xbuf[b, tb:tb + pad, :]
        y_.append(_silu(y))

    row = lax.broadcasted_iota(jnp.int32, (tb, tb), 0)
    col = lax.broadcasted_iota(jnp.int32, (tb, tb), 1)
    same = (row // CHUNK) == (col // CHUNK)
    incl = same & (row >= col)
    strict = same & (row > col)
    tril01 = jnp.where(incl, 1.0, 0.0).astype(BF16)
    triu01 = jnp.where(same & (row <= col), 1.0, 0.0).astype(BF16)
    eye = jnp.where(row == col, 1.0, 0.0)
    gates_ = [gates_ref[b] for b in seqs]
    gc_cols_ = [_mm_exact_lhs(tril01, g) for g in gates_]
    gc_rows_ = [_mm_exact_rhs(g.T[0:8, :], triu01) for g in gates_]

    q_, k_, v_, beta_, gc_, decay_, kb_ = [], [], [], [], [], [], []
    for n, (b, hd) in enumerate(streams):
        y = y_[b]
        qh = y[:, hd * GDN_DK:(hd + 1) * GDN_DK]
        kh = y[:, GDN_QK_DIM + hd * GDN_DK: GDN_QK_DIM + (hd + 1) * GDN_DK]
        q_.append(qh * (lax.rsqrt(jnp.sum(qh * qh, axis=-1, keepdims=True) + RMS_EPS) * (GDN_DK ** -0.5)))
        k_.append(kh * lax.rsqrt(jnp.sum(kh * kh, axis=-1, keepdims=True) + RMS_EPS))
        v_.append(y[:, 2 * GDN_QK_DIM + hd * GDN_DV: 2 * GDN_QK_DIM + (hd + 1) * GDN_DV])
        beta_.append(gates_[b][:, hd:hd + 1])
        gc = gc_cols_[b][:, GDN_HEADS + hd: GDN_HEADS + hd + 1]
        gr = gc_rows_[b][GDN_HEADS + hd: GDN_HEADS + hd + 1, :]
        gc_.append(gc)
        decay_.append(jnp.where(incl, jnp.exp(jnp.where(incl, gc - gr, 0.0)), 0.0))
        kb_.append(k_[n] * beta_[n])
    ns = range(len(streams))
    khb_ = [k.astype(BF16) for k in k_]
    pw_ = [jnp.where(strict, -(_mm_nt(kb_[n], khb_[n]) * decay_[n]), 0.0) for n in ns]
    attn_ = [_mm_nt(q_[n], khb_[n]) * decay_[n] for n in ns]

    tinv_ = [eye + pw for pw in pw_]
    for _ in range(int(math.log2(CHUNK)) - 1):
        pw_ = [_mm_solve(pw, pw) for pw in pw_]
        tinv_ = [tinv_[n] + _mm_solve(tinv_[n], pw_[n]) for n in ns]

    eg_ = [jnp.exp(gc) for gc in gc_]
    sol_ = [_mm_solve(tinv_[n], jnp.concatenate([v_[n] * beta_[n], kb_[n] * eg_[n]], axis=-1)) for n in ns]
    qdec_ = [q_[n] * eg_[n] for n in ns]

    s_ = [s_ref[b, hd] for b, hd in streams]
    us_ = [[] for _ in ns]
    oi_ = [[] for _ in ns]
    for c in range(nc):
        r0, r1 = c * CHUNK, (c + 1) * CHUNK
        for n in ns:
            g_end = gc_[n][r1 - 1:r1, :]
            k_end = k_[n][r0:r1] * jnp.exp(g_end - gc_[n][r0:r1])
            both = _mm(jnp.concatenate([sol_[n][r0:r1, GDN_DV:], qdec_[n][r0:r1]], axis=0), s_[n])
            u = sol_[n][r0:r1, :GDN_DV] - both[:CHUNK]
            oi_[n].append(both[CHUNK:])
            us_[n].append(u)
            s_[n] = s_[n] * jnp.exp(g_end) + _mm_tn(k_end, u)
    for n, (b, hd) in enumerate(streams):
        s_ref[b, hd] = s_[n]
        u_all = us_[n][0] if nc == 1 else jnp.concatenate(us_[n], axis=0)
        o_all = oi_[n][0] if nc == 1 else jnp.concatenate(oi_[n], axis=0)
        o = o_all + _mm(attn_[n], u_all)
        o = _rms(o, normw_ref[...]) * _silu(z_ref[b, :, hd * GDN_DV:(hd + 1) * GDN_DV])
        o_ref[b, :, hd * GDN_DV:(hd + 1) * GDN_DV] = o.astype(o_ref.dtype)


def _gdn(qkv2d, z2d, gates2d, hist_pad, s0, p, bsz, seq, tb, nseq):
    nb = seq // tb
    kern = functools.partial(_gdn_kernel, tb=tb, nseq=nseq)
    row = lambda n: pl.BlockSpec((nseq, tb, n), lambda g, j: (g, j, 0))
    state_spec = pl.BlockSpec((nseq, GDN_HEADS, GDN_DK, GDN_DV), lambda g, j: (g, 0, 0, 0))
    view = lambda a: a.reshape(bsz, seq, a.shape[-1])
    out, s_new = pl.pallas_call(
        kern, grid=(bsz // nseq, nb),
        in_specs=[row(GDN_CONV_DIM), row(GDN_V_DIM), row(LANES),
                  pl.BlockSpec((nseq, 8, GDN_CONV_DIM), lambda g, j: (g, 0, 0)),
                  state_spec, _const_spec(p['conv_w'].shape), _const_spec(p['gdn_norm_w'].shape)],
        out_specs=(row(GDN_V_DIM), state_spec),
        out_shape=(jax.ShapeDtypeStruct((bsz, seq, GDN_V_DIM), BF16),
                   jax.ShapeDtypeStruct((bsz, GDN_HEADS, GDN_DK, GDN_DV), F32)),
        scratch_shapes=[pltpu.VMEM((nseq, tb + 8, GDN_CONV_DIM), F32)],
        compiler_params=_params("parallel", "arbitrary"), name="gdn",
    )(view(qkv2d), view(z2d), view(gates2d), hist_pad, s0, p['conv_w'], p['gdn_norm_w'])
    return out.reshape(bsz * seq, GDN_V_DIM), s_new


def _outproj_kernel(x_ref, g_ref, a_ref, wog_ref, woa_ref, lnw_ref, wr_ref, br_ref,
                    x1_ref, h2_ref, comb_ref):
    subs = _subtiles(x_ref.shape[0])
    h2_ = []
    for r in subs:
        x1 = x_ref[r, :] + _mm(g_ref[r, :], wog_ref[...]) + _mm(a_ref[r, :], woa_ref[...])
        x1_ref[r, :] = x1
        h2_.append(_rms(x1, lnw_ref[...]))
        h2_ref[r, :] = h2_[-1].astype(BF16)
    logits_ = [_mm3(h2, wr_ref[...]) + br_ref[...] for h2 in h2_]
    for r, logits in zip(subs, logits_):
        comb_ref[r, :] = _route(logits)


def _route(logits):
    lane = lax.broadcasted_iota(jnp.int32, logits.shape, 1)
    big = jnp.int32(LANES)
    is_grp = (lane >= N_EXPERTS) & (lane < N_EXPERTS + N_GROUPS)
    gl = jnp.where(is_grp, logits, -jnp.inf)
    gmax = jnp.max(gl, axis=-1, keepdims=True)
    grp_p = 1.0 / jnp.sum(jnp.exp(gl - gmax), axis=-1, keepdims=True)
    grp_idx = jnp.min(jnp.where(gl == gmax, lane - N_EXPERTS, big), axis=-1, keepdims=True)

    in_grp = (lane < N_EXPERTS) & ((lane // EXPERTS_PER_GROUP) == grp_idx)
    el = jnp.where(in_grp, logits, -jnp.inf)
    m1 = jnp.max(el, axis=-1, keepdims=True)
    i1 = jnp.min(jnp.where(el == m1, lane, big), axis=-1, keepdims=True)
    el2 = jnp.where(lane == i1, -jnp.inf, el)
    m2 = jnp.max(el2, axis=-1, keepdims=True)
    i2 = jnp.min(jnp.where(el2 == m2, lane, big), axis=-1, keepdims=True)
    e2 = jnp.exp(m2 - m1)
    w1 = 1.0 / (1.0 + e2)
    w2 = e2 * w1
    return jnp.where(lane == 0, i1.astype(F32), jnp.where(lane == 1, i2.astype(F32),
                     jnp.where(lane == 2, grp_p * w1, jnp.where(lane == 3, grp_p * w2, 0.0))))


def _outproj(x2d, gdn_out, mla_out, p, tm):
    t = x2d.shape[0]
    row = lambda n: pl.BlockSpec((tm, n), lambda i: (i, 0))
    consts = [p['w_out_gdn'], p['w_out_mla'], p['ln_ffn_w'], p['w_router'], p['b_router']]
    return pl.pallas_call(
        _outproj_kernel, grid=(t // tm,),
        in_specs=[row(D_MODEL), row(GDN_V_DIM), row(MLA_HEADS * V_HEAD)] + [_const_spec(c.shape) for c in consts],
        out_specs=(row(D_MODEL), row(D_MODEL), row(LANES)),
        out_shape=(jax.ShapeDtypeStruct((t, D_MODEL), F32), jax.ShapeDtypeStruct((t, D_MODEL), BF16),
                   jax.ShapeDtypeStruct((t, LANES), F32)),
        compiler_params=_params("parallel"), name="outproj",
    )(x2d, gdn_out, mla_out, *consts)


def _moe_slots(tm):
    n = 2 * tm + N_EXPERTS * MOE_SEG_ALIGN + MOE_ROW_BLOCK
    return -(-n // LANES) * LANES


def _moe_kernel(h_ref, route_ref, x1_ref, wg_ref, wu_ref, wd_ref, y_ref, xs_sc, ys_sc, dcol_sc, offs_ref,
                *, tm, slots, n_chunks):
    g = pl.program_id(1)
    chunk = slots // n_chunks
    lb = min(2 * LANES, tm)
    nlb = tm // lb

    @pl.when(g == 0)
    def _sort_picks():
        rt = route_ref[...].T
        eid = lax.broadcasted_iota(jnp.int32, (N_EXPERTS, lb), 0).astype(F32)
        pieces = [jnp.where(rt[k:k + 1, b * lb:(b + 1) * lb] == eid, 1.0, 0.0)
                  for k in range(2) for b in range(nlb)]
        ind = jnp.concatenate(pieces, axis=0)
        r_i = lax.broadcasted_iota(jnp.int32, (lb, lb), 0)
        c_i = lax.broadcasted_iota(jnp.int32, (lb, lb), 1)
        before = _mm(ind, jnp.where(r_i < c_i, 1.0, 0.0))
        tot = jnp.sum(ind, axis=1, keepdims=True)
        run = jnp.zeros((N_EXPERTS, 1), F32)
        runs = []
        for i in range(len(pieces)):
            runs.append(run)
            run = run + tot[i * N_EXPERTS:(i + 1) * N_EXPERTS]
        seg = jnp.floor((run + (MOE_SEG_ALIGN - 1)) * (1.0 / MOE_SEG_ALIGN)) * MOE_SEG_ALIGN
        er = lax.broadcasted_iota(jnp.int32, (N_EXPERTS, N_EXPERTS), 0)
        ec = lax.broadcasted_iota(jnp.int32, (N_EXPERTS, N_EXPERTS), 1)
        start = _mm_exact_lhs(jnp.where(er > ec, 1.0, 0.0).astype(BF16),
                              jnp.broadcast_to(seg, (N_EXPERTS, LANES)))[:, 0:1]
        start_i = start.astype(jnp.int32)
        for e in range(N_EXPERTS):
            offs_ref[e] = start_i[e, 0]
        offs_ref[N_EXPERTS] = start_i[N_EXPERTS - 1, 0] + seg.astype(jnp.int32)[N_EXPERTS - 1, 0]

        dest = []
        for k in range(2):
            cols = []
            for b in range(nlb):
                i = k * nlb + b
                piece = ind[i * N_EXPERTS:(i + 1) * N_EXPERTS]
                pos = start + runs[i] + before[i * N_EXPERTS:(i + 1) * N_EXPERTS]
                cols.append(jnp.sum(piece * pos, axis=0, keepdims=True))
            dest.append(cols[0] if nlb == 1 else jnp.concatenate(cols, axis=1))
        rec = jnp.concatenate([dest[0], dest[1], rt[2:4], jnp.zeros((LANES - 4, tm), F32)], axis=0)
        dcol_sc[...] = rec.T

        h = h_ref[...]
        for c in range(n_chunks):
            s_i = (c * chunk + lax.broadcasted_iota(jnp.int32, (chunk, 1), 0)).astype(F32)
            sel = jnp.where(s_i == dest[0], 1.0, jnp.where(s_i == dest[1], 1.0, 0.0))
            xs_sc[c * chunk:(c + 1) * chunk, :] = _mm(sel, h).astype(BF16)
        ys_sc[...] = jnp.zeros(ys_sc.shape, BF16)

    steps = range(MOE_EXPERTS_PER_STEP)
    offs = [offs_ref[g * MOE_EXPERTS_PER_STEP + q] for q in steps] + [offs_ref[(g + 1) * MOE_EXPERTS_PER_STEP]]

    def rows(q, b):
        return pl.ds(pl.multiple_of(offs[q] + b * MOE_ROW_BLOCK, MOE_SEG_ALIGN), MOE_ROW_BLOCK)

    for q in steps:
        n_blocks = (offs[q + 1] - offs[q] + (MOE_ROW_BLOCK - 1)) // MOE_ROW_BLOCK

        def block(b, carry, q=q):
            xb = xs_sc[rows(q, b), :]
            act = _silu(_mm(xb, wg_ref[q])) * _mm(xb, wu_ref[q])
            ys_sc[rows(q, b), :] = _mm(act, wd_ref[q]).astype(BF16)
            return carry

        lax.fori_loop(1, n_blocks, block, 0)

    xb_ = [xs_sc[rows(q, 0), :] for q in steps]
    gate_ = [_mm(xb_[q], wg_ref[q]) for q in steps]
    up_ = [_mm(xb_[q], wu_ref[q]) for q in steps]
    act_ = [_silu(gate_[q]) * up_[q] for q in steps]
    out_ = [_mm(act_[q], wd_ref[q]).astype(BF16) for q in steps]
    for q in steps:
        ys_sc[rows(q, 0), :] = out_[q]

    @pl.when(g == pl.num_programs(1) - 1)
    def _combine():
        rec = dcol_sc[...]
        d0, d1, c0, c1 = rec[:, 0:1], rec[:, 1:2], rec[:, 2:3], rec[:, 3:4]
        y = x1_ref[...]
        for c in range(n_chunks):
            s_i = (c * chunk + lax.broadcasted_iota(jnp.int32, (1, chunk), 1)).astype(F32)
            wsel = jnp.where(s_i == d0, c0, jnp.where(s_i == d1, c1, 0.0))
            y = y + _mm(wsel, ys_sc[c * chunk:(c + 1) * chunk, :])
        y_ref[...] = y


def _moe(h2, route, x1, p, tm):
    t = h2.shape[0]
    slots = _moe_slots(tm)
    n_chunks = 3 if slots % (3 * LANES) == 0 else 1
    kern = functools.partial(_moe_kernel, tm=tm, slots=slots, n_chunks=n_chunks)
    row = lambda n: pl.BlockSpec((tm, n), lambda i, g: (i, 0))
    eps = MOE_EXPERTS_PER_STEP
    return pl.pallas_call(
        kern, grid=(t // tm, N_EXPERTS // eps),
        in_specs=[row(D_MODEL), row(LANES), row(D_MODEL),
                  pl.BlockSpec((eps, D_MODEL, D_EXPERT), lambda i, g: (g, 0, 0)),
                  pl.BlockSpec((eps, D_MODEL, D_EXPERT), lambda i, g: (g, 0, 0)),
                  pl.BlockSpec((eps, D_EXPERT, D_MODEL), lambda i, g: (g, 0, 0))],
        out_specs=pl.BlockSpec((tm, D_MODEL), lambda i, g: (i, 0)),
        out_shape=jax.ShapeDtypeStruct((t, D_MODEL), F32),
        scratch_shapes=[pltpu.VMEM((slots, D_MODEL), BF16), pltpu.VMEM((slots, D_MODEL), BF16),
                        pltpu.VMEM((tm, LANES), F32), pltpu.SMEM((N_EXPERTS + 1,), jnp.int32)],
        compiler_params=_params("parallel", "arbitrary"), name="moe",
    )(h2, route, x1, p['w_exp_gate'], p['w_exp_up'], p['w_exp_down'])


def _cast_kernel(*refs):
    n = len(refs) // 2
    for src, dst in zip(refs[:n], refs[n:]):
        dst[...] = src[...].astype(dst.dtype)


def _expert_weights_bf16(*weights):
    specs = [pl.BlockSpec((1,) + w.shape[1:], lambda e: (e, 0, 0)) for w in weights]
    return pl.pallas_call(
        _cast_kernel, grid=(weights[0].shape[0],), in_specs=specs, out_specs=specs,
        out_shape=[jax.ShapeDtypeStruct(w.shape, BF16) for w in weights],
        compiler_params=_params("parallel"), name="expert_cast",
    )(*weights)


def _rope_tables(pos):
    inv = ROPE_THETA ** (-jnp.arange(ROPE_HALF, dtype=F32) / ROPE_HALF)
    ang = pos.astype(F32)[:, None] * inv[None, :]
    cos, sin = jnp.cos(ang), jnp.sin(ang)
    return jnp.concatenate([cos, cos] * 2, axis=-1), jnp.concatenate([-sin, sin] * 2, axis=-1)


def _tile_rows(tables, tm):
    n = tables[0].shape[0]
    if n >= tm:
        assert n % tm == 0
        return tables
    assert tm % n == 0
    return tuple(jnp.tile(t, (tm // n, 1)) for t in tables)


def _pad_lanes(v, offset):
    return jnp.zeros((1, LANES), F32).at[0, offset:offset + v.shape[0]].set(v.astype(F32))


def _q_layout_constants():
    n_nope = MLA_HEADS * QK_NOPE
    col = np.arange(n_nope + MLA_HEADS * QK_ROPE)
    head = np.where(col < n_nope, col // QK_NOPE, (col - n_nope) // QK_ROPE)
    same_head = (head[:, None] == head[None, :]).astype(np.float32)
    j = np.arange(MLA_HEADS * QK_ROPE)
    src = (j // QK_ROPE) * QK_ROPE + (j % QK_ROPE + ROPE_HALF) % QK_ROPE
    swap = (j[:, None] == src[None, :]).astype(np.float32)
    return same_head, swap


_Q_HEAD_SUM, _ROPE_SWAP = _q_layout_constants()


def _prepare_params(p):
    w_in = p['w_in']
    sizes = (GDN_CONV_DIM, GDN_V_DIM, GDN_HEADS, GDN_HEADS, Q_LORA, KV_LORA, QK_ROPE)
    offs = np.concatenate([[0], np.cumsum(sizes)])
    part = [w_in[:, int(offs[i]):int(offs[i + 1])] for i in range(len(sizes))]
    w_ba = jnp.zeros((D_MODEL, LANES), F32).at[:, :2 * GDN_HEADS].set(jnp.concatenate([part[2], part[3]], axis=1))

    uq = p['mla_w_uq'].reshape(Q_LORA, MLA_HEADS, QK_HEAD)
    w_uq = jnp.concatenate([uq[:, :, :QK_NOPE].reshape(Q_LORA, -1), uq[:, :, QK_NOPE:].reshape(Q_LORA, -1)], axis=1)
    ukv = p['mla_w_ukv'].reshape(KV_LORA, MLA_HEADS, QK_NOPE + V_HEAD)
    w_ukv = jnp.concatenate([ukv[:, :, :QK_NOPE].reshape(KV_LORA, -1), ukv[:, :, QK_NOPE:].reshape(KV_LORA, -1)], axis=1)

    w_router = jnp.zeros((D_MODEL, LANES), F32)
    w_router = w_router.at[:, :N_EXPERTS].set(p['w_expert_router'])
    w_router = w_router.at[:, N_EXPERTS:N_EXPERTS + N_GROUPS].set(p['w_group_router'])
    b_router = _pad_lanes(jnp.concatenate([p['b_expert_router'], p['b_group_router']]), 0)

    row = lambda v: v.astype(F32)[None, :]
    return {
        'ln_mix_w': row(p['ln_mix_w']),
        'w_qkv': part[0].astype(BF16), 'w_z': part[1].astype(BF16), 'w_ba': w_ba,
        'a_log': _pad_lanes(p['gdn_a_log'], GDN_HEADS), 'dt_bias': _pad_lanes(p['gdn_dt_bias'], GDN_HEADS),
        'w_cq': part[4].astype(BF16), 'w_ckv': part[5].astype(BF16), 'w_kpe': part[6].astype(BF16),
        'q_a_norm_w': row(p['mla_q_a_norm_w']), 'kv_a_norm_w': row(p['mla_kv_a_norm_w']),
        'w_uq': w_uq.astype(BF16), 'w_ukv': w_ukv.astype(BF16),
        'q_norm_w': row(jnp.concatenate([jnp.tile(p['mla_q_norm_w'][:QK_NOPE], MLA_HEADS),
                                         jnp.tile(p['mla_q_norm_w'][QK_NOPE:], MLA_HEADS)])),
        'q_head_sum': jnp.asarray(_Q_HEAD_SUM, BF16), 'rope_swap': jnp.asarray(_ROPE_SWAP, BF16),
        'k_norm_w': row(p['mla_k_norm_w']),
        'conv_w': p['gdn_conv_w'].astype(F32), 'gdn_norm_w': row(p['gdn_norm_w']),
        'w_out_gdn': p['w_out'][:GDN_V_DIM].astype(BF16), 'w_out_mla': p['w_out'][GDN_V_DIM:].astype(BF16),
        'ln_ffn_w': row(p['ln_ffn_w']), 'w_router': w_router, 'b_router': b_router,
        **dict(zip(('w_exp_gate', 'w_exp_up', 'w_exp_down'),
                   _expert_weights_bf16(p['w_exp_gate'], p['w_exp_up'], p['w_exp_down']))),
    }


def _tiles(bsz, seq, past):
    t = bsz * seq
    tm = min(1024, t)
    n_keys = past + seq
    if past == 0:
        tk = min(tm // TOKEN_SUBTILES, seq)
        tq = min(2 * tk, seq)
        lk = n_keys
    else:
        assert n_keys % CHUNK == 0
        tq = seq
        lk = -(-n_keys // LANES) * LANES
        tk = lk
    kv_bytes_per_head = lk * (QK_PAD + V_ROWS) * 2
    hps = MLA_HEADS if MLA_HEADS * kv_bytes_per_head <= FLASH_RESIDENT_KV_BYTES else FLASH_HEADS_PER_STEP
    tb = min(256, seq)
    gseq = 2 if bsz % 2 == 0 else 1
    tmoe = min(1024, t)
    assert seq % tq == 0 and lk % tk == 0 and seq % tb == 0 and tb % CHUNK == 0 and t % tm == 0 and t % tmoe == 0
    return tm, tq, tk, lk, hps, tb, gseq, tmoe


def _hybrid_layer(x, conv_hist, s0, ckv_past, kpe_past, p):
    bsz, seq, _ = x.shape
    past = ckv_past.shape[1]
    t = bsz * seq
    tm, tq, tk, lk, hps, tb, gseq, tmoe = _tiles(bsz, seq, past)
    x2d = x.reshape(t, D_MODEL)

    rope_q = _tile_rows(_rope_tables(past + jnp.arange(seq, dtype=jnp.int32)), tm)
    n_keys = past + seq
    if past == 0:
        qkv, z, gates, q, ckv, kpe, k, vt = _inproj(x2d, p, rope_q, tm, key_tile=tk)
    else:
        assert tk == lk
        qkv, z, gates, q, ckv, kpe = _inproj(x2d, p, rope_q, tm)
        rope_k = _rope_tables(jnp.arange(lk, dtype=jnp.int32))
        k, vt = _kvprep(ckv_past.astype(F32), kpe_past.astype(F32), ckv.reshape(bsz, seq, KV_LORA),
                        kpe.reshape(bsz, seq, QK_ROPE), p, rope_k, lk)
    ckv3, kpe3 = ckv.reshape(bsz, seq, KV_LORA), kpe.reshape(bsz, seq, QK_ROPE)
    mla_out = _flash(q, k, vt, bsz, seq, lk, past, tq, tk, hps, tq // min(tq, tk))

    hist_pad = jnp.concatenate([jnp.zeros((bsz, 8 - (CONV_W - 1), GDN_CONV_DIM), F32), conv_hist.astype(F32)], axis=1)
    gdn_out, s_new = _gdn(qkv, z, gates, hist_pad, s0.astype(F32), p, bsz, seq, tb, gseq)
    assert seq >= CONV_W - 1
    conv_new = qkv.reshape(bsz, seq, GDN_CONV_DIM)[:, seq - (CONV_W - 1):]

    x1, h2, comb = _outproj(x2d, gdn_out, mla_out, p, tm)
    y = _moe(h2, comb, x1, p, tmoe)
    return y.reshape(bsz, seq, D_MODEL), ckv3, kpe3, s_new.astype(s0.dtype), conv_new


def kernel(x_prompt, x_sample, cache_mla_ckv, cache_mla_kpe, state_gdn, state_gdn_conv, ln_mix_w, w_in, gdn_conv_w, gdn_a_log, gdn_dt_bias, gdn_norm_w, mla_q_a_norm_w, mla_w_uq, mla_kv_a_norm_w, mla_w_ukv, mla_q_norm_w, mla_k_norm_w, w_out, ln_ffn_w, w_group_router, b_group_router, w_expert_router, b_expert_router, w_exp_gate, w_exp_up, w_exp_down):
    depth = w_in.shape[0]
    bsz = x_prompt.shape[0]
    y_p, y_s = x_prompt, x_sample
    outs_p, outs_s = [], []
    for l in range(depth):
        p = _prepare_params({
            'ln_mix_w': ln_mix_w[l], 'w_in': w_in[l], 'gdn_conv_w': gdn_conv_w[l],
            'gdn_a_log': gdn_a_log[l], 'gdn_dt_bias': gdn_dt_bias[l], 'gdn_norm_w': gdn_norm_w[l],
            'mla_q_a_norm_w': mla_q_a_norm_w[l], 'mla_w_uq': mla_w_uq[l],
            'mla_kv_a_norm_w': mla_kv_a_norm_w[l], 'mla_w_ukv': mla_w_ukv[l],
            'mla_q_norm_w': mla_q_norm_w[l], 'mla_k_norm_w': mla_k_norm_w[l],
            'w_out': w_out[l], 'ln_ffn_w': ln_ffn_w[l],
            'w_group_router': w_group_router[l], 'b_group_router': b_group_router[l],
            'w_expert_router': w_expert_router[l], 'b_expert_router': b_expert_router[l],
            'w_exp_gate': w_exp_gate[l], 'w_exp_up': w_exp_up[l], 'w_exp_down': w_exp_down[l],
        })
        y_p, *st_p = _hybrid_layer(
            y_p,
            jnp.zeros((bsz, CONV_W - 1, GDN_CONV_DIM), y_p.dtype),
            jnp.zeros((bsz, GDN_HEADS, GDN_DK, GDN_DV), state_gdn.dtype),
            jnp.zeros((bsz, 0, KV_LORA), cache_mla_ckv.dtype),
            jnp.zeros((bsz, 0, QK_ROPE), cache_mla_kpe.dtype),
            p)
        y_s, *st_s = _hybrid_layer(y_s, state_gdn_conv[l], state_gdn[l], cache_mla_ckv[l], cache_mla_kpe[l], p)
        outs_p.append(st_p)
        outs_s.append(st_s)
    stack = lambda outs, i: jnp.stack([o[i] for o in outs])
    return (y_p, y_s,
            stack(outs_p, 0), stack(outs_p, 1), stack(outs_p, 2), stack(outs_p, 3),
            stack(outs_s, 0), stack(outs_s, 1), stack(outs_s, 2), stack(outs_s, 3))
```

```python
import functools
import math

import jax
import jax.numpy as jnp
import numpy as np
from jax import lax
from jax.experimental import pallas as pl
from jax.experimental.pallas import tpu as pltpu

F32 = jnp.float32
BF16 = jnp.bfloat16

D_MODEL = 1024
CHUNK = 64
RMS_EPS = 1e-6

GDN_HEADS = 4
GDN_DK = 128
GDN_DV = 128
GDN_QK_DIM = GDN_HEADS * GDN_DK
GDN_V_DIM = GDN_HEADS * GDN_DV
GDN_CONV_DIM = 2 * GDN_QK_DIM + GDN_V_DIM
CONV_W = 4

MLA_HEADS = 4
Q_LORA = 512
KV_LORA = 256
QK_NOPE = 128
QK_ROPE = 64
ROPE_HALF = QK_ROPE // 2
V_HEAD = 128
V_ROWS = V_HEAD + 16
QK_HEAD = QK_NOPE + QK_ROPE
QK_PAD = QK_NOPE + 2 * QK_ROPE
ROPE_THETA = 10000.0

N_GROUPS = 4
EXPERTS_PER_GROUP = 8
N_EXPERTS = N_GROUPS * EXPERTS_PER_GROUP
D_EXPERT = 256

LANES = 128
VMEM_LIMIT_BYTES = 58 * 1024 * 1024

Q_SCALE = (QK_HEAD ** -0.5) * math.log2(math.e)
MASK_VALUE = -1e30
FLASH_HEADS_PER_STEP = 2
FLASH_RESIDENT_KV_BYTES = 8 * 1024 * 1024
GDN_SOLVE_PASSES = 1
TOKEN_SUBTILES = 2
MOE_SEG_ALIGN = 16
MOE_ROW_BLOCK = 128
MOE_EXPERTS_PER_STEP = 4


def _mm(a, b):
    return jnp.dot(a.astype(BF16), b.astype(BF16), preferred_element_type=F32)


def _mm_nt(a, b):
    return lax.dot_general(a.astype(BF16), b.astype(BF16), (((1,), (1,)), ((), ())),
                           preferred_element_type=F32)


def _mm_tn(a, b):
    return lax.dot_general(a.astype(BF16), b.astype(BF16), (((0,), (0,)), ((), ())),
                           preferred_element_type=F32)


def _split(a):
    hi = a.astype(BF16)
    lo = (a - hi.astype(F32)).astype(BF16)
    return hi, lo


def _mm3(a, b):
    ah, al = _split(a)
    bh, bl = _split(b)
    return _mm(ah, bh) + (_mm(ah, bl) + _mm(al, bh))


def _mm_solve(a, b):
    return _mm3(a, b) if GDN_SOLVE_PASSES == 3 else _mm(a, b)


def _mm_exact_rhs(a, b01):
    a1 = a.astype(BF16)
    r1 = a - a1.astype(F32)
    a2 = r1.astype(BF16)
    a3 = (r1 - a2.astype(F32)).astype(BF16)
    return _mm(a1, b01) + (_mm(a2, b01) + _mm(a3, b01))


def _mm2_exact_rhs(a, b01):
    hi, lo = _split(a)
    return _mm(hi, b01) + _mm(lo, b01)


def _mm_exact_lhs(a01, b):
    b1 = b.astype(BF16)
    r1 = b - b1.astype(F32)
    b2 = r1.astype(BF16)
    b3 = (r1 - b2.astype(F32)).astype(BF16)
    return _mm(a01, b1) + (_mm(a01, b2) + _mm(a01, b3))


def _rms(x, w):
    return x * lax.rsqrt(jnp.mean(x * x, axis=-1, keepdims=True) + RMS_EPS) * w


def _silu(x):
    return x * jax.nn.sigmoid(x)


def _swap_halves(x):
    h = x.shape[-1] // 2
    return jnp.concatenate([x[:, h:], x[:, :h]], axis=-1)


def _subtiles(tm):
    n = tm // TOKEN_SUBTILES
    return [pl.ds(i * n, n) for i in range(TOKEN_SUBTILES)]


def _const_spec(shape):
    nd = len(shape)
    return pl.BlockSpec(shape, lambda *_: (0,) * nd, pipeline_mode=pl.Buffered(1))


def _params(*sem):
    return pltpu.CompilerParams(dimension_semantics=sem, vmem_limit_bytes=VMEM_LIMIT_BYTES)


def _inproj_kernel(x_ref, lnw_ref, wqkv_ref, wz_ref, wba_ref, alog_ref, dtb_ref, wcq_ref, qanw_ref,
                   wuq_ref, qnw_ref, hsum_ref, swap_ref, cos_ref, sin_ref, wckv_ref, kvanw_ref, wkpe_ref,
                   wukv_ref, knw_ref, qkv_ref, z_ref, gates_ref, q_ref, ckv_ref, kpe_ref, *kv_refs):
    subs = _subtiles(x_ref.shape[0])
    h_ = [_rms(x_ref[r, :], lnw_ref[...]) for r in subs]
    hb_ = [h.astype(BF16) for h in h_]
    raw_, ckv_, cq_, kpe_ = [], [], [], []
    for r, h, hb in zip(subs, h_, hb_):
        qkv_ref[r, :] = _mm(hb, wqkv_ref[...])
        z_ref[r, :] = _mm(hb, wz_ref[...])
        kpe_.append(_mm(hb, wkpe_ref[...]))
        kpe_ref[r, :] = kpe_[-1]
        raw_.append(_mm3(h, wba_ref[...]))
        ckv_.append(_mm(hb, wckv_ref[...]))
        cq_.append(_mm(hb, wcq_ref[...]))

    qall_ = [_mm(_rms(cq, qanw_ref[...]), wuq_ref[...]) for cq in cq_]

    for i, (r, raw, ckv) in enumerate(zip(subs, raw_, ckv_)):
        lane = lax.broadcasted_iota(jnp.int32, raw.shape, 1)
        pre = raw + dtb_ref[...]
        softplus = jnp.maximum(pre, 0.0) + jnp.log1p(jnp.exp(-jnp.abs(pre)))
        g = -jnp.exp(alog_ref[...]) * softplus
        gates_ref[r, :] = jnp.where(lane < GDN_HEADS, jax.nn.sigmoid(raw), g)
        ckv_n = _rms(ckv, kvanw_ref[...])
        ckv_ref[r, :] = ckv_n
        if kv_refs:
            k_ref, vt_ref = kv_refs
            heads = _keys_values(ckv_n, kpe_[i], cos_ref[r, :], sin_ref[r, :], wukv_ref[...], knw_ref[...])
            for hd, (kh, vt) in enumerate(heads):
                k_ref[hd, r, :] = kh
                vt_ref[hd, i] = vt

    rope0 = MLA_HEADS * QK_NOPE
    lane = lax.broadcasted_iota(jnp.int32, (1, 2 * QK_ROPE), 1)
    for r, qall in zip(subs, qall_):
        ssq = _mm(qall * qall, hsum_ref[...])
        qn = qall * (lax.rsqrt(ssq * (1.0 / QK_HEAD) + RMS_EPS) * Q_SCALE) * qnw_ref[...]
        rw = qn[:, rope0:]
        cos2 = jnp.concatenate([cos_ref[r, :]] * 2, axis=-1)
        sin2 = jnp.concatenate([sin_ref[r, :]] * 2, axis=-1)
        rot = rw * cos2 + _mm2_exact_rhs(rw, swap_ref[...]) * sin2
        for hd in range(MLA_HEADS):
            pair = rot[:, (hd // 2) * 2 * QK_ROPE:(hd // 2 + 1) * 2 * QK_ROPE]
            mine = jnp.where((lane // QK_ROPE) == (hd % 2), pair, 0.0)
            q_ref[hd, r, :] = jnp.concatenate([qn[:, hd * QK_NOPE:(hd + 1) * QK_NOPE], mine], axis=-1).astype(BF16)


def _inproj(x2d, p, rope_q, tm, key_tile=None):
    t = x2d.shape[0]
    nt = t // tm
    cos_t, sin_t = rope_q
    n_rope_blocks = cos_t.shape[0] // tm
    rope_spec = pl.BlockSpec((tm, 2 * QK_ROPE), lambda i: (i % n_rope_blocks, 0))
    row = lambda n: pl.BlockSpec((tm, n), lambda i: (i, 0))
    consts = [p['ln_mix_w'], p['w_qkv'], p['w_z'], p['w_ba'], p['a_log'], p['dt_bias'], p['w_cq'],
              p['q_a_norm_w'], p['w_uq'], p['q_norm_w'], p['q_head_sum'], p['rope_swap']]
    consts2 = [p['w_ckv'], p['kv_a_norm_w'], p['w_kpe'], p['w_ukv'], p['k_norm_w']]
    in_specs = ([row(D_MODEL)] + [_const_spec(c.shape) for c in consts] + [rope_spec, rope_spec]
                + [_const_spec(c.shape) for c in consts2])
    out_shape = (
        jax.ShapeDtypeStruct((t, GDN_CONV_DIM), F32),
        jax.ShapeDtypeStruct((t, GDN_V_DIM), F32),
        jax.ShapeDtypeStruct((t, LANES), F32),
        jax.ShapeDtypeStruct((MLA_HEADS, t, QK_PAD), BF16),
        jax.ShapeDtypeStruct((t, KV_LORA), F32),
        jax.ShapeDtypeStruct((t, QK_ROPE), F32),
    )
    out_specs = (row(GDN_CONV_DIM), row(GDN_V_DIM), row(LANES),
                 pl.BlockSpec((MLA_HEADS, tm, QK_PAD), lambda i: (0, i, 0)),
                 row(KV_LORA), row(QK_ROPE))
    if key_tile is not None:
        assert tm // TOKEN_SUBTILES == key_tile
        out_shape += (jax.ShapeDtypeStruct((MLA_HEADS, t, QK_PAD), BF16),
                      jax.ShapeDtypeStruct((MLA_HEADS, t // key_tile, V_ROWS, key_tile), BF16))
        out_specs += (pl.BlockSpec((MLA_HEADS, tm, QK_PAD), lambda i: (0, i, 0)),
                      pl.BlockSpec((MLA_HEADS, TOKEN_SUBTILES, V_ROWS, key_tile), lambda i: (0, i, 0, 0)))
    return pl.pallas_call(
        _inproj_kernel, grid=(nt,), in_specs=in_specs, out_specs=out_specs, out_shape=out_shape,
        compiler_params=_params("parallel"), name="inproj",
    )(x2d, *consts, cos_t, sin_t, *consts2)


def _keys_values(ckv, kpe, cosf, sinf, wukv, knw):
    kv = _mm(ckv, wukv)
    kw = kpe * knw[:, QK_NOPE:]
    krot = kw * cosf[:, :QK_ROPE] + _swap_halves(kw) * sinf[:, :QK_ROPE]
    krot = jnp.concatenate([krot, krot], axis=-1)
    s_kpe = jnp.sum(kpe * kpe, axis=-1, keepdims=True)
    v0 = MLA_HEADS * QK_NOPE
    pad_row = lax.broadcasted_iota(jnp.int32, (V_ROWS - V_HEAD, kv.shape[0]), 0)
    ones_rows = jnp.where(pad_row == 0, 1.0, 0.0)
    out = []
    for hd in range(MLA_HEADS):
        kn = kv[:, hd * QK_NOPE:(hd + 1) * QK_NOPE]
        inv = lax.rsqrt((jnp.sum(kn * kn, axis=-1, keepdims=True) + s_kpe) * (1.0 / QK_HEAD) + RMS_EPS)
        kh = jnp.concatenate([kn * knw[:, :QK_NOPE], krot], axis=-1) * inv
        vt = kv[:, v0 + hd * V_HEAD: v0 + (hd + 1) * V_HEAD].T
        out.append((kh.astype(BF16), jnp.concatenate([vt, ones_rows], axis=0).astype(BF16)))
    return out


def _kvprep_kernel(ckv_old_ref, kpe_old_ref, ckv_new_ref, kpe_new_ref, cos_ref, sin_ref, wukv_ref, knw_ref,
                   k_ref, vt_ref):
    n_pad = k_ref.shape[1] - ckv_old_ref.shape[1] - ckv_new_ref.shape[1]

    def rows(old_ref, new_ref):
        return jnp.concatenate([old_ref[0], new_ref[0], jnp.zeros((n_pad, old_ref.shape[2]), F32)], axis=0)

    heads = _keys_values(rows(ckv_old_ref, ckv_new_ref), rows(kpe_old_ref, kpe_new_ref),
                         cos_ref[...], sin_ref[...], wukv_ref[...], knw_ref[...])
    for hd, (kh, vt) in enumerate(heads):
        k_ref[hd] = kh
        vt_ref[hd, 0] = vt


def _kvprep(ckv_old, kpe_old, ckv_new, kpe_new, p, rope_k, lk):
    bsz = ckv_old.shape[0]
    cos_t, sin_t = rope_k
    per_seq = lambda a: pl.BlockSpec((1,) + a.shape[1:], lambda b: (b, 0, 0))
    return pl.pallas_call(
        _kvprep_kernel, grid=(bsz,),
        in_specs=[per_seq(ckv_old), per_seq(kpe_old), per_seq(ckv_new), per_seq(kpe_new),
                  _const_spec(cos_t.shape), _const_spec(sin_t.shape),
                  _const_spec(p['w_ukv'].shape), _const_spec(p['k_norm_w'].shape)],
        out_specs=(pl.BlockSpec((MLA_HEADS, lk, QK_PAD), lambda b: (0, b, 0)),
                   pl.BlockSpec((MLA_HEADS, 1, V_ROWS, lk), lambda b: (0, b, 0, 0))),
        out_shape=(jax.ShapeDtypeStruct((MLA_HEADS, bsz * lk, QK_PAD), BF16),
                   jax.ShapeDtypeStruct((MLA_HEADS, bsz, V_ROWS, lk), BF16)),
        compiler_params=_params("parallel"), name="kvprep",
    )(ckv_old, kpe_old, ckv_new, kpe_new, cos_t, sin_t, p['w_ukv'], p['k_norm_w'])


def _flash_kernel(q_ref, k_ref, vt_ref, o_ref, m_sc, acc_sc, *, tq, tk, nk, past, hps, qsplit):
    i = pl.program_id(2)
    q_first = past + i * tq
    q_last = q_first + tq - 1
    n_full = jnp.minimum(nk, ((q_first // CHUNK + 1) * CHUNK) // tk)
    n_vis = jnp.minimum(nk, ((q_last // CHUNK + 1) * CHUNK + tk - 1) // tk)

    m_sc[...] = jnp.full(m_sc.shape, MASK_VALUE, F32)
    acc_sc[...] = jnp.zeros(acc_sc.shape, F32)

    tqs = tq // qsplit
    streams = [(hd, pl.ds(c * tqs, tqs)) for hd in range(hps) for c in range(qsplit)]

    def step(j, masked):
        k0 = pl.multiple_of(j * tk, tk)
        s_ = [_mm_nt(k_ref[hd, pl.ds(k0, tk), :], q_ref[hd, cols, :]) for hd, cols in streams]
        if masked:
            kpos = k0 + lax.broadcasted_iota(jnp.int32, (tk, 1), 0)
            for n, (hd, cols) in enumerate(streams):
                qpos = q_first + (n % qsplit) * tqs + lax.broadcasted_iota(jnp.int32, (1, tqs), 1)
                s_[n] = jnp.where((kpos // CHUNK) <= (qpos // CHUNK), s_[n], MASK_VALUE)
        p_, alpha_ = [], []
        for n, (hd, cols) in enumerate(streams):
            m_old = m_sc[hd, :, cols]
            m_new = jnp.maximum(m_old, jnp.max(s_[n], axis=0, keepdims=True))
            alpha_.append(jnp.exp2(m_old - m_new))
            p_.append(jnp.exp2(s_[n] - m_new).astype(BF16))
            m_sc[hd, :, cols] = m_new
        for n, (hd, cols) in enumerate(streams):
            acc_sc[hd, :, cols] = alpha_[n] * acc_sc[hd, :, cols] + _mm(vt_ref[hd, j], p_[n])

    def full_body(j, c):
        step(j, False)
        return c

    def masked_body(j, c):
        step(j, True)
        return c

    lax.fori_loop(0, n_full, full_body, 0)
    lax.fori_loop(n_full, n_vis, masked_body, 0)
    for hd in range(hps):
        o_t = acc_sc[hd, :V_HEAD, :] / acc_sc[hd, V_HEAD:V_HEAD + 1, :]
        o_ref[:, hd * V_HEAD:(hd + 1) * V_HEAD] = o_t.T.astype(o_ref.dtype)


def _flash(q, k, vt, bsz, lq, lk, past, tq, tk, hps, qsplit):
    nq = lq // tq
    nk = lk // tk
    kern = functools.partial(_flash_kernel, tq=tq, tk=tk, nk=nk, past=past, hps=hps, qsplit=qsplit)
    once = pl.Buffered(1 if nq > 1 else 2)
    return pl.pallas_call(
        kern, grid=(bsz, MLA_HEADS // hps, nq),
        in_specs=[pl.BlockSpec((hps, tq, QK_PAD), lambda b, h, i: (h, b * nq + i, 0)),
                  pl.BlockSpec((hps, lk, QK_PAD), lambda b, h, i: (h, b, 0), pipeline_mode=once),
                  pl.BlockSpec((hps, nk, V_ROWS, tk), lambda b, h, i: (h, b, 0, 0), pipeline_mode=once)],
        out_specs=pl.BlockSpec((tq, hps * V_HEAD), lambda b, h, i: (b * nq + i, h)),
        out_shape=jax.ShapeDtypeStruct((bsz * lq, MLA_HEADS * V_HEAD), BF16),
        scratch_shapes=[pltpu.VMEM((hps, 1, tq), F32), pltpu.VMEM((hps, V_ROWS, tq), F32)],
        compiler_params=_params("parallel", "parallel", "arbitrary"), name="flash",
    )(q, k, vt)


def _gdn_kernel(qkv_ref, z_ref, gates_ref, hist_ref, s0_ref, convw_ref, normw_ref,
                o_ref, s_ref, xbuf, *, tb, nseq):
    j = pl.program_id(1)
    nc = tb // CHUNK
    pad = 8
    seqs = range(nseq)
    streams = [(b, hd) for b in seqs for hd in range(GDN_HEADS)]

    @pl.when(j == 0)
    def _():
        xbuf[:, 0:pad, :] = hist_ref[...]
        s_ref[...] = s0_ref[...]

    cw = convw_ref[...]
    y_ = []
    for b in seqs:
        xbuf[b, pad:pad + tb, :] = qkv_ref[b]
        y = xbuf[b, pad - 3:pad - 3 + tb, :] * cw[0:1, :]
        for i in range(1, CONV_W):
            y = y + xbuf[b, pad - 3 + i:pad - 3 + i + tb, :] * cw[i:i + 1, :]
        xbuf[b, 0:pad, :] = xbuf[b, tb:tb + pad, :]
        y_.append(_silu(y))

    row = lax.broadcasted_iota(jnp.int32, (tb, tb), 0)
    col = lax.broadcasted_iota(jnp.int32, (tb, tb), 1)
    same = (row // CHUNK) == (col // CHUNK)
    incl = same & (row >= col)
    strict = same & (row > col)
    tril01 = jnp.where(incl, 1.0, 0.0).astype(BF16)
    triu01 = jnp.where(same & (row <= col), 1.0, 0.0).astype(BF16)
    eye = jnp.where(row == col, 1.0, 0.0)
    gates_ = [gates_ref[b] for b in seqs]
    gc_cols_ = [_mm_exact_lhs(tril01, g) for g in gates_]
    gc_rows_ = [_mm_exact_rhs(g.T[0:8, :], triu01) for g in gates_]

    q_, k_, v_, beta_, gc_, decay_, kb_ = [], [], [], [], [], [], []
    for n, (b, hd) in enumerate(streams):
        y = y_[b]
        qh = y[:, hd * GDN_DK:(hd + 1) * GDN_DK]
        kh = y[:, GDN_QK_DIM + hd * GDN_DK: GDN_QK_DIM + (hd + 1) * GDN_DK]
        q_.append(qh * (lax.rsqrt(jnp.sum(qh * qh, axis=-1, keepdims=True) + RMS_EPS) * (GDN_DK ** -0.5)))
        k_.append(kh * lax.rsqrt(jnp.sum(kh * kh, axis=-1, keepdims=True) + RMS_EPS))
        v_.append(y[:, 2 * GDN_QK_DIM + hd * GDN_DV: 2 * GDN_QK_DIM + (hd + 1) * GDN_DV])
        beta_.append(gates_[b][:, hd:hd + 1])
        gc = gc_cols_[b][:, GDN_HEADS + hd: GDN_HEADS + hd + 1]
        gr = gc_rows_[b][GDN_HEADS + hd: GDN_HEADS + hd + 1, :]
        gc_.append(gc)
        decay_.append(jnp.where(incl, jnp.exp(jnp.where(incl, gc - gr, 0.0)), 0.0))
        kb_.append(k_[n] * beta_[n])
    ns = range(len(streams))
    khb_ = [k.astype(BF16) for k in k_]
    pw_ = [jnp.where(strict, -(_mm_nt(kb_[n], khb_[n]) * decay_[n]), 0.0) for n in ns]
    attn_ = [_mm_nt(q_[n], khb_[n]) * decay_[n] for n in ns]

    tinv_ = [eye + pw for pw in pw_]
    for _ in range(int(math.log2(CHUNK)) - 1):
        pw_ = [_mm_solve(pw, pw) for pw in pw_]
        tinv_ = [tinv_[n] + _mm_solve(tinv_[n], pw_[n]) for n in ns]

    eg_ = [jnp.exp(gc) for gc in gc_]
    sol_ = [_mm_solve(tinv_[n], jnp.concatenate([v_[n] * beta_[n], kb_[n] * eg_[n]], axis=-1)) for n in ns]
    qdec_ = [q_[n] * eg_[n] for n in ns]

    s_ = [s_ref[b, hd] for b, hd in streams]
    us_ = [[] for _ in ns]
    oi_ = [[] for _ in ns]
    for c in range(nc):
        r0, r1 = c * CHUNK, (c + 1) * CHUNK
        for n in ns:
            g_end = gc_[n][r1 - 1:r1, :]
            k_end = k_[n][r0:r1] * jnp.exp(g_end - gc_[n][r0:r1])
            both = _mm(jnp.concatenate([sol_[n][r0:r1, GDN_DV:], qdec_[n][r0:r1]], axis=0), s_[n])
            u = sol_[n][r0:r1, :GDN_DV] - both[:CHUNK]
            oi_[n].append(both[CHUNK:])
            us_[n].append(u)
            s_[n] = s_[n] * jnp.exp(g_end) + _mm_tn(k_end, u)
    for n, (b, hd) in enumerate(streams):
        s_ref[b, hd] = s_[n]
        u_all = us_[n][0] if nc == 1 else jnp.concatenate(us_[n], axis=0)
        o_all = oi_[n][0] if nc == 1 else jnp.concatenate(oi_[n], axis=0)
        o = o_all + _mm(attn_[n], u_all)
        o = _rms(o, normw_ref[...]) * _silu(z_ref[b, :, hd * GDN_DV:(hd + 1) * GDN_DV])
        o_ref[b, :, hd * GDN_DV:(hd + 1) * GDN_DV] = o.astype(o_ref.dtype)


def _gdn(qkv2d, z2d, gates2d, hist_pad, s0, p, bsz, seq, tb, nseq):
    nb = seq // tb
    kern = functools.partial(_gdn_kernel, tb=tb, nseq=nseq)
    row = lambda n: pl.BlockSpec((nseq, tb, n), lambda g, j: (g, j, 0))
    state_spec = pl.BlockSpec((nseq, GDN_HEADS, GDN_DK, GDN_DV), lambda g, j: (g, 0, 0, 0))
    view = lambda a: a.reshape(bsz, seq, a.shape[-1])
    out, s_new = pl.pallas_call(
        kern, grid=(bsz // nseq, nb),
        in_specs=[row(GDN_CONV_DIM), row(GDN_V_DIM), row(LANES),
                  pl.BlockSpec((nseq, 8, GDN_CONV_DIM), lambda g, j: (g, 0, 0)),
                  state_spec, _const_spec(p['conv_w'].shape), _const_spec(p['gdn_norm_w'].shape)],
        out_specs=(row(GDN_V_DIM), state_spec),
        out_shape=(jax.ShapeDtypeStruct((bsz, seq, GDN_V_DIM), BF16),
                   jax.ShapeDtypeStruct((bsz, GDN_HEADS, GDN_DK, GDN_DV), F32)),
        scratch_shapes=[pltpu.VMEM((nseq, tb + 8, GDN_CONV_DIM), F32)],
        compiler_params=_params("parallel", "arbitrary"), name="gdn",
    )(view(qkv2d), view(z2d), view(gates2d), hist_pad, s0, p['conv_w'], p['gdn_norm_w'])
    return out.reshape(bsz * seq, GDN_V_DIM), s_new


def _outproj_kernel(x_ref, g_ref, a_ref, wog_ref, woa_ref, lnw_ref, wr_ref, br_ref,
                    x1_ref, h2_ref, comb_ref):
    subs = _subtiles(x_ref.shape[0])
    h2_ = []
    for r in subs:
        x1 = x_ref[r, :] + _mm(g_ref[r, :], wog_ref[...]) + _mm(a_ref[r, :], woa_ref[...])
        x1_ref[r, :] = x1
        h2_.append(_rms(x1, lnw_ref[...]))
        h2_ref[r, :] = h2_[-1].astype(BF16)
    logits_ = [_mm3(h2, wr_ref[...]) + br_ref[...] for h2 in h2_]
    for r, logits in zip(subs, logits_):
        comb_ref[r, :] = _route(logits)


def _route(logits):
    lane = lax.broadcasted_iota(jnp.int32, logits.shape, 1)
    big = jnp.int32(LANES)
    is_grp = (lane >= N_EXPERTS) & (lane < N_EXPERTS + N_GROUPS)
    gl = jnp.where(is_grp, logits, -jnp.inf)
    gmax = jnp.max(gl, axis=-1, keepdims=True)
    grp_p = 1.0 / jnp.sum(jnp.exp(gl - gmax), axis=-1, keepdims=True)
    grp_idx = jnp.min(jnp.where(gl == gmax, lane - N_EXPERTS, big), axis=-1, keepdims=True)

    in_grp = (lane < N_EXPERTS) & ((lane // EXPERTS_PER_GROUP) == grp_idx)
    el = jnp.where(in_grp, logits, -jnp.inf)
    m1 = jnp.max(el, axis=-1, keepdims=True)
    i1 = jnp.min(jnp.where(el == m1, lane, big), axis=-1, keepdims=True)
    el2 = jnp.where(lane == i1, -jnp.inf, el)
    m2 = jnp.max(el2, axis=-1, keepdims=True)
    i2 = jnp.min(jnp.where(el2 == m2, lane, big), axis=-1, keepdims=True)
    e2 = jnp.exp(m2 - m1)
    w1 = 1.0 / (1.0 + e2)
    w2 = e2 * w1
    return jnp.where(lane == 0, i1.astype(F32), jnp.where(lane == 1, i2.astype(F32),
                     jnp.where(lane == 2, grp_p * w1, jnp.where(lane == 3, grp_p * w2, 0.0))))


def _outproj(x2d, gdn_out, mla_out, p, tm):
    t = x2d.shape[0]
    row = lambda n: pl.BlockSpec((tm, n), lambda i: (i, 0))
    consts = [p['w_out_gdn'], p['w_out_mla'], p['ln_ffn_w'], p['w_router'], p['b_router']]
    return pl.pallas_call(
        _outproj_kernel, grid=(t // tm,),
        in_specs=[row(D_MODEL), row(GDN_V_DIM), row(MLA_HEADS * V_HEAD)] + [_const_spec(c.shape) for c in consts],
        out_specs=(row(D_MODEL), row(D_MODEL), row(LANES)),
        out_shape=(jax.ShapeDtypeStruct((t, D_MODEL), F32), jax.ShapeDtypeStruct((t, D_MODEL), BF16),
                   jax.ShapeDtypeStruct((t, LANES), F32)),
        compiler_params=_params("parallel"), name="outproj",
    )(x2d, gdn_out, mla_out, *consts)


def _moe_slots(tm):
    n = 2 * tm + N_EXPERTS * MOE_SEG_ALIGN + MOE_ROW_BLOCK
    return -(-n // LANES) * LANES


def _moe_kernel(h_ref, route_ref, x1_ref, wg_ref, wu_ref, wd_ref, y_ref, xs_sc, ys_sc, dcol_sc, offs_ref,
                *, tm, slots, n_chunks):
    g = pl.program_id(1)
    chunk = slots // n_chunks
    lb = min(2 * LANES, tm)
    nlb = tm // lb

    @pl.when(g == 0)
    def _sort_picks():
        rt = route_ref[...].T
        eid = lax.broadcasted_iota(jnp.int32, (N_EXPERTS, lb), 0).astype(F32)
        pieces = [jnp.where(rt[k:k + 1, b * lb:(b + 1) * lb] == eid, 1.0, 0.0)
                  for k in range(2) for b in range(nlb)]
        ind = jnp.concatenate(pieces, axis=0)
        r_i = lax.broadcasted_iota(jnp.int32, (lb, lb), 0)
        c_i = lax.broadcasted_iota(jnp.int32, (lb, lb), 1)
        before = _mm(ind, jnp.where(r_i < c_i, 1.0, 0.0))
        tot = jnp.sum(ind, axis=1, keepdims=True)
        run = jnp.zeros((N_EXPERTS, 1), F32)
        runs = []
        for i in range(len(pieces)):
            runs.append(run)
            run = run + tot[i * N_EXPERTS:(i + 1) * N_EXPERTS]
        seg = jnp.floor((run + (MOE_SEG_ALIGN - 1)) * (1.0 / MOE_SEG_ALIGN)) * MOE_SEG_ALIGN
        er = lax.broadcasted_iota(jnp.int32, (N_EXPERTS, N_EXPERTS), 0)
        ec = lax.broadcasted_iota(jnp.int32, (N_EXPERTS, N_EXPERTS), 1)
        start = _mm_exact_lhs(jnp.where(er > ec, 1.0, 0.0).astype(BF16),
                              jnp.broadcast_to(seg, (N_EXPERTS, LANES)))[:, 0:1]
        start_i = start.astype(jnp.int32)
        for e in range(N_EXPERTS):
            offs_ref[e] = start_i[e, 0]
        offs_ref[N_EXPERTS] = start_i[N_EXPERTS - 1, 0] + seg.astype(jnp.int32)[N_EXPERTS - 1, 0]

        dest = []
        for k in range(2):
            cols = []
            for b in range(nlb):
                i = k * nlb + b
                piece = ind[i * N_EXPERTS:(i + 1) * N_EXPERTS]
                pos = start + runs[i] + before[i * N_EXPERTS:(i + 1) * N_EXPERTS]
                cols.append(jnp.sum(piece * pos, axis=0, keepdims=True))
            dest.append(cols[0] if nlb == 1 else jnp.concatenate(cols, axis=1))
        rec = jnp.concatenate([dest[0], dest[1], rt[2:4], jnp.zeros((LANES - 4, tm), F32)], axis=0)
        dcol_sc[...] = rec.T

        h = h_ref[...]
        for c in range(n_chunks):
            s_i = (c * chunk + lax.broadcasted_iota(jnp.int32, (chunk, 1), 0)).astype(F32)
            sel = jnp.where(s_i == dest[0], 1.0, jnp.where(s_i == dest[1], 1.0, 0.0))
            xs_sc[c * chunk:(c + 1) * chunk, :] = _mm(sel, h).astype(BF16)
        ys_sc[...] = jnp.zeros(ys_sc.shape, BF16)

    steps = range(MOE_EXPERTS_PER_STEP)
    offs = [offs_ref[g * MOE_EXPERTS_PER_STEP + q] for q in steps] + [offs_ref[(g + 1) * MOE_EXPERTS_PER_STEP]]

    def rows(q, b):
        return pl.ds(pl.multiple_of(offs[q] + b * MOE_ROW_BLOCK, MOE_SEG_ALIGN), MOE_ROW_BLOCK)

    for q in steps:
        n_blocks = (offs[q + 1] - offs[q] + (MOE_ROW_BLOCK - 1)) // MOE_ROW_BLOCK

        def block(b, carry, q=q):
            xb = xs_sc[rows(q, b), :]
            act = _silu(_mm(xb, wg_ref[q])) * _mm(xb, wu_ref[q])
            ys_sc[rows(q, b), :] = _mm(act, wd_ref[q]).astype(BF16)
            return carry

        lax.fori_loop(1, n_blocks, block, 0)

    xb_ = [xs_sc[rows(q, 0), :] for q in steps]
    gate_ = [_mm(xb_[q], wg_ref[q]) for q in steps]
    up_ = [_mm(xb_[q], wu_ref[q]) for q in steps]
    act_ = [_silu(gate_[q]) * up_[q] for q in steps]
    out_ = [_mm(act_[q], wd_ref[q]).astype(BF16) for q in steps]
    for q in steps:
        ys_sc[rows(q, 0), :] = out_[q]

    @pl.when(g == pl.num_programs(1) - 1)
    def _combine():
        rec = dcol_sc[...]
        d0, d1, c0, c1 = rec[:, 0:1], rec[:, 1:2], rec[:, 2:3], rec[:, 3:4]
        y = x1_ref[...]
        for c in range(n_chunks):
            s_i = (c * chunk + lax.broadcasted_iota(jnp.int32, (1, chunk), 1)).astype(F32)
            wsel = jnp.where(s_i == d0, c0, jnp.where(s_i == d1, c1, 0.0))
            y = y + _mm(wsel, ys_sc[c * chunk:(c + 1) * chunk, :])
        y_ref[...] = y


def _moe(h2, route, x1, p, tm):
    t = h2.shape[0]
    slots = _moe_slots(tm)
    n_chunks = 3 if slots % (3 * LANES) == 0 else 1
    kern = functools.partial(_moe_kernel, tm=tm, slots=slots, n_chunks=n_chunks)
    row = lambda n: pl.BlockSpec((tm, n), lambda i, g: (i, 0))
    eps = MOE_EXPERTS_PER_STEP
    return pl.pallas_call(
        kern, grid=(t // tm, N_EXPERTS // eps),
        in_specs=[row(D_MODEL), row(LANES), row(D_MODEL),
                  pl.BlockSpec((eps, D_MODEL, D_EXPERT), lambda i, g: (g, 0, 0)),
                  pl.BlockSpec((eps, D_MODEL, D_EXPERT), lambda i, g: (g, 0, 0)),
                  pl.BlockSpec((eps, D_EXPERT, D_MODEL), lambda i, g: (g, 0, 0))],
        out_specs=pl.BlockSpec((tm, D_MODEL), lambda i, g: (i, 0)),
        out_shape=jax.ShapeDtypeStruct((t, D_MODEL), F32),
        scratch_shapes=[pltpu.VMEM((slots, D_MODEL), BF16), pltpu.VMEM((slots, D_MODEL), BF16),
                        pltpu.VMEM((tm, LANES), F32), pltpu.SMEM((N_EXPERTS + 1,), jnp.int32)],
        compiler_params=_params("parallel", "arbitrary"), name="moe",
    )(h2, route, x1, p['w_exp_gate'], p['w_exp_up'], p['w_exp_down'])


def _rope_tables(pos):
    inv = ROPE_THETA ** (-jnp.arange(ROPE_HALF, dtype=F32) / ROPE_HALF)
    per_row = LANES // ROPE_HALF
    n = pos.shape[0]
    assert n % per_row == 0
    pos_dense = jnp.repeat(pos.reshape(n // per_row, per_row), ROPE_HALF, axis=1)
    ang = pos_dense.astype(F32) * jnp.tile(inv, per_row)[None, :]
    cos, sin = jnp.cos(ang).reshape(n, ROPE_HALF), jnp.sin(ang).reshape(n, ROPE_HALF)
    return jnp.concatenate([cos, cos] * 2, axis=-1), jnp.concatenate([-sin, sin] * 2, axis=-1)


def _tile_rows(tables, tm):
    n = tables[0].shape[0]
    if n >= tm:
        assert n % tm == 0
        return tables
    assert tm % n == 0
    return tuple(jnp.tile(t, (tm // n, 1)) for t in tables)


def _pad_lanes(v, offset):
    return jnp.zeros((1, LANES), F32).at[0, offset:offset + v.shape[0]].set(v.astype(F32))


def _q_layout_constants():
    n_nope = MLA_HEADS * QK_NOPE
    col = np.arange(n_nope + MLA_HEADS * QK_ROPE)
    head = np.where(col < n_nope, col // QK_NOPE, (col - n_nope) // QK_ROPE)
    same_head = (head[:, None] == head[None, :]).astype(np.float32)
    j = np.arange(MLA_HEADS * QK_ROPE)
    src = (j // QK_ROPE) * QK_ROPE + (j % QK_ROPE + ROPE_HALF) % QK_ROPE
    swap = (j[:, None] == src[None, :]).astype(np.float32)
    return same_head, swap


_Q_HEAD_SUM, _ROPE_SWAP = _q_layout_constants()


def _prepare_params(p):
    w_in = p['w_in']
    sizes = (GDN_CONV_DIM, GDN_V_DIM, GDN_HEADS, GDN_HEADS, Q_LORA, KV_LORA, QK_ROPE)
    offs = np.concatenate([[0], np.cumsum(sizes)])
    part = [w_in[:, int(offs[i]):int(offs[i + 1])] for i in range(len(sizes))]
    w_ba = jnp.zeros((D_MODEL, LANES), F32).at[:, :2 * GDN_HEADS].set(jnp.concatenate([part[2], part[3]], axis=1))

    uq = p['mla_w_uq'].reshape(Q_LORA, MLA_HEADS, QK_HEAD)
    w_uq = jnp.concatenate([uq[:, :, :QK_NOPE].reshape(Q_LORA, -1), uq[:, :, QK_NOPE:].reshape(Q_LORA, -1)], axis=1)
    ukv = p['mla_w_ukv'].reshape(KV_LORA, MLA_HEADS, QK_NOPE + V_HEAD)
    w_ukv = jnp.concatenate([ukv[:, :, :QK_NOPE].reshape(KV_LORA, -1), ukv[:, :, QK_NOPE:].reshape(KV_LORA, -1)], axis=1)

    w_router = jnp.zeros((D_MODEL, LANES), F32)
    w_router = w_router.at[:, :N_EXPERTS].set(p['w_expert_router'])
    w_router = w_router.at[:, N_EXPERTS:N_EXPERTS + N_GROUPS].set(p['w_group_router'])
    b_router = _pad_lanes(jnp.concatenate([p['b_expert_router'], p['b_group_router']]), 0)

    row = lambda v: v.astype(F32)[None, :]
    return {
        'ln_mix_w': row(p['ln_mix_w']),
        'w_qkv': part[0].astype(BF16), 'w_z': part[1].astype(BF16), 'w_ba': w_ba,
        'a_log': _pad_lanes(p['gdn_a_log'], GDN_HEADS), 'dt_bias': _pad_lanes(p['gdn_dt_bias'], GDN_HEADS),
        'w_cq': part[4].astype(BF16), 'w_ckv': part[5].astype(BF16), 'w_kpe': part[6].astype(BF16),
        'q_a_norm_w': row(p['mla_q_a_norm_w']), 'kv_a_norm_w': row(p['mla_kv_a_norm_w']),
        'w_uq': w_uq.astype(BF16), 'w_ukv': w_ukv.astype(BF16),
        'q_norm_w': row(jnp.concatenate([jnp.tile(p['mla_q_norm_w'][:QK_NOPE], MLA_HEADS),
                                         jnp.tile(p['mla_q_norm_w'][QK_NOPE:], MLA_HEADS)])),
        'q_head_sum': jnp.asarray(_Q_HEAD_SUM, BF16), 'rope_swap': jnp.asarray(_ROPE_SWAP, BF16),
        'k_norm_w': row(p['mla_k_norm_w']),
        'conv_w': p['gdn_conv_w'].astype(F32), 'gdn_norm_w': row(p['gdn_norm_w']),
        'w_out_gdn': p['w_out'][:GDN_V_DIM].astype(BF16), 'w_out_mla': p['w_out'][GDN_V_DIM:].astype(BF16),
        'ln_ffn_w': row(p['ln_ffn_w']), 'w_router': w_router, 'b_router': b_router,
        'w_exp_gate': p['w_exp_gate'].astype(BF16), 'w_exp_up': p['w_exp_up'].astype(BF16),
        'w_exp_down': p['w_exp_down'].astype(BF16),
    }


def _tiles(bsz, seq, past):
    t = bsz * seq
    tm = min(1024, t)
    n_keys = past + seq
    if past == 0:
        tk = min(tm // TOKEN_SUBTILES, seq)
        tq = min(2 * tk, seq)
        lk = n_keys
    else:
        assert n_keys % CHUNK == 0
        tq = seq
        lk = -(-n_keys // LANES) * LANES
        tk = lk
    kv_bytes_per_head = lk * (QK_PAD + V_ROWS) * 2
    hps = MLA_HEADS if MLA_HEADS * kv_bytes_per_head <= FLASH_RESIDENT_KV_BYTES else FLASH_HEADS_PER_STEP
    tb = min(256, seq)
    gseq = 2 if bsz % 2 == 0 else 1
    tmoe = min(1024, t)
    assert seq % tq == 0 and lk % tk == 0 and seq % tb == 0 and tb % CHUNK == 0 and t % tm == 0 and t % tmoe == 0
    return tm, tq, tk, lk, hps, tb, gseq, tmoe


def _hybrid_layer(x, conv_hist, s0, ckv_past, kpe_past, p):
    bsz, seq, _ = x.shape
    past = ckv_past.shape[1]
    t = bsz * seq
    tm, tq, tk, lk, hps, tb, gseq, tmoe = _tiles(bsz, seq, past)
    x2d = x.reshape(t, D_MODEL)

    rope_q = _tile_rows(_rope_tables(past + jnp.arange(seq, dtype=jnp.int32)), tm)
    n_keys = past + seq
    if past == 0:
        qkv, z, gates, q, ckv, kpe, k, vt = _inproj(x2d, p, rope_q, tm, key_tile=tk)
    else:
        assert tk == lk
        qkv, z, gates, q, ckv, kpe = _inproj(x2d, p, rope_q, tm)
        rope_k = _rope_tables(jnp.arange(lk, dtype=jnp.int32))
        k, vt = _kvprep(ckv_past.astype(F32), kpe_past.astype(F32), ckv.reshape(bsz, seq, KV_LORA),
                        kpe.reshape(bsz, seq, QK_ROPE), p, rope_k, lk)
    ckv3, kpe3 = ckv.reshape(bsz, seq, KV_LORA), kpe.reshape(bsz, seq, QK_ROPE)
    mla_out = _flash(q, k, vt, bsz, seq, lk, past, tq, tk, hps, tq // min(tq, tk))

    hist_pad = jnp.concatenate([jnp.zeros((bsz, 8 - (CONV_W - 1), GDN_CONV_DIM), F32), conv_hist.astype(F32)], axis=1)
    gdn_out, s_new = _gdn(qkv, z, gates, hist_pad, s0.astype(F32), p, bsz, seq, tb, gseq)
    assert seq >= CONV_W - 1
    conv_new = qkv.reshape(bsz, seq, GDN_CONV_DIM)[:, seq - (CONV_W - 1):]

    x1, h2, comb = _outproj(x2d, gdn_out, mla_out, p, tm)
    y = _moe(h2, comb, x1, p, tmoe)
    return y.reshape(bsz, seq, D_MODEL), ckv3, kpe3, s_new.astype(s0.dtype), conv_new


def kernel(x_prompt, x_sample, cache_mla_ckv, cache_mla_kpe, state_gdn, state_gdn_conv, ln_mix_w, w_in, gdn_conv_w, gdn_a_log, gdn_dt_bias, gdn_norm_w, mla_q_a_norm_w, mla_w_uq, mla_kv_a_norm_w, mla_w_ukv, mla_q_norm_w, mla_k_norm_w, w_out, ln_ffn_w, w_group_router, b_group_router, w_expert_router, b_expert_router, w_exp_gate, w_exp_up, w_exp_down):
    depth = w_in.shape[0]
    bsz = x_prompt.shape[0]
    y_p, y_s = x_prompt, x_sample
    outs_p, outs_s = [], []
    for l in range(depth):
        p = _prepare_params({
            'ln_mix_w': ln_mix_w[l], 'w_in': w_in[l], 'gdn_conv_w': gdn_conv_w[l],
            'gdn_a_log': gdn_a_log[l], 'gdn_dt_bias': gdn_dt_bias[l], 'gdn_norm_w': gdn_norm_w[l],
            'mla_q_a_norm_w': mla_q_a_norm_w[l], 'mla_w_uq': mla_w_uq[l],
            'mla_kv_a_norm_w': mla_kv_a_norm_w[l], 'mla_w_ukv': mla_w_ukv[l],
            'mla_q_norm_w': mla_q_norm_w[l], 'mla_k_norm_w': mla_k_norm_w[l],
            'w_out': w_out[l], 'ln_ffn_w': ln_ffn_w[l],
            'w_group_router': w_group_router[l], 'b_group_router': b_group_router[l],
            'w_expert_router': w_expert_router[l], 'b_expert_router': b_expert_router[l],
            'w_exp_gate': w_exp_gate[l], 'w_exp_up': w_exp_up[l], 'w_exp_down': w_exp_down[l],
        })
        y_p, *st_p = _hybrid_layer(
            y_p,
            jnp.zeros((bsz, CONV_W - 1, GDN_CONV_DIM), y_p.dtype),
            jnp.zeros((bsz, GDN_HEADS, GDN_DK, GDN_DV), state_gdn.dtype),
            jnp.zeros((bsz, 0, KV_LORA), cache_mla_ckv.dtype),
            jnp.zeros((bsz, 0, QK_ROPE), cache_mla_kpe.dtype),
            p)
        y_s, *st_s = _hybrid_layer(y_s, state_gdn_conv[l], state_gdn[l], cache_mla_ckv[l], cache_mla_kpe[l], p)
        outs_p.append(st_p)
        outs_s.append(st_s)
    stack = lambda outs, i: jnp.stack([o[i] for o in outs])
    return (y_p, y_s,
            stack(outs_p, 0), stack(outs_p, 1), stack(outs_p, 2), stack(outs_p, 3),
            stack(outs_s, 0), stack(outs_s, 1), stack(outs_s, 2), stack(outs_s, 3))
```
